```python
import math
import jax, jax.numpy as jnp
from jax import lax
import numpy as np

D_MODEL = 1024
BATCH = 4
SEQ = 4096
DEPTH = 2

GRID_W = 64
CTX_LEN = 256
EPS = 1e-6
N_HEADS = 8
Q_LORA = 384
KV_LORA = 256
QK_NOPE = 64
QK_ROPE = 32
V_HEAD = 64
ROPE_BASE = 10000.0
Q_BLOCK = 128
ATTN_SCALE = (QK_NOPE + QK_ROPE) ** -0.5
HY_WIDTH = 512
HY_EMB = 17
HY_FILTER_HIDDEN = 64
HY_DECAY_MIN = -math.log(1e-2) / 1.5
HY_DECAY_MAX = -math.log(1e-2) / 0.3
D_FF = 2816
N_EXPERTS = 8
TOP_K = 2
N_DENSE = (DEPTH + 1) // 2
N_MOE = DEPTH // 2
KV_START = Q_LORA
HY_START = KV_START + KV_LORA + QK_ROPE
GATE_START = HY_START + 3 * HY_WIDTH
IN_COLS = GATE_START + 2 * D_MODEL

kernel_name = 'hybrid_mla_hyena_moe_dit_trunk'


def rmsnorm(x, g):
    xf = x.astype(jnp.float32)
    xf = xf * lax.rsqrt(jnp.mean(xf * xf, axis=-1, keepdims=True) + EPS)
    return xf.astype(x.dtype) * g


def adaln(cond, w, b, n_chunks):
    return jnp.split(jax.nn.silu(cond) @ w + b, n_chunks, axis=-1)


def axial_rope_tables(rows, dtype):
    n_freq = QK_ROPE // 4
    inv_freq = ROPE_BASE ** (-jnp.arange(n_freq, dtype=jnp.float32) / n_freq)
    r = jnp.repeat(jnp.arange(rows, dtype=jnp.float32), GRID_W)
    col = jnp.tile(jnp.arange(GRID_W, dtype=jnp.float32), rows)
    ang = jnp.concatenate([r[:, None] * inv_freq, col[:, None] * inv_freq], axis=-1)
    return jnp.cos(ang).astype(dtype), jnp.sin(ang).astype(dtype)


def apply_rope(v, cos, sin):
    half = QK_ROPE // 2
    a, b = v[..., :half], v[..., half:]
    return jnp.concatenate([a * cos - b * sin, b * cos + a * sin], axis=-1)


def mla_queries(p_q, q_norm_g, w_uq, rope):
    bsz, n, _ = p_q.shape
    q = (rmsnorm(p_q, q_norm_g) @ w_uq).reshape(bsz, n, N_HEADS, QK_NOPE + QK_ROPE)
    if rope is None:
        return q
    cos, sin = rope
    return jnp.concatenate([q[..., :QK_NOPE], apply_rope(q[..., QK_NOPE:], cos[:, None], sin[:, None])], axis=-1)


def mla_keys_values(p_kv, kv_norm_g, w_uk, w_uv, rope):
    bsz, n, _ = p_kv.shape
    c_kv = rmsnorm(p_kv[..., :KV_LORA], kv_norm_g)
    k_rope = p_kv[..., KV_LORA:]
    if rope is not None:
        k_rope = apply_rope(k_rope, *rope)
    k_nope = (c_kv @ w_uk).reshape(bsz, n, N_HEADS, QK_NOPE)
    v = (c_kv @ w_uv).reshape(bsz, n, N_HEADS, V_HEAD)
    k = jnp.concatenate([k_nope, jnp.broadcast_to(k_rope[:, :, None, :], (bsz, n, N_HEADS, QK_ROPE))], axis=-1)
    return k, v


def attend(q, k, v):
    s = jnp.einsum('bqhd,bkhd->bhqk', q, k).astype(jnp.float32) * ATTN_SCALE
    p = jax.nn.softmax(s, axis=-1).astype(v.dtype)
    return jnp.einsum('bhqk,bkhd->bqhd', p, v)


def attend_blocked(q, k, v):
    bsz, n, h, dk = q.shape
    qb = q.reshape(bsz, n // Q_BLOCK, Q_BLOCK, h, dk).transpose(1, 0, 2, 3, 4)
    o = lax.map(lambda qi: attend(qi, k, v), qb)
    return o.transpose(1, 0, 2, 3, 4).reshape(bsz, n, h * V_HEAD)


def hyena_filter(n, w1, b1, w2, b2, w3, freq, decay):
    f32 = jnp.float32
    w1, b1, w2, b2, w3, freq, decay = (a.astype(f32) for a in (w1, b1, w2, b2, w3, freq, decay))
    bands = (HY_EMB - 1) // 2
    t = jnp.linspace(0.0, 1.0, n, dtype=f32)[:, None]
    phase = (2.0 * math.pi / n) * jnp.arange(n, dtype=f32)[:, None] * jnp.linspace(1e-4, bands - 1, bands, dtype=f32)
    z = jnp.concatenate([t, jnp.cos(phase), -jnp.sin(phase)], axis=-1)
    h = jnp.sin(freq * (z @ w1 + b1))
    h = jnp.sin(freq * (h @ w2 + b2))
    h = (h @ w3).reshape(n, 2, HY_WIDTH) * jnp.exp(-t[:, :, None] * jnp.abs(decay))
    k = jnp.concatenate([h[:, 0], jnp.zeros((1, HY_WIDTH), f32), h[:0:-1, 1]], axis=0)
    return k / jnp.sum(jnp.abs(k), axis=0, keepdims=True)


def hyena(p, short_w, short_b, w1, b1, w2, b2, w3, freq, decay, bias):
    n = p.shape[1]
    pp = jnp.pad(p, ((0, 0), (1, 1), (0, 0)))
    u = pp[:, :-2] * short_w[0] + pp[:, 1:-1] * short_w[1] + pp[:, 2:] * short_w[2] + short_b
    v, x1, x0 = jnp.split(u, 3, axis=-1)
    z = v * x1
    k = hyena_filter(n, w1, b1, w2, b2, w3, freq, decay)
    zf = z.astype(jnp.float32)
    y = jnp.fft.irfft(jnp.fft.rfft(zf, n=2 * n, axis=1) * jnp.fft.rfft(k, axis=0)[None], n=2 * n, axis=1)[:, :n]
    y = (y + zf * bias.astype(jnp.float32)).astype(z.dtype)
    return y * x0


def token_mixer(p, k_all, v_all, rope, q_norm_g, w_uq, hy, w_br_attn, w_br_hy, w_out):
    bsz, n, _ = p.shape
    q = mla_queries(p[..., :Q_LORA], q_norm_g, w_uq, rope)
    if rope is None:
        o_attn = attend(q, k_all, v_all).reshape(bsz, n, N_HEADS * V_HEAD)
    else:
        o_attn = attend_blocked(q, k_all, v_all)
    o_hy = hyena(p[..., HY_START:GATE_START], *hy)
    g_attn, g_hy = jnp.split(jax.nn.sigmoid(p[..., GATE_START:]), 2, axis=-1)
    merged = g_attn * (o_attn @ w_br_attn) + g_hy * (o_hy @ w_br_hy)
    return merged @ w_out


def swiglu(h, w_gate, w_up, w_down):
    return (jax.nn.silu(h @ w_gate) * (h @ w_up)) @ w_down


def moe_swiglu(h, router, w_gate, w_up, w_down):
    bsz, n, d = h.shape
    t = h.reshape(-1, d)
    logits = (t @ router).astype(jnp.float32)
    top_val, top_idx = lax.top_k(logits, TOP_K)
    top_w = jax.nn.softmax(top_val, axis=-1)
    combine = jnp.sum(jax.nn.one_hot(top_idx, N_EXPERTS, dtype=jnp.float32) * top_w[..., None], axis=1).astype(h.dtype)
    out = jnp.zeros_like(t)
    for e in range(N_EXPERTS):
        out = out + combine[:, e:e + 1] * swiglu(t, w_gate[e], w_up[e], w_down[e])
    return out.reshape(bsz, n, d)


def channel_mixer(h, layer, ffn_w_gate, ffn_w_up, ffn_w_down, moe_router, moe_w_gate, moe_w_up, moe_w_down):
    i = layer // 2
    if layer % 2 == 0:
        return swiglu(h, ffn_w_gate[i], ffn_w_up[i], ffn_w_down[i])
    return moe_swiglu(h, moe_router[i], moe_w_gate[i], moe_w_up[i], moe_w_down[i])


def setup_inputs(seed: int = 0) -> dict:
    key = jax.random.key(seed)
    ks = iter(jax.random.split(key, 48))
    f32 = jnp.float32
    D, L = D_MODEL, DEPTH

    def nrm(shape, fan_in, gain=1.0):
        return jax.random.normal(next(ks), shape, f32) * (gain * fan_in ** -0.5)

    def gain_vec(shape):
        return 1.0 + 0.02 * jax.random.normal(next(ks), shape, f32)

    def small(shape, scale):
        return scale * jax.random.normal(next(ks), shape, f32)

    return {
        'x': jax.random.normal(next(ks), (BATCH, SEQ, D), f32),
        'c': jax.random.normal(next(ks), (BATCH, D), f32),
        'ctx': jax.random.normal(next(ks), (BATCH, CTX_LEN, D), f32),
        'c_ctx': jax.random.normal(next(ks), (D,), f32),
        'w_mod': nrm((L, D, 6 * D), D, 0.5),
        'b_mod': small((L, 6 * D), 0.02),
        'norm1_g': gain_vec((L, D)),
        'norm2_g': gain_vec((L, D)),
        'w_in': nrm((L, D, IN_COLS), D),
        'q_norm_g': gain_vec((L, Q_LORA)),
        'kv_norm_g': gain_vec((L, KV_LORA)),
        'w_uq': nrm((L, Q_LORA, N_HEADS * (QK_NOPE + QK_ROPE)), Q_LORA),
        'w_uk': nrm((L, KV_LORA, N_HEADS * QK_NOPE), KV_LORA),
        'w_uv': nrm((L, KV_LORA, N_HEADS * V_HEAD), KV_LORA),
        'hy_short_w': nrm((L, 3, 3 * HY_WIDTH), 3),
        'hy_short_b': small((L, 3 * HY_WIDTH), 0.02),
        'hy_w1': nrm((L, HY_EMB, HY_FILTER_HIDDEN), HY_EMB),
        'hy_b1': small((L, HY_FILTER_HIDDEN), 0.1),
        'hy_w2': nrm((L, HY_FILTER_HIDDEN, HY_FILTER_HIDDEN), HY_FILTER_HIDDEN),
        'hy_b2': small((L, HY_FILTER_HIDDEN), 0.1),
        'hy_w3': nrm((L, HY_FILTER_HIDDEN, 2 * HY_WIDTH), HY_FILTER_HIDDEN),
        'hy_freq': gain_vec((L, HY_FILTER_HIDDEN)),
        'hy_decay': jax.random.uniform(next(ks), (L, 2, HY_WIDTH), f32, minval=HY_DECAY_MIN, maxval=HY_DECAY_MAX),
        'hy_bias': jax.random.normal(next(ks), (L, HY_WIDTH), f32),
        'w_br_attn': nrm((L, N_HEADS * V_HEAD, D), N_HEADS * V_HEAD),
        'w_br_hy': nrm((L, HY_WIDTH, D), HY_WIDTH),
        'w_out': nrm((L, D, D), D),
        'ffn_w_gate': nrm((N_DENSE, D, D_FF), D),
        'ffn_w_up': nrm((N_DENSE, D, D_FF), D),
        'ffn_w_down': nrm((N_DENSE, D_FF, D), D_FF),
        'moe_router': nrm((N_MOE, D, N_EXPERTS), D),
        'moe_w_gate': nrm((N_MOE, N_EXPERTS, D, D_FF), D),
        'moe_w_up': nrm((N_MOE, N_EXPERTS, D, D_FF), D),
        'moe_w_down': nrm((N_MOE, N_EXPERTS, D_FF, D), D_FF),
        'final_g': gain_vec((D,)),
    }


def reference(x, c, ctx, c_ctx, w_mod, b_mod, norm1_g, norm2_g, w_in, q_norm_g, kv_norm_g, w_uq, w_uk, w_uv,
              hy_short_w, hy_short_b, hy_w1, hy_b1, hy_w2, hy_b2, hy_w3, hy_freq, hy_decay, hy_bias,
              w_br_attn, w_br_hy, w_out, ffn_w_gate, ffn_w_up, ffn_w_down,
              moe_router, moe_w_gate, moe_w_up, moe_w_down, final_g):
    rows = x.shape[1] // GRID_W
    rope = axial_rope_tables(rows, x.dtype)
    for layer in range(DEPTH):
        last = layer == DEPTH - 1
        sh1, sc1, g1, sh2, sc2, g2 = (m[:, None, :] for m in adaln(c, w_mod[layer], b_mod[layer], 6))
        hy = (hy_short_w[layer], hy_short_b[layer], hy_w1[layer], hy_b1[layer], hy_w2[layer], hy_b2[layer],
              hy_w3[layer], hy_freq[layer], hy_decay[layer], hy_bias[layer])

        h_lat = rmsnorm(x, norm1_g[layer]) * (1 + sc1) + sh1
        p_lat = h_lat @ w_in[layer]
        k_lat, v_lat = mla_keys_values(p_lat[..., KV_START:HY_START], kv_norm_g[layer], w_uk[layer], w_uv[layer], rope)

        if last:
            csh1, csc1 = adaln(c_ctx, w_mod[layer][:, :2 * D_MODEL], b_mod[layer][:2 * D_MODEL], 2)
            h_ctx = rmsnorm(ctx, norm1_g[layer]) * (1 + csc1) + csh1
            p_ctx_kv = h_ctx @ w_in[layer][:, KV_START:HY_START]
            k_ctx, v_ctx = mla_keys_values(p_ctx_kv, kv_norm_g[layer], w_uk[layer], w_uv[layer], None)
        else:
            csh1, csc1, cg1, csh2, csc2, cg2 = adaln(c_ctx, w_mod[layer], b_mod[layer], 6)
            h_ctx = rmsnorm(ctx, norm1_g[layer]) * (1 + csc1) + csh1
            p_ctx = h_ctx @ w_in[layer]
            k_ctx, v_ctx = mla_keys_values(p_ctx[..., KV_START:HY_START], kv_norm_g[layer], w_uk[layer], w_uv[layer], None)
            mix_ctx = token_mixer(p_ctx, k_ctx, v_ctx, None, q_norm_g[layer], w_uq[layer], hy,
                                  w_br_attn[layer], w_br_hy[layer], w_out[layer])

        k_all = jnp.concatenate([k_ctx, k_lat], axis=1)
        v_all = jnp.concatenate([v_ctx, v_lat], axis=1)
        mix_lat = token_mixer(p_lat, k_all, v_all, rope, q_norm_g[layer], w_uq[layer], hy,
                              w_br_attn[layer], w_br_hy[layer], w_out[layer])
        x = x + g1 * mix_lat
        x = x + g2 * channel_mixer(rmsnorm(x, norm2_g[layer]) * (1 + sc2) + sh2, layer,
                                   ffn_w_gate, ffn_w_up, ffn_w_down, moe_router, moe_w_gate, moe_w_up, moe_w_down)
        if not last:
            ctx = ctx + cg1 * mix_ctx
            ctx = ctx + cg2 * channel_mixer(rmsnorm(ctx, norm2_g[layer]) * (1 + csc2) + csh2, layer,
                                            ffn_w_gate, ffn_w_up, ffn_w_down, moe_router, moe_w_gate, moe_w_up, moe_w_down)
    return rmsnorm(x, final_g)
```

```python
import functools
import math

import numpy as np
import jax
import jax.numpy as jnp
from jax import lax
from jax.experimental import pallas as pl
from jax.experimental.pallas import tpu as pltpu

F32 = jnp.float32
BF16 = jnp.bfloat16
HIGHEST = lax.Precision.HIGHEST

D_MODEL = 1024
GRID_W = 64
EPS = 1e-6
N_HEADS = 8
Q_LORA = 384
KV_LORA = 256
QK_NOPE = 64
QK_ROPE = 32
V_HEAD = 64
ROPE_BASE = 10000.0
ATTN_SCALE = (QK_NOPE + QK_ROPE) ** -0.5
HY_WIDTH = 512
HY_EMB = 17
HY_HIDDEN = 64
D_FF = 2816
N_EXPERTS = 8
KV_START = Q_LORA
HY_START = KV_START + KV_LORA + QK_ROPE
GATE_START = HY_START + 3 * HY_WIDTH

LANES = 128
SUBLANES = 8
HEAD_PAD = LANES
VMEM_LIMIT = 56 * 2**20

ROW_TILE = 256
FFN_ROW_TILE = 512
FFN_CHUNK = 1408
KV_TILE = 256
DFT_P = 64
DFT_Q = 128
COLSLOT_BQ = 8


def _cparams(*sem):
    return pltpu.CompilerParams(dimension_semantics=sem, vmem_limit_bytes=VMEM_LIMIT)


def _dot(a, b):
    return jnp.dot(a, b, preferred_element_type=F32)


def _dot_hi(a, b):
    return jnp.dot(a, b, precision=HIGHEST, preferred_element_type=F32)


def _rms(xf, g):
    return xf * lax.rsqrt(jnp.mean(xf * xf, axis=-1, keepdims=True) + EPS) * g


def _full(shape):
    nd = len(shape)
    return pl.BlockSpec(shape, lambda *_: (0,) * nd)


def _adaln_body(c_ref, w_ref, b_ref, o_ref):
    c = c_ref[...]
    o_ref[...] = _dot_hi(c * jax.nn.sigmoid(c), w_ref[...]) + b_ref[...]


def _adaln(cond8, w, b):
    d, n = w.shape
    return pl.pallas_call(
        _adaln_body,
        grid=(n // d,),
        in_specs=[_full((SUBLANES, d)), pl.BlockSpec((d, d), lambda j: (0, j)),
                  pl.BlockSpec((1, d), lambda j: (0, j))],
        out_specs=pl.BlockSpec((SUBLANES, d), lambda j: (0, j)),
        out_shape=jax.ShapeDtypeStruct((SUBLANES, n), F32),
        compiler_params=_cparams("arbitrary"),
        name="adaln",
    )(cond8, w, b.reshape(1, n))


def _mod_spec(chunk):
    return pl.BlockSpec((SUBLANES, D_MODEL), lambda t, *_: (0, chunk))


def _mod_row(ref, group):
    return ref[pl.ds(group, 1), :]


def _inproj_body(tiles_per_group, fixed_group, use_rope, want_q, want_hg, *refs):
    it = iter(refs)
    x_ref, sh_ref, sc_ref, g1_ref = next(it), next(it), next(it), next(it)
    wkv_ref, kvg_ref, wuk_ref, wuv_ref, wkr_ref = next(it), next(it), next(it), next(it), next(it)
    if want_q:
        wq_ref, qg_ref, wuqa_ref, wuqb_ref = next(it), next(it), next(it), next(it)
    if want_hg:
        why_ref, wgate_ref = next(it), next(it)
    if use_rope:
        cos_ref, sin_ref = next(it), next(it)
    k_out, v_out = next(it), next(it)
    if want_q:
        q_out = next(it)
    if want_hg:
        phy_out, gate_out = next(it), next(it)

    group = fixed_group if fixed_group is not None else pl.program_id(0) // tiles_per_group
    xf = x_ref[...]
    h = _rms(xf, g1_ref[...]) * (1.0 + _mod_row(sc_ref, group)) + _mod_row(sh_ref, group)
    h = h.astype(BF16)
    if use_rope:
        cos, sin = cos_ref[...], sin_ref[...]

    def rope(a, b):
        return a * cos + b * sin if use_rope else a

    ckv = _rms(_dot(h, wkv_ref[...]), kvg_ref[...]).astype(BF16)
    v_out[...] = _dot(ckv, wuv_ref[...]).astype(BF16)
    k_nope = _dot(ckv, wuk_ref[...])
    kr = _dot(h, wkr_ref[...])
    k_rope = rope(kr[:, :HEAD_PAD], kr[:, HEAD_PAD:])
    for hd in range(N_HEADS):
        sl = slice(hd * HEAD_PAD, (hd + 1) * HEAD_PAD)
        k_out[:, sl] = (k_nope[:, sl] + k_rope).astype(BF16)
    if want_q:
        qn = _rms(_dot(h, wq_ref[...]), qg_ref[...]).astype(BF16)
        qa = _dot(qn, wuqa_ref[...])
        qb = _dot(qn, wuqb_ref[...]) if use_rope else None
        for hd in range(N_HEADS):
            sl = slice(hd * HEAD_PAD, (hd + 1) * HEAD_PAD)
            q_out[:, sl] = (rope(qa[:, sl], None if qb is None else qb[:, sl]) * ATTN_SCALE).astype(BF16)
    if want_hg:
        n_hy = why_ref.shape[1]
        for c0 in range(0, n_hy, 512):
            phy_out[:, c0:c0 + 512] = _dot(h, why_ref[:, c0:c0 + 512])
        n_g = wgate_ref.shape[1]
        for c0 in range(0, n_g, 512):
            gate_out[:, c0:c0 + 512] = jax.nn.sigmoid(_dot(h, wgate_ref[:, c0:c0 + 512]))


def _inproj(x, mod, lw, *, tiles_per_group, fixed_group, rope_tabs, want_q, want_hg):
    rows = x.shape[0]
    nt = rows // ROW_TILE
    use_rope = rope_tabs is not None
    row_spec = lambda n: pl.BlockSpec((ROW_TILE, n), lambda t: (t, 0))
    ins = [x, mod, mod, lw["norm1_g"], lw["w_kv"], lw["kv_norm_g"], lw["w_uk"], lw["w_uv"], lw["w_kr"]]
    specs = [row_spec(D_MODEL), _mod_spec(0), _mod_spec(1), _full((1, D_MODEL)),
             _full(lw["w_kv"].shape), _full((1, KV_LORA)), _full(lw["w_uk"].shape), _full(lw["w_uv"].shape),
             _full(lw["w_kr"].shape)]
    if want_q:
        ins += [lw["w_q"], lw["q_norm_g"], lw["w_uq_a"], lw["w_uq_b"]]
        specs += [_full(lw["w_q"].shape), _full((1, Q_LORA)), _full(lw["w_uq_a"].shape), _full(lw["w_uq_b"].shape)]
    if want_hg:
        ins += [lw["w_hy"], lw["w_gate"]]
        specs += [_full(lw["w_hy"].shape), _full(lw["w_gate"].shape)]
    if use_rope:
        seq_tiles = rope_tabs[0].shape[0] // ROW_TILE
        ins += list(rope_tabs)
        specs += [pl.BlockSpec((ROW_TILE, HEAD_PAD), lambda t: (t % seq_tiles, 0))] * 2
    hp = N_HEADS * HEAD_PAD
    out_shape = [jax.ShapeDtypeStruct((rows, hp), BF16), jax.ShapeDtypeStruct((rows, N_HEADS * V_HEAD), BF16)]
    out_specs = [row_spec(hp), row_spec(N_HEADS * V_HEAD)]
    if want_q:
        out_shape.append(jax.ShapeDtypeStruct((rows, hp), BF16))
        out_specs.append(row_spec(hp))
    if want_hg:
        out_shape += [jax.ShapeDtypeStruct((rows, 3 * HY_WIDTH), F32), jax.ShapeDtypeStruct((rows, 2 * D_MODEL), F32)]
        out_specs += [row_spec(3 * HY_WIDTH), row_spec(2 * D_MODEL)]
    outs = pl.pallas_call(
        functools.partial(_inproj_body, tiles_per_group, fixed_group, use_rope, want_q, want_hg),
        grid=(nt,), in_specs=specs, out_specs=out_specs, out_shape=out_shape,
        compiler_params=_cparams("arbitrary"), name="inproj",
    )(*ins)
    res = {"k": outs[0], "v": outs[1]}
    i = 2
    if want_q:
        res["q"] = outs[i]
        i += 1
    if want_hg:
        res["p_hy"], res["gates"] = outs[i], outs[i + 1]
    return res


def _attn_body(n_lat_blocks, *refs):
    if n_lat_blocks:
        q_ref, kc_ref, vc_ref, kl_ref, vl_ref, o_ref = refs
    else:
        q_ref, kc_ref, vc_ref, o_ref = refs
    tq = q_ref.shape[0]
    q = q_ref[...]

    def step(kblk, vblk, carry):
        out = []
        for hd in range(2):
            m, l, acc = carry[hd]
            sl = slice(hd * HEAD_PAD, (hd + 1) * HEAD_PAD)
            s = lax.dot_general(q[:, sl], kblk[:, sl], (((1,), (1,)), ((), ())), preferred_element_type=F32)
            m_new = jnp.maximum(m, jnp.max(s, axis=-1, keepdims=True))
            p = jnp.exp(s - m_new)
            alpha = jnp.exp(m - m_new)
            l = alpha * l + jnp.sum(p, axis=-1, keepdims=True)
            acc = alpha * acc + _dot(p.astype(BF16), vblk)
            out.append((m_new, l, acc))
        return tuple(out)

    init = tuple((jnp.full((tq, 1), -jnp.inf, F32), jnp.zeros((tq, 1), F32), jnp.zeros((tq, 2 * V_HEAD), F32))
                 for _ in range(2))
    carry = step(kc_ref[0], vc_ref[0], init)
    if n_lat_blocks:
        def body(i, c):
            r0 = pl.multiple_of(i * KV_TILE, KV_TILE)
            return step(kl_ref[0, pl.ds(r0, KV_TILE), :], vl_ref[0, pl.ds(r0, KV_TILE), :], c)
        carry = lax.fori_loop(0, n_lat_blocks, body, carry)
    (_, l0, a0), (_, l1, a1) = carry
    lane = lax.broadcasted_iota(jnp.int32, (tq, 2 * V_HEAD), 1)
    o_ref[...] = jnp.where(lane < V_HEAD, a0 / l0, a1 / l1).astype(BF16)


def _attention(q, k_ctx, v_ctx, k_lat, v_lat, batch):
    rows = q.shape[0]
    qt_per_b = rows // batch // ROW_TILE
    ctx_len = k_ctx.shape[1]
    pair_w = 2 * HEAD_PAD
    ins = [q, k_ctx, v_ctx]
    specs = [pl.BlockSpec((ROW_TILE, pair_w), lambda b, hp, t: (b * qt_per_b + t, hp)),
             pl.BlockSpec((1, ctx_len, pair_w), lambda b, hp, t: (b, 0, hp)),
             pl.BlockSpec((1, ctx_len, 2 * V_HEAD), lambda b, hp, t: (b, 0, hp))]
    n_lat_blocks = 0
    if k_lat is not None:
        lat_len = k_lat.shape[1]
        n_lat_blocks = lat_len // KV_TILE
        ins += [k_lat, v_lat]
        specs += [pl.BlockSpec((1, lat_len, pair_w), lambda b, hp, t: (b, 0, hp)),
                  pl.BlockSpec((1, lat_len, 2 * V_HEAD), lambda b, hp, t: (b, 0, hp))]
    return pl.pallas_call(
        functools.partial(_attn_body, n_lat_blocks),
        grid=(batch, N_HEADS // 2, qt_per_b), in_specs=specs,
        out_specs=pl.BlockSpec((ROW_TILE, 2 * V_HEAD), lambda b, hp, t: (b * qt_per_b + t, hp)),
        out_shape=jax.ShapeDtypeStruct((rows, N_HEADS * V_HEAD), BF16),
        compiler_params=_cparams("arbitrary", "arbitrary", "arbitrary"), name="attention",
    )(*ins)


def _hypre_body(seq_tiles, n_chunks, p_ref, prev_ref, next_ref, w_ref, b_ref, z_ref, x0_ref, *zt_ref):
    j = pl.program_id(0) % seq_tiles
    p = p_ref[...]
    tm = p.shape[0]
    row = lax.broadcasted_iota(jnp.int32, (tm, 1), 0)
    prev_row = jnp.where(j != 0, prev_ref[SUBLANES - 1:SUBLANES, :], 0.0)
    next_row = jnp.where(j != seq_tiles - 1, next_ref[0:1, :], 0.0)
    up = jnp.where(row == 0, prev_row, pltpu.roll(p, 1, 0))
    dn = jnp.where(row == tm - 1, next_row, pltpu.roll(p, tm - 1, 0))
    u = up * w_ref[0:1, :] + p * w_ref[1:2, :] + dn * w_ref[2:3, :] + b_ref[...]
    c = HY_WIDTH
    z = u[:, :c] * u[:, c:2 * c]
    z_ref[...] = z
    x0_ref[...] = u[:, 2 * c:]
    if zt_ref:
        q = tm // n_chunks
        for a in range(n_chunks):
            zt_ref[0][0, :, a * c:(a + 1) * c] = z[a * q:(a + 1) * q, :]


def _hyena_pre(p_hy, short_w, short_b, batch, dft_q):
    rows = p_hy.shape[0]
    n = rows // batch
    seq_tiles = n // ROW_TILE
    nt = rows // ROW_TILE
    c3 = 3 * HY_WIDTH
    halo = ROW_TILE // SUBLANES
    last8 = rows // SUBLANES - 1
    specs = [pl.BlockSpec((ROW_TILE, c3), lambda t: (t, 0)),
             pl.BlockSpec((SUBLANES, c3), lambda t: (jnp.maximum(t * halo - 1, 0), 0)),
             pl.BlockSpec((SUBLANES, c3), lambda t: (jnp.minimum((t + 1) * halo, last8), 0)),
             _full((3, c3)), _full((1, c3))]
    out_shape = [jax.ShapeDtypeStruct((rows, HY_WIDTH), F32)] * 2
    out_specs = [pl.BlockSpec((ROW_TILE, HY_WIDTH), lambda t: (t, 0))] * 2
    n_chunks = 1
    if dft_q is not None:
        n_chunks = ROW_TILE // dft_q
        half_p = n // dft_q
        out_shape.append(jax.ShapeDtypeStruct((batch // 2, dft_q, 2 * half_p * HY_WIDTH), F32))

        def zt_map(t):
            b, jt = t // seq_tiles, t % seq_tiles
            return (b // 2, 0, (b % 2) * seq_tiles + jt)
        out_specs.append(pl.BlockSpec((1, dft_q, n_chunks * HY_WIDTH), zt_map))
    return pl.pallas_call(
        functools.partial(_hypre_body, seq_tiles, n_chunks),
        grid=(nt,), in_specs=specs, out_specs=out_specs, out_shape=out_shape,
        compiler_params=_cparams("arbitrary"), name="hyena_pre",
    )(p_hy, p_hy, p_hy, short_w, short_b.reshape(1, c3))


def _filter_body(emb_ref, w1_ref, b1_ref, w2_ref, b2_ref, w3_ref, freq_ref, decay_ref, h0_ref, h1_ref, s_ref):
    a = pl.program_id(0)
    emb = emb_ref[...]
    freq = freq_ref[...]
    h = jnp.sin(freq * (_dot_hi(emb, w1_ref[...]) + b1_ref[...]))
    h = jnp.sin(freq * (_dot_hi(h, w2_ref[...]) + b2_ref[...]))
    h = _dot_hi(h, w3_ref[...]) * jnp.exp(-emb[:, 0:1] * jnp.abs(decay_ref[...]))
    c = HY_WIDTH
    row = lax.broadcasted_iota(jnp.int32, (emb.shape[0], 1), 0)
    h0 = h[:, :c]
    h1 = jnp.where(jnp.logical_and(a == 0, row == 0), 0.0, h[:, c:])
    h0_ref[0] = h0
    h1_ref[0] = h1
    part = jnp.sum(jnp.abs(h0) + jnp.abs(h1), axis=0, keepdims=True)

    @pl.when(a == 0)
    def _():
        s_ref[...] = jnp.zeros_like(s_ref)
    s_ref[...] += part


def _hyena_filter(n, q, lw):
    f32 = F32
    bands = (HY_EMB - 1) // 2
    t = jnp.linspace(0.0, 1.0, n, dtype=f32)[:, None]
    phase = (2.0 * math.pi / n) * jnp.arange(n, dtype=f32)[:, None] * jnp.linspace(1e-4, bands - 1, bands, dtype=f32)
    emb = jnp.concatenate([t, jnp.cos(phase), -jnp.sin(phase), jnp.zeros((n, 32 - HY_EMB), f32)], axis=-1)
    w1 = jnp.concatenate([lw["hy_w1"], jnp.zeros((32 - HY_EMB, HY_HIDDEN), f32)], axis=0)
    c = HY_WIDTH
    slots = n // q
    row = lambda v: v.reshape(1, -1)
    return pl.pallas_call(
        _filter_body,
        grid=(slots,),
        in_specs=[pl.BlockSpec((q, 32), lambda a: (a, 0)), _full((32, HY_HIDDEN)), _full((1, HY_HIDDEN)),
                  _full((HY_HIDDEN, HY_HIDDEN)), _full((1, HY_HIDDEN)), _full((HY_HIDDEN, 2 * c)),
                  _full((1, HY_HIDDEN)), _full((1, 2 * c))],
        out_specs=[pl.BlockSpec((1, q, c), lambda a: (0, 0, a)), pl.BlockSpec((1, q, c), lambda a: (0, 0, a)),
                   _full((1, c))],
        out_shape=[jax.ShapeDtypeStruct((1, q, slots * c), f32)] * 2 + [jax.ShapeDtypeStruct((1, c), f32)],
        compiler_params=_cparams("arbitrary"), name="hyena_filter",
    )(emb, w1, row(lw["hy_b1"]), lw["hy_w2"], row(lw["hy_b2"]), lw["hy_w3"], row(lw["hy_freq"]),
      row(lw["hy_decay"]))


def _colslot_body(nparts, bq, c, *refs):
    x_refs, t_refs, o_ref = refs[:nparts], refs[nparts:2 * nparts], refs[-1]
    for j in range(bq):
        acc = None
        for x_ref, t_ref in zip(x_refs, t_refs):
            r = _dot_hi(t_ref[j], x_ref[0, j])
            acc = r if acc is None else acc + r
        o_ref[0, :, j * c:(j + 1) * c] = acc


def _colslot_matmul(xs, tabs, c=HY_WIDTH):
    g, q = xs[0].shape[0], xs[0].shape[1]
    m = tabs[0].shape[1]
    bq = COLSLOT_BQ
    xs4 = [x.reshape(g, q, -1, c) for x in xs]
    specs = [pl.BlockSpec((1, bq, x.shape[2], c), lambda gi, qi: (gi, qi, 0, 0)) for x in xs4]
    specs += [pl.BlockSpec((bq, m, t.shape[2]), lambda gi, qi: (qi, 0, 0)) for t in tabs]
    return pl.pallas_call(
        functools.partial(_colslot_body, len(xs), bq, c),
        grid=(g, q // bq), in_specs=specs,
        out_specs=pl.BlockSpec((1, m, bq * c), lambda gi, qi: (gi, 0, qi)),
        out_shape=jax.ShapeDtypeStruct((g, m, q * c), F32),
        compiler_params=_cparams("arbitrary", "arbitrary"), name="dft_stride_stage",
    )(*xs4, *tabs)


def _spec_mid_body(g_ref, m3_ref, m3i_ref, k_ref, hr_ref, hi_ref):
    q = hr_ref.shape[1]
    x = _dot_hi(m3_ref[...], g_ref[0, :, 0].reshape(2 * q, -1))
    kh = k_ref[0]
    xr, xi, kr, ki = x[:q], x[q:], kh[:q], kh[q:]
    y = jnp.concatenate([xr * kr - xi * ki, xr * ki + xi * kr], axis=0)
    h = _dot_hi(m3i_ref[...], y)
    hr_ref[0] = h[:q]
    hi_ref[0] = h[q:]


def _spec_mid(gf, m3, m3i, khat, p, q, c=HY_WIDTH):
    bp = gf.shape[0]
    g5 = gf.reshape(bp, 2, p, q, c)
    return pl.pallas_call(
        _spec_mid_body,
        grid=(bp, p),
        in_specs=[pl.BlockSpec((1, 2, 1, q, c), lambda b, ci: (b, 0, ci, 0, 0)), _full((2 * q, 2 * q)),
                  _full((2 * q, 2 * q)), pl.BlockSpec((1, 2 * q, c), lambda b, ci: (ci, 0, 0))],
        out_specs=[pl.BlockSpec((1, q, c), lambda b, ci: (b, 0, ci))] * 2,
        out_shape=[jax.ShapeDtypeStruct((bp, q, p * c), F32)] * 2,
        compiler_params=_cparams("arbitrary", "arbitrary"), name="dft_mid",
    )(g5, m3, m3i, khat)


def _filter_spec_body(g_ref, m3k_ref, s_ref, k_ref):
    q4 = m3k_ref.shape[1]
    k_ref[0] = _dot_hi(m3k_ref[...], g_ref[0, :, 0].reshape(q4, -1)) / s_ref[...]


def _filter_spec(gk, m3k, s, p, q, c=HY_WIDTH):
    g5 = gk.reshape(1, 4, p, q, c)
    return pl.pallas_call(
        _filter_spec_body,
        grid=(p,),
        in_specs=[pl.BlockSpec((1, 4, 1, q, c), lambda ci: (0, 0, ci, 0, 0)), _full((2 * q, 4 * q)), _full((1, c))],
        out_specs=pl.BlockSpec((1, 2 * q, c), lambda ci: (ci, 0, 0)),
        out_shape=jax.ShapeDtypeStruct((p, 2 * q, c), F32),
        compiler_params=_cparams("arbitrary"), name="filter_spectrum",
    )(g5, m3k, s)


def _direct_conv_body(z_ref, h0_ref, h1_ref, s_ref, tf_ref, tk_ref, ti_ref, o_ref):
    n2 = tf_ref.shape[0] // 2
    kh = _dot_hi(tk_ref[...], jnp.concatenate([h0_ref[0], h1_ref[0]], axis=0)) / s_ref[...]
    x = _dot_hi(tf_ref[...], z_ref[0])
    xr, xi, kr, ki = x[:n2], x[n2:], kh[:n2], kh[n2:]
    y = jnp.concatenate([xr * kr - xi * ki, xr * ki + xi * kr], axis=0)
    o_ref[0] = _dot_hi(ti_ref[...], y)


def _direct_conv(z, h0, h1, s, n, c=HY_WIDTH):
    bp = z.shape[0] // (2 * n)
    tf, tk, ti = _direct_tables(n)
    out = pl.pallas_call(
        _direct_conv_body,
        grid=(bp,),
        in_specs=[pl.BlockSpec((1, 2 * n, c), lambda b: (b, 0, 0)), _full((1, n, c)), _full((1, n, c)), _full((1, c)),
                  _full(tf.shape), _full(tk.shape), _full(ti.shape)],
        out_specs=pl.BlockSpec((1, 2 * n, c), lambda b: (b, 0, 0)),
        out_shape=jax.ShapeDtypeStruct((bp, 2 * n, c), F32),
        compiler_params=_cparams("arbitrary"), name="direct_conv",
    )(z.reshape(bp, 2 * n, c), h0, h1, s, tf, tk, ti)
    return out.reshape(bp * 2 * n, c)


def _cis(m, n_total):
    ang = (2.0 * np.pi / n_total) * (m % n_total).astype(np.float64)
    return np.cos(ang), -np.sin(ang)


@functools.lru_cache(maxsize=None)
def _direct_tables(n):
    nn = 2 * n
    f = np.arange(nn)[:, None]
    t = np.arange(n)[None, :]
    cr, ci = _cis(f * t, nn)
    tf = np.block([[cr, -ci], [ci, cr]])
    tk = np.block([[cr, cr], [ci, -ci]])
    ti = np.block([[cr.T, ci.T], [-ci.T, cr.T]]) / nn
    return tuple(a.astype(np.float32) for a in (tf, tk, ti))


@functools.lru_cache(maxsize=None)
def _twostage_tables(p, q):
    nn = p * q
    hp = p // 2
    s = np.arange(q)[:, None, None]
    c = np.arange(p)[None, :, None]
    a = np.arange(hp)[None, None, :]
    tr, ti = _cis(c * (q * a + s), nn)
    z = np.zeros_like(tr)
    t_data = np.concatenate([np.concatenate([tr, -ti], 2), np.concatenate([ti, tr], 2)], 1)
    t_k0 = np.concatenate([tr, ti, z, z], 1)
    t_k1 = np.concatenate([z, z, tr, -ti], 1)
    trt, tit = np.swapaxes(tr, 1, 2) / nn, np.swapaxes(ti, 1, 2) / nn
    t_inv_r = np.concatenate([trt, -tit], 1)
    t_inv_i = np.concatenate([tit, trt], 1)
    d = np.arange(q)[:, None]
    b = np.arange(q)[None, :]
    fr, fi = _cis(d * b, q)
    m3 = np.block([[fr, -fi], [fi, fr]])
    m3i = np.block([[fr, fi], [-fi, fr]])
    m3k = np.block([[fr, -fi, fr, fi], [fi, fr, -fi, fr]])
    return tuple(x.astype(np.float32) for x in (t_data, t_k0, t_k1, t_inv_r, t_inv_i, m3, m3i, m3k))


def _long_conv_spectrum(n, lw):
    p, q = DFT_P, DFT_Q
    h0, h1, s = _hyena_filter(n, q, lw)
    _, t_k0, t_k1, _, _, _, _, m3k = _twostage_tables(p, q)
    gk = _colslot_matmul([h0, h1], [t_k0, t_k1])
    return _filter_spec(gk, m3k, s, p, q)


def _long_conv(zt, khat):
    p, q = DFT_P, DFT_Q
    t_data, _, _, t_inv_r, t_inv_i, m3, m3i, _ = _twostage_tables(p, q)
    gf = _colslot_matmul([zt], [t_data])
    hr, hi = _spec_mid(gf, m3, m3i, khat, p, q)
    y = _colslot_matmul([hr, hi], [t_inv_r, t_inv_i])
    return y.reshape(-1, HY_WIDTH)


def _merge_body(tiles_per_group, fixed_group, x_ref, g1_ref, oa_ref, conv_ref, z_ref, x0_ref, gate_ref, bias_ref,
                wba_ref, wbh_ref, wout_ref, o_ref):
    group = fixed_group if fixed_group is not None else pl.program_id(0) // tiles_per_group
    o_hy = ((conv_ref[...] + z_ref[...] * bias_ref[...]) * x0_ref[...]).astype(BF16)
    d = D_MODEL
    merged = gate_ref[:, :d] * _dot(oa_ref[...], wba_ref[...]) + gate_ref[:, d:] * _dot(o_hy, wbh_ref[...])
    mix = _dot(merged.astype(BF16), wout_ref[...])
    o_ref[...] = x_ref[...] + _mod_row(g1_ref, group) * mix


def _merge(x, mod, o_attn, conv, z, x0, gates, lw, *, tiles_per_group, fixed_group):
    rows = x.shape[0]
    row_spec = lambda n: pl.BlockSpec((ROW_TILE, n), lambda t: (t, 0))
    c = HY_WIDTH
    return pl.pallas_call(
        functools.partial(_merge_body, tiles_per_group, fixed_group),
        grid=(rows // ROW_TILE,),
        in_specs=[row_spec(D_MODEL), _mod_spec(2), row_spec(N_HEADS * V_HEAD), row_spec(c), row_spec(c), row_spec(c),
                  row_spec(2 * D_MODEL), _full((1, c)), _full(lw["w_br_attn"].shape), _full(lw["w_br_hy"].shape),
                  _full(lw["w_out"].shape)],
        out_specs=row_spec(D_MODEL),
        out_shape=jax.ShapeDtypeStruct((rows, D_MODEL), F32),
        compiler_params=_cparams("arbitrary"), name="merge_out",
    )(x, mod, o_attn, conv, z, x0, gates, lw["hy_bias"], lw["w_br_attn"], lw["w_br_hy"], lw["w_out"])


def _ffn_body(tiles_per_group, fixed_group, n_experts, final_norm, *refs):
    it = iter(refs)
    x_ref, sh_ref, sc_ref, g2_ref, n2_ref = next(it), next(it), next(it), next(it), next(it)
    if n_experts:
        router_ref = next(it)
    wg_ref, wu_ref, wd_ref = next(it), next(it), next(it)
    if final_norm:
        fg_ref = next(it)
    o_ref, h_scr, acc_scr = next(it), next(it), next(it)
    if n_experts:
        comb_scr = next(it)
    group = fixed_group if fixed_group is not None else pl.program_id(0) // tiles_per_group
    if n_experts:
        e, ci = pl.program_id(1), pl.program_id(2)
        first = jnp.logical_and(e == 0, ci == 0)
        last = jnp.logical_and(e == n_experts - 1, ci == pl.num_programs(2) - 1)
    else:
        ci = pl.program_id(1)
        first, last = ci == 0, ci == pl.num_programs(1) - 1

    @pl.when(first)
    def _():
        h = _rms(x_ref[...], n2_ref[...]) * (1.0 + _mod_row(sc_ref, group)) + _mod_row(sh_ref, group)
        h_scr[...] = h.astype(BF16)
        acc_scr[...] = jnp.zeros_like(acc_scr)
        if n_experts:
            logits = _dot_hi(h, router_ref[...])
            lane = lax.broadcasted_iota(jnp.int32, logits.shape, 1)
            logits = jnp.where(lane < n_experts, logits, -jnp.inf)
            m1 = jnp.max(logits, axis=-1, keepdims=True)
            i1 = jnp.min(jnp.where(logits == m1, lane, LANES), axis=-1, keepdims=True)
            rest = jnp.where(lane == i1, -jnp.inf, logits)
            m2 = jnp.max(rest, axis=-1, keepdims=True)
            i2 = jnp.min(jnp.where(rest == m2, lane, LANES), axis=-1, keepdims=True)
            e2 = jnp.exp(m2 - m1)
            w1 = 1.0 / (1.0 + e2)
            comb_scr[...] = jnp.where(lane == i1, w1, 0.0) + jnp.where(lane == i2, e2 * w1, 0.0)

    h = h_scr[...]
    gate = _dot(h, wg_ref[0])
    up = _dot(h, wu_ref[0])
    y = _dot((gate * jax.nn.sigmoid(gate) * up).astype(BF16), wd_ref[0])
    if n_experts:
        lane = lax.broadcasted_iota(jnp.int32, comb_scr.shape, 1)
        y = y * jnp.sum(jnp.where(lane == e, comb_scr[...], 0.0), axis=-1, keepdims=True)
    acc_scr[...] += y

    @pl.when(last)
    def _():
        out = x_ref[...] + _mod_row(g2_ref, group) * acc_scr[...]
        if final_norm:
            out = _rms(out, fg_ref[...])
        o_ref[...] = out


def _ffn(x, mod, norm_g, w_gate, w_up, w_down, *, tiles_per_group, fixed_group, router=None, final_g=None,
         row_tile=FFN_ROW_TILE):
    rows = x.shape[0]
    tm = min(row_tile, rows)
    n_e = w_gate.shape[0] if router is not None else 0
    n_chunks = D_FF // FFN_CHUNK
    if n_e:
        grid = (rows // tm, n_e, n_chunks)
        wmap_in = lambda t, e, ci: (e, 0, ci)
        wmap_out = lambda t, e, ci: (e, ci, 0)
        sem = ("arbitrary",) * 3
    else:
        grid = (rows // tm, n_chunks)
        wmap_in = lambda t, ci: (0, 0, ci)
        wmap_out = lambda t, ci: (0, ci, 0)
        sem = ("arbitrary",) * 2
    row_spec = pl.BlockSpec((tm, D_MODEL), lambda t, *_: (t, 0))
    ins = [x, mod, mod, mod, norm_g]
    specs = [row_spec, _mod_spec(3), _mod_spec(4), _mod_spec(5), _full((1, D_MODEL))]
    if n_e:
        ins.append(router)
        specs.append(_full(router.shape))
    ins += [w_gate, w_up, w_down]
    specs += [pl.BlockSpec((1, D_MODEL, FFN_CHUNK), wmap_in), pl.BlockSpec((1, D_MODEL, FFN_CHUNK), wmap_in),
              pl.BlockSpec((1, FFN_CHUNK, D_MODEL), wmap_out)]
    if final_g is not None:
        ins.append(final_g)
        specs.append(_full((1, D_MODEL)))
    scratch = [pltpu.VMEM((tm, D_MODEL), BF16), pltpu.VMEM((tm, D_MODEL), F32)]
    if n_e:
        scratch.append(pltpu.VMEM((tm, LANES), F32))
    return pl.pallas_call(
        functools.partial(_ffn_body, tiles_per_group * ROW_TILE // tm if tiles_per_group else None, fixed_group, n_e,
                          final_g is not None),
        grid=grid, in_specs=specs, out_specs=row_spec,
        out_shape=jax.ShapeDtypeStruct((rows, D_MODEL), F32),
        scratch_shapes=scratch,
        compiler_params=_cparams(*sem), name="ffn_moe" if n_e else "ffn",
    )(*ins)


def _pad_heads(w, width):
    k = w.shape[0]
    w = w.reshape(k, N_HEADS, width)
    return jnp.pad(w, ((0, 0), (0, 0), (0, HEAD_PAD - width))).reshape(k, N_HEADS * HEAD_PAD)


def _rot_cols(w):
    half = QK_ROPE // 2
    return jnp.concatenate([-w[..., half:], w[..., :half]], axis=-1)


def _layer_weights(p, layer):
    w_in = p["w_in"][layer]
    w_uq = p["w_uq"][layer].reshape(Q_LORA, N_HEADS, QK_NOPE + QK_ROPE)
    zeros_nope = jnp.zeros((Q_LORA, N_HEADS, QK_NOPE), F32)
    w_uq_b = jnp.concatenate([zeros_nope, _rot_cols(w_uq[..., QK_NOPE:])], axis=-1)
    w_kr = w_in[:, KV_START + KV_LORA:HY_START]
    zk = jnp.zeros((D_MODEL, QK_NOPE), F32)
    zp = jnp.zeros((D_MODEL, HEAD_PAD - QK_NOPE - QK_ROPE), F32)
    w_kr2 = jnp.concatenate([zk, w_kr, zp, zk, _rot_cols(w_kr), zp], axis=-1)
    row = lambda v: v.reshape(1, -1)
    bf = lambda v: v.astype(BF16)
    return {
        "norm1_g": row(p["norm1_g"][layer]), "norm2_g": row(p["norm2_g"][layer]),
        "w_q": bf(w_in[:, :Q_LORA]), "q_norm_g": row(p["q_norm_g"][layer]),
        "w_uq_a": bf(_pad_heads(w_uq.reshape(Q_LORA, -1), QK_NOPE + QK_ROPE)),
        "w_uq_b": bf(_pad_heads(w_uq_b.reshape(Q_LORA, -1), QK_NOPE + QK_ROPE)),
        "w_kv": bf(w_in[:, KV_START:KV_START + KV_LORA]), "kv_norm_g": row(p["kv_norm_g"][layer]),
        "w_uk": bf(_pad_heads(p["w_uk"][layer], QK_NOPE)), "w_uv": bf(p["w_uv"][layer]),
        "w_kr": bf(w_kr2),
        "w_hy": bf(w_in[:, HY_START:GATE_START]), "w_gate": bf(w_in[:, GATE_START:]),
        "hy_short_w": p["hy_short_w"][layer], "hy_short_b": p["hy_short_b"][layer],
        "hy_w1": p["hy_w1"][layer], "hy_b1": p["hy_b1"][layer], "hy_w2": p["hy_w2"][layer],
        "hy_b2": p["hy_b2"][layer], "hy_w3": p["hy_w3"][layer], "hy_freq": p["hy_freq"][layer],
        "hy_decay": p["hy_decay"][layer], "hy_bias": row(p["hy_bias"][layer]),
        "w_br_attn": bf(p["w_br_attn"][layer]), "w_br_hy": bf(p["w_br_hy"][layer]), "w_out": bf(p["w_out"][layer]),
    }


def _rope_tables(n):
    rows = n // GRID_W
    n_freq = QK_ROPE // 4
    inv_freq = ROPE_BASE ** (-jnp.arange(n_freq, dtype=F32) / n_freq)
    r = jnp.repeat(jnp.arange(rows, dtype=F32), GRID_W)
    col = jnp.tile(jnp.arange(GRID_W, dtype=F32), rows)
    ang = jnp.concatenate([r[:, None] * inv_freq, col[:, None] * inv_freq], axis=-1)
    cos, sin = jnp.cos(ang), jnp.sin(ang)
    ones = jnp.ones((n, QK_NOPE), F32)
    zeros = jnp.zeros((n, QK_NOPE), F32)
    pad = jnp.zeros((n, HEAD_PAD - QK_NOPE - QK_ROPE), F32)
    return (jnp.concatenate([ones, cos, cos, pad], axis=-1), jnp.concatenate([zeros, sin, sin, pad], axis=-1))


def _token_mixer(xs, mod, lw, khat, conv_short, *, batch, n, tiles_per_group, fixed_group, rope_tabs,
                 k_ctx=None, v_ctx=None):
    pr = _inproj(xs, mod, lw, tiles_per_group=tiles_per_group, fixed_group=fixed_group, rope_tabs=rope_tabs,
                 want_q=True, want_hg=True)
    k3 = pr["k"].reshape(batch, n, -1)
    v3 = pr["v"].reshape(batch, n, -1)
    if k_ctx is None:
        o_attn = _attention(pr["q"], k3, v3, None, None, batch)
    else:
        o_attn = _attention(pr["q"], k_ctx, v_ctx, k3, v3, batch)
    if conv_short:
        z, x0 = _hyena_pre(pr["p_hy"], lw["hy_short_w"], lw["hy_short_b"], batch, None)
        conv = _direct_conv(z, *khat, n)
    else:
        z, x0, zt = _hyena_pre(pr["p_hy"], lw["hy_short_w"], lw["hy_short_b"], batch, DFT_Q)
        conv = _long_conv(zt, khat)
    x_new = _merge(xs, mod, o_attn, conv, z, x0, pr["gates"], lw, tiles_per_group=tiles_per_group,
                   fixed_group=fixed_group)
    return x_new, k3, v3


def kernel(x, c, ctx, c_ctx, w_mod, b_mod, norm1_g, norm2_g, w_in, q_norm_g, kv_norm_g, w_uq, w_uk, w_uv, hy_short_w, hy_short_b, hy_w1, hy_b1, hy_w2, hy_b2, hy_w3, hy_freq, hy_decay, hy_bias, w_br_attn, w_br_hy, w_out, ffn_w_gate, ffn_w_up, ffn_w_down, moe_router, moe_w_gate, moe_w_up, moe_w_down, final_g):
    p = dict(w_in=w_in, norm1_g=norm1_g, norm2_g=norm2_g, q_norm_g=q_norm_g, kv_norm_g=kv_norm_g, w_uq=w_uq,
             w_uk=w_uk, w_uv=w_uv, hy_short_w=hy_short_w, hy_short_b=hy_short_b, hy_w1=hy_w1, hy_b1=hy_b1,
             hy_w2=hy_w2, hy_b2=hy_b2, hy_w3=hy_w3, hy_freq=hy_freq, hy_decay=hy_decay, hy_bias=hy_bias,
             w_br_attn=w_br_attn, w_br_hy=w_br_hy, w_out=w_out)
    batch, seq, d = x.shape
    ctx_len = ctx.shape[1]
    depth = w_mod.shape[0]
    ctx_group = batch
    cond8 = jnp.zeros((SUBLANES, d), F32).at[:batch].set(c).at[ctx_group].set(c_ctx)
    rope_tabs = _rope_tables(seq)
    lat_tiles = seq // ROW_TILE
    xs = x.reshape(batch * seq, d)
    cs = ctx.reshape(batch * ctx_len, d)
    bf = lambda v: v.astype(BF16)
    for layer in range(depth):
        last = layer == depth - 1
        lw = _layer_weights(p, layer)
        mod = _adaln(cond8, w_mod[layer], b_mod[layer])
        khat_lat = _long_conv_spectrum(seq, lw)
        if last:
            pr = _inproj(cs, mod, lw, tiles_per_group=None, fixed_group=ctx_group, rope_tabs=None, want_q=False,
                         want_hg=False)
            k_ctx, v_ctx = pr["k"].reshape(batch, ctx_len, -1), pr["v"].reshape(batch, ctx_len, -1)
        else:
            khat_ctx = _hyena_filter(ctx_len, ctx_len, lw)
            cs_mid, k_ctx, v_ctx = _token_mixer(cs, mod, lw, khat_ctx, True, batch=batch, n=ctx_len,
                                                tiles_per_group=None, fixed_group=ctx_group, rope_tabs=None)
        xs, _, _ = _token_mixer(xs, mod, lw, khat_lat, False, batch=batch, n=seq, tiles_per_group=lat_tiles,
                                fixed_group=None, rope_tabs=rope_tabs, k_ctx=k_ctx, v_ctx=v_ctx)
        i = layer // 2
        n2 = lw["norm2_g"]
        if layer % 2 == 0:
            wg, wu, wd = bf(ffn_w_gate[i])[None], bf(ffn_w_up[i])[None], bf(ffn_w_down[i])[None]
            router = None
        else:
            wg, wu, wd = bf(moe_w_gate[i]), bf(moe_w_up[i]), bf(moe_w_down[i])
            router = jnp.pad(moe_router[i], ((0, 0), (0, LANES - N_EXPERTS)))
        xs = _ffn(xs, mod, n2, wg, wu, wd, tiles_per_group=lat_tiles, fixed_group=None, router=router,
                  final_g=final_g.reshape(1, d) if last else None)
        if not last:
            cs = _ffn(cs_mid, mod, n2, wg, wu, wd, tiles_per_group=None, fixed_group=ctx_group, router=router)
    return xs.reshape(batch, seq, d)
```

```python
import functools
import math

import numpy as np
import jax
import jax.numpy as jnp
from jax import lax
from jax.experimental import pallas as pl
from jax.experimental.pallas import tpu as pltpu

F32 = jnp.float32
BF16 = jnp.bfloat16
HIGHEST = lax.Precision.HIGHEST

D_MODEL = 1024
GRID_W = 64
EPS = 1e-6
N_HEADS = 8
Q_LORA = 384
KV_LORA = 256
QK_NOPE = 64
QK_ROPE = 32
V_HEAD = 64
ROPE_BASE = 10000.0
ATTN_SCALE = (QK_NOPE + QK_ROPE) ** -0.5
HY_WIDTH = 512
HY_EMB = 17
HY_HIDDEN = 64
D_FF = 2816
N_EXPERTS = 8
KV_START = Q_LORA
HY_START = KV_START + KV_LORA + QK_ROPE
GATE_START = HY_START + 3 * HY_WIDTH

LANES = 128
SUBLANES = 8
HEAD_PAD = LANES
VMEM_LIMIT = 56 * 2**20

ROW_TILE = 256
FFN_ROW_TILE = 512
FFN_CHUNK = 1408
VT_BLOCK = ROW_TILE
ATTN_KV = 512
ATTN_Q_TILE = 512
Q_SCALE = ATTN_SCALE * math.log2(math.e)
DFT_P = 64
DFT_Q = 128
COLSLOT_BQ = 8


def _cparams(*sem):
    return pltpu.CompilerParams(dimension_semantics=sem, vmem_limit_bytes=VMEM_LIMIT)


def _dot(a, b):
    return jnp.dot(a, b, preferred_element_type=F32)


def _dot_hi(a, b):
    return jnp.dot(a, b, precision=HIGHEST, preferred_element_type=F32)


def _rms(xf, g):
    return xf * lax.rsqrt(jnp.mean(xf * xf, axis=-1, keepdims=True) + EPS) * g


def _full(shape):
    nd = len(shape)
    return pl.BlockSpec(shape, lambda *_: (0,) * nd)


def _adaln_body(c_ref, w_ref, b_ref, o_ref):
    c = c_ref[...]
    o_ref[...] = _dot_hi(c * jax.nn.sigmoid(c), w_ref[...]) + b_ref[...]


def _adaln(cond8, w, b):
    d, n = w.shape
    return pl.pallas_call(
        _adaln_body,
        grid=(n // d,),
        in_specs=[_full((SUBLANES, d)), pl.BlockSpec((d, d), lambda j: (0, j)),
                  pl.BlockSpec((1, d), lambda j: (0, j))],
        out_specs=pl.BlockSpec((SUBLANES, d), lambda j: (0, j)),
        out_shape=jax.ShapeDtypeStruct((SUBLANES, n), F32),
        compiler_params=_cparams("arbitrary"),
        name="adaln",
    )(cond8, w, b.reshape(1, n))


def _mod_spec(chunk):
    return pl.BlockSpec((SUBLANES, D_MODEL), lambda t, *_: (0, chunk))


def _mod_row(ref, group):
    return ref[pl.ds(group, 1), :]


def _inproj_body(tiles_per_group, fixed_group, use_rope, want_q, want_hg, *refs):
    it = iter(refs)
    x_ref, sh_ref, sc_ref, g1_ref = next(it), next(it), next(it), next(it)
    wkv_ref, kvg_ref, wuk_ref, wuv_ref, wkr_ref = next(it), next(it), next(it), next(it), next(it)
    if want_q:
        wq_ref, qg_ref, wuqa_ref, wuqb_ref = next(it), next(it), next(it), next(it)
    if want_hg:
        why_ref, wgate_ref = next(it), next(it)
    if use_rope:
        cos_ref, sin_ref = next(it), next(it)
    k_out, v_out = next(it), next(it)
    if want_q:
        q_out = next(it)
    if want_hg:
        phy_out, gate_out = next(it), next(it)

    group = fixed_group if fixed_group is not None else pl.program_id(0) // tiles_per_group
    xf = x_ref[...]
    h = _rms(xf, g1_ref[...]) * (1.0 + _mod_row(sc_ref, group)) + _mod_row(sh_ref, group)
    h = h.astype(BF16)
    if use_rope:
        cos, sin = cos_ref[...], sin_ref[...]

    def rope(a, b):
        return a * cos + b * sin if use_rope else a

    ckv = _rms(_dot(h, wkv_ref[...]), kvg_ref[...]).astype(BF16)
    vt = lax.dot_general(wuv_ref[...], ckv, (((1,), (1,)), ((), ())), preferred_element_type=F32)
    vrow = lax.broadcasted_iota(jnp.int32, vt.shape, 0) & (HEAD_PAD - 1)
    v_out[0] = jnp.where(vrow == V_HEAD, 1.0, vt).astype(BF16)
    k_nope = _dot(ckv, wuk_ref[...])
    kr = _dot(h, wkr_ref[...])
    k_rope = rope(kr[:, :HEAD_PAD], kr[:, HEAD_PAD:])
    for hd in range(N_HEADS):
        sl = slice(hd * HEAD_PAD, (hd + 1) * HEAD_PAD)
        k_out[:, sl] = (k_nope[:, sl] + k_rope).astype(BF16)
    if want_q:
        qn = _rms(_dot(h, wq_ref[...]), qg_ref[...]).astype(BF16)
        qa = _dot(qn, wuqa_ref[...])
        qb = _dot(qn, wuqb_ref[...]) if use_rope else None
        for hd in range(N_HEADS):
            sl = slice(hd * HEAD_PAD, (hd + 1) * HEAD_PAD)
            q_out[:, sl] = (rope(qa[:, sl], None if qb is None else qb[:, sl]) * Q_SCALE).astype(BF16)
    if want_hg:
        n_hy = why_ref.shape[1]
        for c0 in range(0, n_hy, 512):
            phy_out[:, c0:c0 + 512] = _dot(h, why_ref[:, c0:c0 + 512])
        n_g = wgate_ref.shape[1]
        for c0 in range(0, n_g, 512):
            gate_out[:, c0:c0 + 512] = jax.nn.sigmoid(_dot(h, wgate_ref[:, c0:c0 + 512]))


def _inproj(x, mod, lw, *, tiles_per_group, fixed_group, rope_tabs, want_q, want_hg):
    rows = x.shape[0]
    nt = rows // ROW_TILE
    use_rope = rope_tabs is not None
    row_spec = lambda n: pl.BlockSpec((ROW_TILE, n), lambda t: (t, 0))
    ins = [x, mod, mod, lw["norm1_g"], lw["w_kv"], lw["kv_norm_g"], lw["w_uk"], lw["w_uv"], lw["w_kr"]]
    specs = [row_spec(D_MODEL), _mod_spec(0), _mod_spec(1), _full((1, D_MODEL)),
             _full(lw["w_kv"].shape), _full((1, KV_LORA)), _full(lw["w_uk"].shape), _full(lw["w_uv"].shape),
             _full(lw["w_kr"].shape)]
    if want_q:
        ins += [lw["w_q"], lw["q_norm_g"], lw["w_uq_a"], lw["w_uq_b"]]
        specs += [_full(lw["w_q"].shape), _full((1, Q_LORA)), _full(lw["w_uq_a"].shape), _full(lw["w_uq_b"].shape)]
    if want_hg:
        ins += [lw["w_hy"], lw["w_gate"]]
        specs += [_full(lw["w_hy"].shape), _full(lw["w_gate"].shape)]
    if use_rope:
        seq_tiles = rope_tabs[0].shape[0] // ROW_TILE
        ins += list(rope_tabs)
        specs += [pl.BlockSpec((ROW_TILE, HEAD_PAD), lambda t: (t % seq_tiles, 0))] * 2
    hp = N_HEADS * HEAD_PAD
    out_shape = [jax.ShapeDtypeStruct((rows, hp), BF16), jax.ShapeDtypeStruct((nt, hp, ROW_TILE), BF16)]
    out_specs = [row_spec(hp), pl.BlockSpec((1, hp, ROW_TILE), lambda t: (t, 0, 0))]
    if want_q:
        out_shape.append(jax.ShapeDtypeStruct((rows, hp), BF16))
        out_specs.append(row_spec(hp))
    if want_hg:
        out_shape += [jax.ShapeDtypeStruct((rows, 3 * HY_WIDTH), F32), jax.ShapeDtypeStruct((rows, 2 * D_MODEL), F32)]
        out_specs += [row_spec(3 * HY_WIDTH), row_spec(2 * D_MODEL)]
    outs = pl.pallas_call(
        functools.partial(_inproj_body, tiles_per_group, fixed_group, use_rope, want_q, want_hg),
        grid=(nt,), in_specs=specs, out_specs=out_specs, out_shape=out_shape,
        compiler_params=_cparams("arbitrary"), name="inproj",
    )(*ins)
    res = {"k": outs[0], "v": outs[1]}
    i = 2
    if want_q:
        res["q"] = outs[i]
        i += 1
    if want_hg:
        res["p_hy"], res["gates"] = outs[i], outs[i + 1]
    return res


def _attn_body(n_lat_blocks, *refs):
    if n_lat_blocks:
        q_ref, kc_ref, vc_ref, kl_ref, vl_ref, o_ref = refs
    else:
        q_ref, kc_ref, vc_ref, o_ref = refs
    tq = q_ref.shape[0]
    q = q_ref[...]
    heads = tuple(slice(hd * HEAD_PAD, (hd + 1) * HEAD_PAD) for hd in range(2))
    sub = ATTN_KV // VT_BLOCK

    def scores(kblk):
        return tuple(lax.dot_general(kblk[:, sl], q[:, sl], (((1,), (1,)), ((), ())), preferred_element_type=F32)
                     for sl in heads)

    def update(st, vt_blocks, carry):
        m_new = [jnp.maximum(carry[hd][0], jnp.max(st[hd], axis=0, keepdims=True)) for hd in range(2)]
        p = [jnp.exp2(st[hd] - m_new[hd]).astype(BF16) for hd in range(2)]
        out = []
        for hd in range(2):
            m, acc = carry[hd]
            pv = None
            for j, vt in enumerate(vt_blocks):
                r = _dot(vt[heads[hd], :], p[hd][j * VT_BLOCK:(j + 1) * VT_BLOCK])
                pv = r if pv is None else pv + r
            out.append((m_new[hd], jnp.exp2(m - m_new[hd]) * acc + pv))
        return tuple(out)

    init = tuple((jnp.full((1, tq), -jnp.inf, F32), jnp.zeros((HEAD_PAD, tq), F32)) for _ in range(2))
    carry = update(scores(kc_ref[0]), [vc_ref[0, 0]], init)
    if n_lat_blocks:
        def k_block(i):
            return kl_ref[0, pl.ds(pl.multiple_of(i * ATTN_KV, ATTN_KV), ATTN_KV), :]

        def v_blocks(i):
            return [vl_ref[0, i * sub + j] for j in range(sub)]

        def body(i, c):
            st, carry = c
            st_next = scores(k_block(i + 1))
            return st_next, update(st, v_blocks(i), carry)
        st, carry = lax.fori_loop(0, n_lat_blocks - 1, body, (scores(k_block(0)), carry))
        carry = update(st, v_blocks(n_lat_blocks - 1), carry)
    o_t = jnp.concatenate([acc[:V_HEAD] / acc[V_HEAD:V_HEAD + 1] for _, acc in carry], axis=0)
    o_ref[...] = o_t.T.astype(BF16)


def _attention(q, k_ctx, vt_ctx, k_lat, vt_lat, batch):
    rows = q.shape[0]
    tq = min(ATTN_Q_TILE, rows // batch)
    qt_per_b = rows // batch // tq
    ctx_len = k_ctx.shape[1]
    assert ctx_len == VT_BLOCK
    pair_w = 2 * HEAD_PAD
    ins = [q, k_ctx, vt_ctx]
    specs = [pl.BlockSpec((tq, pair_w), lambda b, hp, t: (b * qt_per_b + t, hp)),
             pl.BlockSpec((1, ctx_len, pair_w), lambda b, hp, t: (b, 0, hp)),
             pl.BlockSpec((1, 1, pair_w, VT_BLOCK), lambda b, hp, t: (b, 0, hp, 0))]
    n_lat_blocks = 0
    if k_lat is not None:
        lat_len = k_lat.shape[1]
        n_lat_blocks = lat_len // ATTN_KV
        ins += [k_lat, vt_lat]
        specs += [pl.BlockSpec((1, lat_len, pair_w), lambda b, hp, t: (b, 0, hp)),
                  pl.BlockSpec((1, lat_len // VT_BLOCK, pair_w, VT_BLOCK), lambda b, hp, t: (b, 0, hp, 0))]
    return pl.pallas_call(
        functools.partial(_attn_body, n_lat_blocks),
        grid=(batch, N_HEADS // 2, qt_per_b), in_specs=specs,
        out_specs=pl.BlockSpec((tq, 2 * V_HEAD), lambda b, hp, t: (b * qt_per_b + t, hp)),
        out_shape=jax.ShapeDtypeStruct((rows, N_HEADS * V_HEAD), BF16),
        compiler_params=_cparams("arbitrary", "arbitrary", "arbitrary"), name="attention",
    )(*ins)


def _hypre_body(seq_tiles, n_chunks, p_ref, prev_ref, next_ref, w_ref, b_ref, z_ref, x0_ref, *zt_ref):
    j = pl.program_id(0) % seq_tiles
    p = p_ref[...]
    tm = p.shape[0]
    row = lax.broadcasted_iota(jnp.int32, (tm, 1), 0)
    prev_row = jnp.where(j != 0, prev_ref[SUBLANES - 1:SUBLANES, :], 0.0)
    next_row = jnp.where(j != seq_tiles - 1, next_ref[0:1, :], 0.0)
    up = jnp.where(row == 0, prev_row, pltpu.roll(p, 1, 0))
    dn = jnp.where(row == tm - 1, next_row, pltpu.roll(p, tm - 1, 0))
    u = up * w_ref[0:1, :] + p * w_ref[1:2, :] + dn * w_ref[2:3, :] + b_ref[...]
    c = HY_WIDTH
    z = u[:, :c] * u[:, c:2 * c]
    z_ref[...] = z
    x0_ref[...] = u[:, 2 * c:]
    if zt_ref:
        q = tm // n_chunks
        for a in range(n_chunks):
            zt_ref[0][0, :, a * c:(a + 1) * c] = z[a * q:(a + 1) * q, :]


def _hyena_pre(p_hy, short_w, short_b, batch, dft_q):
    rows = p_hy.shape[0]
    n = rows // batch
    seq_tiles = n // ROW_TILE
    nt = rows // ROW_TILE
    c3 = 3 * HY_WIDTH
    halo = ROW_TILE // SUBLANES
    last8 = rows // SUBLANES - 1
    specs = [pl.BlockSpec((ROW_TILE, c3), lambda t: (t, 0)),
             pl.BlockSpec((SUBLANES, c3), lambda t: (jnp.maximum(t * halo - 1, 0), 0)),
             pl.BlockSpec((SUBLANES, c3), lambda t: (jnp.minimum((t + 1) * halo, last8), 0)),
             _full((3, c3)), _full((1, c3))]
    out_shape = [jax.ShapeDtypeStruct((rows, HY_WIDTH), F32)] * 2
    out_specs = [pl.BlockSpec((ROW_TILE, HY_WIDTH), lambda t: (t, 0))] * 2
    n_chunks = 1
    if dft_q is not None:
        n_chunks = ROW_TILE // dft_q
        half_p = n // dft_q
        out_shape.append(jax.ShapeDtypeStruct((batch // 2, dft_q, 2 * half_p * HY_WIDTH), F32))

        def zt_map(t):
            b, jt = t // seq_tiles, t % seq_tiles
            return (b // 2, 0, (b % 2) * seq_tiles + jt)
        out_specs.append(pl.BlockSpec((1, dft_q, n_chunks * HY_WIDTH), zt_map))
    return pl.pallas_call(
        functools.partial(_hypre_body, seq_tiles, n_chunks),
        grid=(nt,), in_specs=specs, out_specs=out_specs, out_shape=out_shape,
        compiler_params=_cparams("arbitrary"), name="hyena_pre",
    )(p_hy, p_hy, p_hy, short_w, short_b.reshape(1, c3))


def _filter_body(emb_ref, w1_ref, b1_ref, w2_ref, b2_ref, w3_ref, freq_ref, decay_ref, h0_ref, h1_ref, s_ref):
    a = pl.program_id(0)
    emb = emb_ref[...]
    freq = freq_ref[...]
    h = jnp.sin(freq * (_dot_hi(emb, w1_ref[...]) + b1_ref[...]))
    h = jnp.sin(freq * (_dot_hi(h, w2_ref[...]) + b2_ref[...]))
    h = _dot_hi(h, w3_ref[...]) * jnp.exp(-emb[:, 0:1] * jnp.abs(decay_ref[...]))
    c = HY_WIDTH
    row = lax.broadcasted_iota(jnp.int32, (emb.shape[0], 1), 0)
    h0 = h[:, :c]
    h1 = jnp.where(jnp.logical_and(a == 0, row == 0), 0.0, h[:, c:])
    h0_ref[0] = h0
    h1_ref[0] = h1
    part = jnp.sum(jnp.abs(h0) + jnp.abs(h1), axis=0, keepdims=True)

    @pl.when(a == 0)
    def _():
        s_ref[...] = jnp.zeros_like(s_ref)
    s_ref[...] += part


def _hyena_filter(n, q, lw):
    f32 = F32
    bands = (HY_EMB - 1) // 2
    t = jnp.linspace(0.0, 1.0, n, dtype=f32)[:, None]
    phase = (2.0 * math.pi / n) * jnp.arange(n, dtype=f32)[:, None] * jnp.linspace(1e-4, bands - 1, bands, dtype=f32)
    emb = jnp.concatenate([t, jnp.cos(phase), -jnp.sin(phase), jnp.zeros((n, 32 - HY_EMB), f32)], axis=-1)
    w1 = jnp.concatenate([lw["hy_w1"], jnp.zeros((32 - HY_EMB, HY_HIDDEN), f32)], axis=0)
    c = HY_WIDTH
    slots = n // q
    row = lambda v: v.reshape(1, -1)
    return pl.pallas_call(
        _filter_body,
        grid=(slots,),
        in_specs=[pl.BlockSpec((q, 32), lambda a: (a, 0)), _full((32, HY_HIDDEN)), _full((1, HY_HIDDEN)),
                  _full((HY_HIDDEN, HY_HIDDEN)), _full((1, HY_HIDDEN)), _full((HY_HIDDEN, 2 * c)),
                  _full((1, HY_HIDDEN)), _full((1, 2 * c))],
        out_specs=[pl.BlockSpec((1, q, c), lambda a: (0, 0, a)), pl.BlockSpec((1, q, c), lambda a: (0, 0, a)),
                   _full((1, c))],
        out_shape=[jax.ShapeDtypeStruct((1, q, slots * c), f32)] * 2 + [jax.ShapeDtypeStruct((1, c), f32)],
        compiler_params=_cparams("arbitrary"), name="hyena_filter",
    )(emb, w1, row(lw["hy_b1"]), lw["hy_w2"], row(lw["hy_b2"]), lw["hy_w3"], row(lw["hy_freq"]),
      row(lw["hy_decay"]))


def _colslot_body(nparts, bq, c, *refs):
    x_refs, t_refs, o_ref = refs[:nparts], refs[nparts:2 * nparts], refs[-1]
    for j in range(bq):
        acc = None
        for x_ref, t_ref in zip(x_refs, t_refs):
            r = _dot_hi(t_ref[j], x_ref[0, j])
            acc = r if acc is None else acc + r
        o_ref[0, :, j * c:(j + 1) * c] = acc


def _colslot_matmul(xs, tabs, c=HY_WIDTH):
    g, q = xs[0].shape[0], xs[0].shape[1]
    m = tabs[0].shape[1]
    bq = COLSLOT_BQ
    xs4 = [x.reshape(g, q, -1, c) for x in xs]
    specs = [pl.BlockSpec((1, bq, x.shape[2], c), lambda gi, qi: (gi, qi, 0, 0)) for x in xs4]
    specs += [pl.BlockSpec((bq, m, t.shape[2]), lambda gi, qi: (qi, 0, 0)) for t in tabs]
    return pl.pallas_call(
        functools.partial(_colslot_body, len(xs), bq, c),
        grid=(g, q // bq), in_specs=specs,
        out_specs=pl.BlockSpec((1, m, bq * c), lambda gi, qi: (gi, 0, qi)),
        out_shape=jax.ShapeDtypeStruct((g, m, q * c), F32),
        compiler_params=_cparams("arbitrary", "arbitrary"), name="dft_stride_stage",
    )(*xs4, *tabs)


def _spec_mid_body(g_ref, m3_ref, m3i_ref, k_ref, hr_ref, hi_ref):
    q = hr_ref.shape[1]
    x = _dot_hi(m3_ref[...], g_ref[0, :, 0].reshape(2 * q, -1))
    kh = k_ref[0]
    xr, xi, kr, ki = x[:q], x[q:], kh[:q], kh[q:]
    y = jnp.concatenate([xr * kr - xi * ki, xr * ki + xi * kr], axis=0)
    h = _dot_hi(m3i_ref[...], y)
    hr_ref[0] = h[:q]
    hi_ref[0] = h[q:]


def _spec_mid(gf, m3, m3i, khat, p, q, c=HY_WIDTH):
    bp = gf.shape[0]
    g5 = gf.reshape(bp, 2, p, q, c)
    return pl.pallas_call(
        _spec_mid_body,
        grid=(bp, p),
        in_specs=[pl.BlockSpec((1, 2, 1, q, c), lambda b, ci: (b, 0, ci, 0, 0)), _full((2 * q, 2 * q)),
                  _full((2 * q, 2 * q)), pl.BlockSpec((1, 2 * q, c), lambda b, ci: (ci, 0, 0))],
        out_specs=[pl.BlockSpec((1, q, c), lambda b, ci: (b, 0, ci))] * 2,
        out_shape=[jax.ShapeDtypeStruct((bp, q, p * c), F32)] * 2,
        compiler_params=_cparams("arbitrary", "arbitrary"), name="dft_mid",
    )(g5, m3, m3i, khat)


def _filter_spec_body(g_ref, m3k_ref, s_ref, k_ref):
    q4 = m3k_ref.shape[1]
    k_ref[0] = _dot_hi(m3k_ref[...], g_ref[0, :, 0].reshape(q4, -1)) / s_ref[...]


def _filter_spec(gk, m3k, s, p, q, c=HY_WIDTH):
    g5 = gk.reshape(1, 4, p, q, c)
    return pl.pallas_call(
        _filter_spec_body,
        grid=(p,),
        in_specs=[pl.BlockSpec((1, 4, 1, q, c), lambda ci: (0, 0, ci, 0, 0)), _full((2 * q, 4 * q)), _full((1, c))],
        out_specs=pl.BlockSpec((1, 2 * q, c), lambda ci: (ci, 0, 0)),
        out_shape=jax.ShapeDtypeStruct((p, 2 * q, c), F32),
        compiler_params=_cparams("arbitrary"), name="filter_spectrum",
    )(g5, m3k, s)


def _direct_conv_body(z_ref, h0_ref, h1_ref, s_ref, tf_ref, tk_ref, ti_ref, o_ref):
    n2 = tf_ref.shape[0] // 2
    kh = _dot_hi(tk_ref[...], jnp.concatenate([h0_ref[0], h1_ref[0]], axis=0)) / s_ref[...]
    x = _dot_hi(tf_ref[...], z_ref[0])
    xr, xi, kr, ki = x[:n2], x[n2:], kh[:n2], kh[n2:]
    y = jnp.concatenate([xr * kr - xi * ki, xr * ki + xi * kr], axis=0)
    o_ref[0] = _dot_hi(ti_ref[...], y)


def _direct_conv(z, h0, h1, s, n, c=HY_WIDTH):
    bp = z.shape[0] // (2 * n)
    tf, tk, ti = _direct_tables(n)
    out = pl.pallas_call(
        _direct_conv_body,
        grid=(bp,),
        in_specs=[pl.BlockSpec((1, 2 * n, c), lambda b: (b, 0, 0)), _full((1, n, c)), _full((1, n, c)), _full((1, c)),
                  _full(tf.shape), _full(tk.shape), _full(ti.shape)],
        out_specs=pl.BlockSpec((1, 2 * n, c), lambda b: (b, 0, 0)),
        out_shape=jax.ShapeDtypeStruct((bp, 2 * n, c), F32),
        compiler_params=_cparams("arbitrary"), name="direct_conv",
    )(z.reshape(bp, 2 * n, c), h0, h1, s, tf, tk, ti)
    return out.reshape(bp * 2 * n, c)


def _cis(m, n_total):
    ang = (2.0 * np.pi / n_total) * (m % n_total).astype(np.float64)
    return np.cos(ang), -np.sin(ang)


@functools.lru_cache(maxsize=None)
def _direct_tables(n):
    nn = 2 * n
    f = np.arange(nn)[:, None]
    t = np.arange(n)[None, :]
    cr, ci = _cis(f * t, nn)
    tf = np.block([[cr, -ci], [ci, cr]])
    tk = np.block([[cr, cr], [ci, -ci]])
    ti = np.block([[cr.T, ci.T], [-ci.T, cr.T]]) / nn
    return tuple(a.astype(np.float32) for a in (tf, tk, ti))


@functools.lru_cache(maxsize=None)
def _twostage_tables(p, q):
    nn = p * q
    hp = p // 2
    s = np.arange(q)[:, None, None]
    c = np.arange(p)[None, :, None]
    a = np.arange(hp)[None, None, :]
    tr, ti = _cis(c * (q * a + s), nn)
    z = np.zeros_like(tr)
    t_data = np.concatenate([np.concatenate([tr, -ti], 2), np.concatenate([ti, tr], 2)], 1)
    t_k0 = np.concatenate([tr, ti, z, z], 1)
    t_k1 = np.concatenate([z, z, tr, -ti], 1)
    trt, tit = np.swapaxes(tr, 1, 2) / nn, np.swapaxes(ti, 1, 2) / nn
    t_inv_r = np.concatenate([trt, -tit], 1)
    t_inv_i = np.concatenate([tit, trt], 1)
    d = np.arange(q)[:, None]
    b = np.arange(q)[None, :]
    fr, fi = _cis(d * b, q)
    m3 = np.block([[fr, -fi], [fi, fr]])
    m3i = np.block([[fr, fi], [-fi, fr]])
    m3k = np.block([[fr, -fi, fr, fi], [fi, fr, -fi, fr]])
    return tuple(x.astype(np.float32) for x in (t_data, t_k0, t_k1, t_inv_r, t_inv_i, m3, m3i, m3k))


def _long_conv_spectrum(n, lw):
    p, q = DFT_P, DFT_Q
    h0, h1, s = _hyena_filter(n, q, lw)
    _, t_k0, t_k1, _, _, _, _, m3k = _twostage_tables(p, q)
    gk = _colslot_matmul([h0, h1], [t_k0, t_k1])
    return _filter_spec(gk, m3k, s, p, q)


def _long_conv(zt, khat):
    p, q = DFT_P, DFT_Q
    t_data, _, _, t_inv_r, t_inv_i, m3, m3i, _ = _twostage_tables(p, q)
    gf = _colslot_matmul([zt], [t_data])
    hr, hi = _spec_mid(gf, m3, m3i, khat, p, q)
    y = _colslot_matmul([hr, hi], [t_inv_r, t_inv_i])
    return y.reshape(-1, HY_WIDTH)


def _merge_body(tiles_per_group, fixed_group, x_ref, g1_ref, oa_ref, conv_ref, z_ref, x0_ref, gate_ref, bias_ref,
                wba_ref, wbh_ref, wout_ref, o_ref):
    group = fixed_group if fixed_group is not None else pl.program_id(0) // tiles_per_group
    o_hy = ((conv_ref[...] + z_ref[...] * bias_ref[...]) * x0_ref[...]).astype(BF16)
    d = D_MODEL
    merged = gate_ref[:, :d] * _dot(oa_ref[...], wba_ref[...]) + gate_ref[:, d:] * _dot(o_hy, wbh_ref[...])
    mix = _dot(merged.astype(BF16), wout_ref[...])
    o_ref[...] = x_ref[...] + _mod_row(g1_ref, group) * mix


def _merge(x, mod, o_attn, conv, z, x0, gates, lw, *, tiles_per_group, fixed_group):
    rows = x.shape[0]
    row_spec = lambda n: pl.BlockSpec((ROW_TILE, n), lambda t: (t, 0))
    c = HY_WIDTH
    return pl.pallas_call(
        functools.partial(_merge_body, tiles_per_group, fixed_group),
        grid=(rows // ROW_TILE,),
        in_specs=[row_spec(D_MODEL), _mod_spec(2), row_spec(N_HEADS * V_HEAD), row_spec(c), row_spec(c), row_spec(c),
                  row_spec(2 * D_MODEL), _full((1, c)), _full(lw["w_br_attn"].shape), _full(lw["w_br_hy"].shape),
                  _full(lw["w_out"].shape)],
        out_specs=row_spec(D_MODEL),
        out_shape=jax.ShapeDtypeStruct((rows, D_MODEL), F32),
        compiler_params=_cparams("arbitrary"), name="merge_out",
    )(x, mod, o_attn, conv, z, x0, gates, lw["hy_bias"], lw["w_br_attn"], lw["w_br_hy"], lw["w_out"])


def _ffn_body(tiles_per_group, fixed_group, n_experts, final_norm, *refs):
    it = iter(refs)
    x_ref, sh_ref, sc_ref, g2_ref, n2_ref = next(it), next(it), next(it), next(it), next(it)
    if n_experts:
        router_ref = next(it)
    wg_ref, wu_ref, wd_ref = next(it), next(it), next(it)
    if final_norm:
        fg_ref = next(it)
    o_ref, h_scr, acc_scr = next(it), next(it), next(it)
    if n_experts:
        comb_scr = next(it)
    group = fixed_group if fixed_group is not None else pl.program_id(0) // tiles_per_group
    if n_experts:
        e, ci = pl.program_id(1), pl.program_id(2)
        first = jnp.logical_and(e == 0, ci == 0)
        last = jnp.logical_and(e == n_experts - 1, ci == pl.num_programs(2) - 1)
    else:
        ci = pl.program_id(1)
        first, last = ci == 0, ci == pl.num_programs(1) - 1

    @pl.when(first)
    def _():
        h = _rms(x_ref[...], n2_ref[...]) * (1.0 + _mod_row(sc_ref, group)) + _mod_row(sh_ref, group)
        h_scr[...] = h.astype(BF16)
        acc_scr[...] = jnp.zeros_like(acc_scr)
        if n_experts:
            logits = _dot_hi(h, router_ref[...])
            lane = lax.broadcasted_iota(jnp.int32, logits.shape, 1)
            logits = jnp.where(lane < n_experts, logits, -jnp.inf)
            m1 = jnp.max(logits, axis=-1, keepdims=True)
            i1 = jnp.min(jnp.where(logits == m1, lane, LANES), axis=-1, keepdims=True)
            rest = jnp.where(lane == i1, -jnp.inf, logits)
            m2 = jnp.max(rest, axis=-1, keepdims=True)
            i2 = jnp.min(jnp.where(rest == m2, lane, LANES), axis=-1, keepdims=True)
            e2 = jnp.exp(m2 - m1)
            w1 = 1.0 / (1.0 + e2)
            comb_scr[...] = jnp.where(lane == i1, w1, 0.0) + jnp.where(lane == i2, e2 * w1, 0.0)

    h = h_scr[...]
    gate = _dot(h, wg_ref[0])
    up = _dot(h, wu_ref[0])
    y = _dot((gate * jax.nn.sigmoid(gate) * up).astype(BF16), wd_ref[0])
    if n_experts:
        lane = lax.broadcasted_iota(jnp.int32, comb_scr.shape, 1)
        y = y * jnp.sum(jnp.where(lane == e, comb_scr[...], 0.0), axis=-1, keepdims=True)
    acc_scr[...] += y

    @pl.when(last)
    def _():
        out = x_ref[...] + _mod_row(g2_ref, group) * acc_scr[...]
        if final_norm:
            out = _rms(out, fg_ref[...])
        o_ref[...] = out


def _ffn(x, mod, norm_g, w_gate, w_up, w_down, *, tiles_per_group, fixed_group, router=None, final_g=None,
         row_tile=FFN_ROW_TILE):
    rows = x.shape[0]
    tm = min(row_tile, rows)
    n_e = w_gate.shape[0] if router is not None else 0
    n_chunks = D_FF // FFN_CHUNK
    if n_e:
        grid = (rows // tm, n_e, n_chunks)
        wmap_in = lambda t, e, ci: (e, 0, ci)
        wmap_out = lambda t, e, ci: (e, ci, 0)
        sem = ("arbitrary",) * 3
    else:
        grid = (rows // tm, n_chunks)
        wmap_in = lambda t, ci: (0, 0, ci)
        wmap_out = lambda t, ci: (0, ci, 0)
        sem = ("arbitrary",) * 2
    row_spec = pl.BlockSpec((tm, D_MODEL), lambda t, *_: (t, 0))
    ins = [x, mod, mod, mod, norm_g]
    specs = [row_spec, _mod_spec(3), _mod_spec(4), _mod_spec(5), _full((1, D_MODEL))]
    if n_e:
        ins.append(router)
        specs.append(_full(router.shape))
    ins += [w_gate, w_up, w_down]
    specs += [pl.BlockSpec((1, D_MODEL, FFN_CHUNK), wmap_in), pl.BlockSpec((1, D_MODEL, FFN_CHUNK), wmap_in),
              pl.BlockSpec((1, FFN_CHUNK, D_MODEL), wmap_out)]
    if final_g is not None:
        ins.append(final_g)
        specs.append(_full((1, D_MODEL)))
    scratch = [pltpu.VMEM((tm, D_MODEL), BF16), pltpu.VMEM((tm, D_MODEL), F32)]
    if n_e:
        scratch.append(pltpu.VMEM((tm, LANES), F32))
    return pl.pallas_call(
        functools.partial(_ffn_body, tiles_per_group * ROW_TILE // tm if tiles_per_group else None, fixed_group, n_e,
                          final_g is not None),
        grid=grid, in_specs=specs, out_specs=row_spec,
        out_shape=jax.ShapeDtypeStruct((rows, D_MODEL), F32),
        scratch_shapes=scratch,
        compiler_params=_cparams(*sem), name="ffn_moe" if n_e else "ffn",
    )(*ins)


def _pad_heads(w, width):
    k = w.shape[0]
    w = w.reshape(k, N_HEADS, width)
    return jnp.pad(w, ((0, 0), (0, 0), (0, HEAD_PAD - width))).reshape(k, N_HEADS * HEAD_PAD)


def _rot_cols(w):
    half = QK_ROPE // 2
    return jnp.concatenate([-w[..., half:], w[..., :half]], axis=-1)


def _layer_weights(p, layer):
    w_in = p["w_in"][layer]
    w_uq = p["w_uq"][layer].reshape(Q_LORA, N_HEADS, QK_NOPE + QK_ROPE)
    zeros_nope = jnp.zeros((Q_LORA, N_HEADS, QK_NOPE), F32)
    w_uq_b = jnp.concatenate([zeros_nope, _rot_cols(w_uq[..., QK_NOPE:])], axis=-1)
    w_kr = w_in[:, KV_START + KV_LORA:HY_START]
    zk = jnp.zeros((D_MODEL, QK_NOPE), F32)
    zp = jnp.zeros((D_MODEL, HEAD_PAD - QK_NOPE - QK_ROPE), F32)
    w_kr2 = jnp.concatenate([zk, w_kr, zp, zk, _rot_cols(w_kr), zp], axis=-1)
    row = lambda v: v.reshape(1, -1)
    bf = lambda v: v.astype(BF16)
    return {
        "norm1_g": row(p["norm1_g"][layer]), "norm2_g": row(p["norm2_g"][layer]),
        "w_q": bf(w_in[:, :Q_LORA]), "q_norm_g": row(p["q_norm_g"][layer]),
        "w_uq_a": bf(_pad_heads(w_uq.reshape(Q_LORA, -1), QK_NOPE + QK_ROPE)),
        "w_uq_b": bf(_pad_heads(w_uq_b.reshape(Q_LORA, -1), QK_NOPE + QK_ROPE)),
        "w_kv": bf(w_in[:, KV_START:KV_START + KV_LORA]), "kv_norm_g": row(p["kv_norm_g"][layer]),
        "w_uk": bf(_pad_heads(p["w_uk"][layer], QK_NOPE)), "w_uv": bf(_pad_heads(p["w_uv"][layer], V_HEAD).T),
        "w_kr": bf(w_kr2),
        "w_hy": bf(w_in[:, HY_START:GATE_START]), "w_gate": bf(w_in[:, GATE_START:]),
        "hy_short_w": p["hy_short_w"][layer], "hy_short_b": p["hy_short_b"][layer],
        "hy_w1": p["hy_w1"][layer], "hy_b1": p["hy_b1"][layer], "hy_w2": p["hy_w2"][layer],
        "hy_b2": p["hy_b2"][layer], "hy_w3": p["hy_w3"][layer], "hy_freq": p["hy_freq"][layer],
        "hy_decay": p["hy_decay"][layer], "hy_bias": row(p["hy_bias"][layer]),
        "w_br_attn": bf(p["w_br_attn"][layer]), "w_br_hy": bf(p["w_br_hy"][layer]), "w_out": bf(p["w_out"][layer]),
    }


def _rope_tables(n):
    rows = n // GRID_W
    n_freq = QK_ROPE // 4
    inv_freq = ROPE_BASE ** (-jnp.arange(n_freq, dtype=F32) / n_freq)
    r = jnp.repeat(jnp.arange(rows, dtype=F32), GRID_W)
    col = jnp.tile(jnp.arange(GRID_W, dtype=F32), rows)
    ang = jnp.concatenate([r[:, None] * inv_freq, col[:, None] * inv_freq], axis=-1)
    cos, sin = jnp.cos(ang), jnp.sin(ang)
    ones = jnp.ones((n, QK_NOPE), F32)
    zeros = jnp.zeros((n, QK_NOPE), F32)
    pad = jnp.zeros((n, HEAD_PAD - QK_NOPE - QK_ROPE), F32)
    return (jnp.concatenate([ones, cos, cos, pad], axis=-1), jnp.concatenate([zeros, sin, sin, pad], axis=-1))


def _token_mixer(xs, mod, lw, khat, conv_short, *, batch, n, tiles_per_group, fixed_group, rope_tabs,
                 k_ctx=None, v_ctx=None):
    pr = _inproj(xs, mod, lw, tiles_per_group=tiles_per_group, fixed_group=fixed_group, rope_tabs=rope_tabs,
                 want_q=True, want_hg=True)
    k3 = pr["k"].reshape(batch, n, -1)
    v3 = pr["v"].reshape(batch, n // VT_BLOCK, N_HEADS * HEAD_PAD, VT_BLOCK)
    if k_ctx is None:
        o_attn = _attention(pr["q"], k3, v3, None, None, batch)
    else:
        o_attn = _attention(pr["q"], k_ctx, v_ctx, k3, v3, batch)
    if conv_short:
        z, x0 = _hyena_pre(pr["p_hy"], lw["hy_short_w"], lw["hy_short_b"], batch, None)
        conv = _direct_conv(z, *khat, n)
    else:
        z, x0, zt = _hyena_pre(pr["p_hy"], lw["hy_short_w"], lw["hy_short_b"], batch, DFT_Q)
        conv = _long_conv(zt, khat)
    x_new = _merge(xs, mod, o_attn, conv, z, x0, pr["gates"], lw, tiles_per_group=tiles_per_group,
                   fixed_group=fixed_group)
    return x_new, k3, v3


def kernel(x, c, ctx, c_ctx, w_mod, b_mod, norm1_g, norm2_g, w_in, q_norm_g, kv_norm_g, w_uq, w_uk, w_uv, hy_short_w, hy_short_b, hy_w1, hy_b1, hy_w2, hy_b2, hy_w3, hy_freq, hy_decay, hy_bias, w_br_attn, w_br_hy, w_out, ffn_w_gate, ffn_w_up, ffn_w_down, moe_router, moe_w_gate, moe_w_up, moe_w_down, final_g):
    p = dict(w_in=w_in, norm1_g=norm1_g, norm2_g=norm2_g, q_norm_g=q_norm_g, kv_norm_g=kv_norm_g, w_uq=w_uq,
             w_uk=w_uk, w_uv=w_uv, hy_short_w=hy_short_w, hy_short_b=hy_short_b, hy_w1=hy_w1, hy_b1=hy_b1,
             hy_w2=hy_w2, hy_b2=hy_b2, hy_w3=hy_w3, hy_freq=hy_freq, hy_decay=hy_decay, hy_bias=hy_bias,
             w_br_attn=w_br_attn, w_br_hy=w_br_hy, w_out=w_out)
    batch, seq, d = x.shape
    ctx_len = ctx.shape[1]
    depth = w_mod.shape[0]
    ctx_group = batch
    cond8 = jnp.zeros((SUBLANES, d), F32).at[:batch].set(c).at[ctx_group].set(c_ctx)
    rope_tabs = _rope_tables(seq)
    lat_tiles = seq // ROW_TILE
    xs = x.reshape(batch * seq, d)
    cs = ctx.reshape(batch * ctx_len, d)
    bf = lambda v: v.astype(BF16)
    for layer in range(depth):
        last = layer == depth - 1
        lw = _layer_weights(p, layer)
        mod = _adaln(cond8, w_mod[layer], b_mod[layer])
        khat_lat = _long_conv_spectrum(seq, lw)
        if last:
            pr = _inproj(cs, mod, lw, tiles_per_group=None, fixed_group=ctx_group, rope_tabs=None, want_q=False,
                         want_hg=False)
            k_ctx = pr["k"].reshape(batch, ctx_len, -1)
            v_ctx = pr["v"].reshape(batch, ctx_len // VT_BLOCK, N_HEADS * HEAD_PAD, VT_BLOCK)
        else:
            khat_ctx = _hyena_filter(ctx_len, ctx_len, lw)
            cs_mid, k_ctx, v_ctx = _token_mixer(cs, mod, lw, khat_ctx, True, batch=batch, n=ctx_len,
                                                tiles_per_group=None, fixed_group=ctx_group, rope_tabs=None)
        xs, _, _ = _token_mixer(xs, mod, lw, khat_lat, False, batch=batch, n=seq, tiles_per_group=lat_tiles,
                                fixed_group=None, rope_tabs=rope_tabs, k_ctx=k_ctx, v_ctx=v_ctx)
        i = layer // 2
        n2 = lw["norm2_g"]
        if layer % 2 == 0:
            wg, wu, wd = bf(ffn_w_gate[i])[None], bf(ffn_w_up[i])[None], bf(ffn_w_down[i])[None]
            router = None
        else:
            wg, wu, wd = bf(moe_w_gate[i]), bf(moe_w_up[i]), bf(moe_w_down[i])
            router = jnp.pad(moe_router[i], ((0, 0), (0, LANES - N_EXPERTS)))
        xs = _ffn(xs, mod, n2, wg, wu, wd, tiles_per_group=lat_tiles, fixed_group=None, router=router,
                  final_g=final_g.reshape(1, d) if last else None)
        if not last:
            cs = _ffn(cs_mid, mod, n2, wg, wu, wd, tiles_per_group=None, fixed_group=ctx_group, router=router)
    return xs.reshape(batch, seq, d)
```

```python
import functools
import math

import numpy as np
import jax
import jax.numpy as jnp
from jax import lax
from jax.experimental import pallas as pl
from jax.experimental.pallas import tpu as pltpu

F32 = jnp.float32
BF16 = jnp.bfloat16
HIGHEST = lax.Precision.HIGHEST

D_MODEL = 1024
GRID_W = 64
EPS = 1e-6
N_HEADS = 8
Q_LORA = 384
KV_LORA = 256
QK_NOPE = 64
QK_ROPE = 32
V_HEAD = 64
ROPE_BASE = 10000.0
ATTN_SCALE = (QK_NOPE + QK_ROPE) ** -0.5
HY_WIDTH = 512
HY_EMB = 17
HY_HIDDEN = 64
D_FF = 2816
N_EXPERTS = 8
KV_START = Q_LORA
HY_START = KV_START + KV_LORA + QK_ROPE
GATE_START = HY_START + 3 * HY_WIDTH

LANES = 128
SUBLANES = 8
HEAD_PAD = LANES
VMEM_LIMIT = 56 * 2**20

ROW_TILE = 256
FFN_ROW_TILE = 512
FFN_CHUNK = 1408
MOE_ROW_TILE = 512
VT_BLOCK = ROW_TILE
ATTN_KV = 512
ATTN_Q_TILE = 512
Q_SCALE = ATTN_SCALE * math.log2(math.e)
DFT_P = 64
DFT_Q = 128
COLSLOT_BQ = 8


def _cparams(*sem):
    return pltpu.CompilerParams(dimension_semantics=sem, vmem_limit_bytes=VMEM_LIMIT)


def _dot(a, b):
    return jnp.dot(a, b, preferred_element_type=F32)


def _dot_hi(a, b):
    return jnp.dot(a, b, precision=HIGHEST, preferred_element_type=F32)


def _rms(xf, g):
    return xf * lax.rsqrt(jnp.mean(xf * xf, axis=-1, keepdims=True) + EPS) * g


def _full(shape):
    nd = len(shape)
    return pl.BlockSpec(shape, lambda *_: (0,) * nd)


def _adaln_body(c_ref, w_ref, b_ref, o_ref):
    c = c_ref[...]
    o_ref[...] = _dot_hi(c * jax.nn.sigmoid(c), w_ref[...]) + b_ref[...]


def _adaln(cond8, w, b):
    d, n = w.shape
    return pl.pallas_call(
        _adaln_body,
        grid=(n // d,),
        in_specs=[_full((SUBLANES, d)), pl.BlockSpec((d, d), lambda j: (0, j)),
                  pl.BlockSpec((1, d), lambda j: (0, j))],
        out_specs=pl.BlockSpec((SUBLANES, d), lambda j: (0, j)),
        out_shape=jax.ShapeDtypeStruct((SUBLANES, n), F32),
        compiler_params=_cparams("arbitrary"),
        name="adaln",
    )(cond8, w, b.reshape(1, n))


def _mod_spec(chunk):
    return pl.BlockSpec((SUBLANES, D_MODEL), lambda t, *_: (0, chunk))


def _mod_row(ref, group):
    return ref[pl.ds(group, 1), :]


def _inproj_body(tiles_per_group, fixed_group, use_rope, want_q, want_hg, *refs):
    it = iter(refs)
    x_ref, sh_ref, sc_ref, g1_ref = next(it), next(it), next(it), next(it)
    wkv_ref, kvg_ref, wuk_ref, wuv_ref, wkr_ref = next(it), next(it), next(it), next(it), next(it)
    if want_q:
        wq_ref, qg_ref, wuqa_ref, wuqb_ref = next(it), next(it), next(it), next(it)
    if want_hg:
        why_ref, wgate_ref = next(it), next(it)
    if use_rope:
        cos_ref, sin_ref = next(it), next(it)
    k_out, v_out = next(it), next(it)
    if want_q:
        q_out = next(it)
    if want_hg:
        phy_out, gate_out = next(it), next(it)

    group = fixed_group if fixed_group is not None else pl.program_id(0) // tiles_per_group
    xf = x_ref[...]
    h = _rms(xf, g1_ref[...]) * (1.0 + _mod_row(sc_ref, group)) + _mod_row(sh_ref, group)
    h = h.astype(BF16)
    if use_rope:
        cos, sin = cos_ref[...], sin_ref[...]

    def rope(a, b):
        return a * cos + b * sin if use_rope else a

    ckv = _rms(_dot(h, wkv_ref[...]), kvg_ref[...]).astype(BF16)
    vt = lax.dot_general(wuv_ref[...], ckv, (((1,), (1,)), ((), ())), preferred_element_type=F32)
    vrow = lax.broadcasted_iota(jnp.int32, vt.shape, 0) & (HEAD_PAD - 1)
    v_out[0] = jnp.where(vrow == V_HEAD, 1.0, vt).astype(BF16)
    k_nope = _dot(ckv, wuk_ref[...])
    kr = _dot(h, wkr_ref[...])
    k_rope = rope(kr[:, :HEAD_PAD], kr[:, HEAD_PAD:])
    for hd in range(N_HEADS):
        sl = slice(hd * HEAD_PAD, (hd + 1) * HEAD_PAD)
        k_out[:, sl] = (k_nope[:, sl] + k_rope).astype(BF16)
    if want_q:
        qn = _rms(_dot(h, wq_ref[...]), qg_ref[...]).astype(BF16)
        qa = _dot(qn, wuqa_ref[...])
        qb = _dot(qn, wuqb_ref[...]) if use_rope else None
        for hd in range(N_HEADS):
            sl = slice(hd * HEAD_PAD, (hd + 1) * HEAD_PAD)
            q_out[:, sl] = (rope(qa[:, sl], None if qb is None else qb[:, sl]) * Q_SCALE).astype(BF16)
    if want_hg:
        n_hy = why_ref.shape[1]
        for c0 in range(0, n_hy, 512):
            phy_out[:, c0:c0 + 512] = _dot(h, why_ref[:, c0:c0 + 512])
        n_g = wgate_ref.shape[1]
        for c0 in range(0, n_g, 512):
            gate_out[:, c0:c0 + 512] = jax.nn.sigmoid(_dot(h, wgate_ref[:, c0:c0 + 512]))


def _inproj(x, mod, lw, *, tiles_per_group, fixed_group, rope_tabs, want_q, want_hg):
    rows = x.shape[0]
    nt = rows // ROW_TILE
    use_rope = rope_tabs is not None
    row_spec = lambda n: pl.BlockSpec((ROW_TILE, n), lambda t: (t, 0))
    ins = [x, mod, mod, lw["norm1_g"], lw["w_kv"], lw["kv_norm_g"], lw["w_uk"], lw["w_uv"], lw["w_kr"]]
    specs = [row_spec(D_MODEL), _mod_spec(0), _mod_spec(1), _full((1, D_MODEL)),
             _full(lw["w_kv"].shape), _full((1, KV_LORA)), _full(lw["w_uk"].shape), _full(lw["w_uv"].shape),
             _full(lw["w_kr"].shape)]
    if want_q:
        ins += [lw["w_q"], lw["q_norm_g"], lw["w_uq_a"], lw["w_uq_b"]]
        specs += [_full(lw["w_q"].shape), _full((1, Q_LORA)), _full(lw["w_uq_a"].shape), _full(lw["w_uq_b"].shape)]
    if want_hg:
        ins += [lw["w_hy"], lw["w_gate"]]
        specs += [_full(lw["w_hy"].shape), _full(lw["w_gate"].shape)]
    if use_rope:
        seq_tiles = rope_tabs[0].shape[0] // ROW_TILE
        ins += list(rope_tabs)
        specs += [pl.BlockSpec((ROW_TILE, HEAD_PAD), lambda t: (t % seq_tiles, 0))] * 2
    hp = N_HEADS * HEAD_PAD
    out_shape = [jax.ShapeDtypeStruct((rows, hp), BF16), jax.ShapeDtypeStruct((nt, hp, ROW_TILE), BF16)]
    out_specs = [row_spec(hp), pl.BlockSpec((1, hp, ROW_TILE), lambda t: (t, 0, 0))]
    if want_q:
        out_shape.append(jax.ShapeDtypeStruct((rows, hp), BF16))
        out_specs.append(row_spec(hp))
    if want_hg:
        out_shape += [jax.ShapeDtypeStruct((rows, 3 * HY_WIDTH), F32), jax.ShapeDtypeStruct((rows, 2 * D_MODEL), F32)]
        out_specs += [row_spec(3 * HY_WIDTH), row_spec(2 * D_MODEL)]
    outs = pl.pallas_call(
        functools.partial(_inproj_body, tiles_per_group, fixed_group, use_rope, want_q, want_hg),
        grid=(nt,), in_specs=specs, out_specs=out_specs, out_shape=out_shape,
        compiler_params=_cparams("arbitrary"), name="inproj",
    )(*ins)
    res = {"k": outs[0], "v": outs[1]}
    i = 2
    if want_q:
        res["q"] = outs[i]
        i += 1
    if want_hg:
        res["p_hy"], res["gates"] = outs[i], outs[i + 1]
    return res


def _attn_body(n_lat_blocks, *refs):
    if n_lat_blocks:
        q_ref, kc_ref, vc_ref, kl_ref, vl_ref, o_ref = refs
    else:
        q_ref, kc_ref, vc_ref, o_ref = refs
    tq = q_ref.shape[0]
    q = q_ref[...]
    heads = tuple(slice(hd * HEAD_PAD, (hd + 1) * HEAD_PAD) for hd in range(2))
    sub = ATTN_KV // VT_BLOCK

    def scores(kblk):
        return tuple(lax.dot_general(kblk[:, sl], q[:, sl], (((1,), (1,)), ((), ())), preferred_element_type=F32)
                     for sl in heads)

    def update(st, vt_blocks, carry):
        m_new = [jnp.maximum(carry[hd][0], jnp.max(st[hd], axis=0, keepdims=True)) for hd in range(2)]
        p = [jnp.exp2(st[hd] - m_new[hd]).astype(BF16) for hd in range(2)]
        out = []
        for hd in range(2):
            m, acc = carry[hd]
            pv = None
            for j, vt in enumerate(vt_blocks):
                r = _dot(vt[heads[hd], :], p[hd][j * VT_BLOCK:(j + 1) * VT_BLOCK])
                pv = r if pv is None else pv + r
            out.append((m_new[hd], jnp.exp2(m - m_new[hd]) * acc + pv))
        return tuple(out)

    init = tuple((jnp.full((1, tq), -jnp.inf, F32), jnp.zeros((HEAD_PAD, tq), F32)) for _ in range(2))
    carry = update(scores(kc_ref[0]), [vc_ref[0, 0]], init)
    if n_lat_blocks:
        def k_block(i):
            return kl_ref[0, pl.ds(pl.multiple_of(i * ATTN_KV, ATTN_KV), ATTN_KV), :]

        def v_blocks(i):
            return [vl_ref[0, i * sub + j] for j in range(sub)]

        def body(i, c):
            st, carry = c
            st_next = scores(k_block(i + 1))
            return st_next, update(st, v_blocks(i), carry)
        st, carry = lax.fori_loop(0, n_lat_blocks - 1, body, (scores(k_block(0)), carry))
        carry = update(st, v_blocks(n_lat_blocks - 1), carry)
    o_t = jnp.concatenate([acc[:V_HEAD] / acc[V_HEAD:V_HEAD + 1] for _, acc in carry], axis=0)
    o_ref[...] = o_t.T.astype(BF16)


def _attention(q, k_ctx, vt_ctx, k_lat, vt_lat, batch):
    rows = q.shape[0]
    tq = min(ATTN_Q_TILE, rows // batch)
    qt_per_b = rows // batch // tq
    ctx_len = k_ctx.shape[1]
    assert ctx_len == VT_BLOCK
    pair_w = 2 * HEAD_PAD
    ins = [q, k_ctx, vt_ctx]
    specs = [pl.BlockSpec((tq, pair_w), lambda b, hp, t: (b * qt_per_b + t, hp)),
             pl.BlockSpec((1, ctx_len, pair_w), lambda b, hp, t: (b, 0, hp)),
             pl.BlockSpec((1, 1, pair_w, VT_BLOCK), lambda b, hp, t: (b, 0, hp, 0))]
    n_lat_blocks = 0
    if k_lat is not None:
        lat_len = k_lat.shape[1]
        n_lat_blocks = lat_len // ATTN_KV
        ins += [k_lat, vt_lat]
        specs += [pl.BlockSpec((1, lat_len, pair_w), lambda b, hp, t: (b, 0, hp)),
                  pl.BlockSpec((1, lat_len // VT_BLOCK, pair_w, VT_BLOCK), lambda b, hp, t: (b, 0, hp, 0))]
    return pl.pallas_call(
        functools.partial(_attn_body, n_lat_blocks),
        grid=(batch, N_HEADS // 2, qt_per_b), in_specs=specs,
        out_specs=pl.BlockSpec((tq, 2 * V_HEAD), lambda b, hp, t: (b * qt_per_b + t, hp)),
        out_shape=jax.ShapeDtypeStruct((rows, N_HEADS * V_HEAD), BF16),
        compiler_params=_cparams("arbitrary", "arbitrary", "arbitrary"), name="attention",
    )(*ins)


def _hypre_body(seq_tiles, n_chunks, p_ref, prev_ref, next_ref, w_ref, b_ref, z_ref, x0_ref, *zt_ref):
    j = pl.program_id(0) % seq_tiles
    p = p_ref[...]
    tm = p.shape[0]
    row = lax.broadcasted_iota(jnp.int32, (tm, 1), 0)
    prev_row = jnp.where(j != 0, prev_ref[SUBLANES - 1:SUBLANES, :], 0.0)
    next_row = jnp.where(j != seq_tiles - 1, next_ref[0:1, :], 0.0)
    up = jnp.where(row == 0, prev_row, pltpu.roll(p, 1, 0))
    dn = jnp.where(row == tm - 1, next_row, pltpu.roll(p, tm - 1, 0))
    u = up * w_ref[0:1, :] + p * w_ref[1:2, :] + dn * w_ref[2:3, :] + b_ref[...]
    c = HY_WIDTH
    z = u[:, :c] * u[:, c:2 * c]
    z_ref[...] = z
    x0_ref[...] = u[:, 2 * c:]
    if zt_ref:
        q = tm // n_chunks
        for a in range(n_chunks):
            zt_ref[0][0, :, a * c:(a + 1) * c] = z[a * q:(a + 1) * q, :]


def _hyena_pre(p_hy, short_w, short_b, batch, dft_q):
    rows = p_hy.shape[0]
    n = rows // batch
    seq_tiles = n // ROW_TILE
    nt = rows // ROW_TILE
    c3 = 3 * HY_WIDTH
    halo = ROW_TILE // SUBLANES
    last8 = rows // SUBLANES - 1
    specs = [pl.BlockSpec((ROW_TILE, c3), lambda t: (t, 0)),
             pl.BlockSpec((SUBLANES, c3), lambda t: (jnp.maximum(t * halo - 1, 0), 0)),
             pl.BlockSpec((SUBLANES, c3), lambda t: (jnp.minimum((t + 1) * halo, last8), 0)),
             _full((3, c3)), _full((1, c3))]
    out_shape = [jax.ShapeDtypeStruct((rows, HY_WIDTH), F32)] * 2
    out_specs = [pl.BlockSpec((ROW_TILE, HY_WIDTH), lambda t: (t, 0))] * 2
    n_chunks = 1
    if dft_q is not None:
        n_chunks = ROW_TILE // dft_q
        half_p = n // dft_q
        out_shape.append(jax.ShapeDtypeStruct((batch // 2, dft_q, 2 * half_p * HY_WIDTH), F32))

        def zt_map(t):
            b, jt = t // seq_tiles, t % seq_tiles
            return (b // 2, 0, (b % 2) * seq_tiles + jt)
        out_specs.append(pl.BlockSpec((1, dft_q, n_chunks * HY_WIDTH), zt_map))
    return pl.pallas_call(
        functools.partial(_hypre_body, seq_tiles, n_chunks),
        grid=(nt,), in_specs=specs, out_specs=out_specs, out_shape=out_shape,
        compiler_params=_cparams("arbitrary"), name="hyena_pre",
    )(p_hy, p_hy, p_hy, short_w, short_b.reshape(1, c3))


def _filter_body(emb_ref, w1_ref, b1_ref, w2_ref, b2_ref, w3_ref, freq_ref, decay_ref, h0_ref, h1_ref, s_ref):
    a = pl.program_id(0)
    emb = emb_ref[...]
    freq = freq_ref[...]
    h = jnp.sin(freq * (_dot_hi(emb, w1_ref[...]) + b1_ref[...]))
    h = jnp.sin(freq * (_dot_hi(h, w2_ref[...]) + b2_ref[...]))
    h = _dot_hi(h, w3_ref[...]) * jnp.exp(-emb[:, 0:1] * jnp.abs(decay_ref[...]))
    c = HY_WIDTH
    row = lax.broadcasted_iota(jnp.int32, (emb.shape[0], 1), 0)
    h0 = h[:, :c]
    h1 = jnp.where(jnp.logical_and(a == 0, row == 0), 0.0, h[:, c:])
    h0_ref[0] = h0
    h1_ref[0] = h1
    part = jnp.sum(jnp.abs(h0) + jnp.abs(h1), axis=0, keepdims=True)

    @pl.when(a == 0)
    def _():
        s_ref[...] = jnp.zeros_like(s_ref)
    s_ref[...] += part


def _hyena_filter(n, q, lw):
    f32 = F32
    bands = (HY_EMB - 1) // 2
    t = jnp.linspace(0.0, 1.0, n, dtype=f32)[:, None]
    phase = (2.0 * math.pi / n) * jnp.arange(n, dtype=f32)[:, None] * jnp.linspace(1e-4, bands - 1, bands, dtype=f32)
    emb = jnp.concatenate([t, jnp.cos(phase), -jnp.sin(phase), jnp.zeros((n, 32 - HY_EMB), f32)], axis=-1)
    w1 = jnp.concatenate([lw["hy_w1"], jnp.zeros((32 - HY_EMB, HY_HIDDEN), f32)], axis=0)
    c = HY_WIDTH
    slots = n // q
    row = lambda v: v.reshape(1, -1)
    return pl.pallas_call(
        _filter_body,
        grid=(slots,),
        in_specs=[pl.BlockSpec((q, 32), lambda a: (a, 0)), _full((32, HY_HIDDEN)), _full((1, HY_HIDDEN)),
                  _full((HY_HIDDEN, HY_HIDDEN)), _full((1, HY_HIDDEN)), _full((HY_HIDDEN, 2 * c)),
                  _full((1, HY_HIDDEN)), _full((1, 2 * c))],
        out_specs=[pl.BlockSpec((1, q, c), lambda a: (0, 0, a)), pl.BlockSpec((1, q, c), lambda a: (0, 0, a)),
                   _full((1, c))],
        out_shape=[jax.ShapeDtypeStruct((1, q, slots * c), f32)] * 2 + [jax.ShapeDtypeStruct((1, c), f32)],
        compiler_params=_cparams("arbitrary"), name="hyena_filter",
    )(emb, w1, row(lw["hy_b1"]), lw["hy_w2"], row(lw["hy_b2"]), lw["hy_w3"], row(lw["hy_freq"]),
      row(lw["hy_decay"]))


def _colslot_body(nparts, bq, c, *refs):
    x_refs, t_refs, o_ref = refs[:nparts], refs[nparts:2 * nparts], refs[-1]
    for j in range(bq):
        acc = None
        for x_ref, t_ref in zip(x_refs, t_refs):
            r = _dot_hi(t_ref[j], x_ref[0, j])
            acc = r if acc is None else acc + r
        o_ref[0, :, j * c:(j + 1) * c] = acc


def _colslot_matmul(xs, tabs, c=HY_WIDTH):
    g, q = xs[0].shape[0], xs[0].shape[1]
    m = tabs[0].shape[1]
    bq = COLSLOT_BQ
    xs4 = [x.reshape(g, q, -1, c) for x in xs]
    specs = [pl.BlockSpec((1, bq, x.shape[2], c), lambda gi, qi: (gi, qi, 0, 0)) for x in xs4]
    specs += [pl.BlockSpec((bq, m, t.shape[2]), lambda gi, qi: (qi, 0, 0)) for t in tabs]
    return pl.pallas_call(
        functools.partial(_colslot_body, len(xs), bq, c),
        grid=(g, q // bq), in_specs=specs,
        out_specs=pl.BlockSpec((1, m, bq * c), lambda gi, qi: (gi, 0, qi)),
        out_shape=jax.ShapeDtypeStruct((g, m, q * c), F32),
        compiler_params=_cparams("arbitrary", "arbitrary"), name="dft_stride_stage",
    )(*xs4, *tabs)


def _spec_mid_body(g_ref, m3_ref, m3i_ref, k_ref, hr_ref, hi_ref):
    q = hr_ref.shape[1]
    x = _dot_hi(m3_ref[...], g_ref[0, :, 0].reshape(2 * q, -1))
    kh = k_ref[0]
    xr, xi, kr, ki = x[:q], x[q:], kh[:q], kh[q:]
    y = jnp.concatenate([xr * kr - xi * ki, xr * ki + xi * kr], axis=0)
    h = _dot_hi(m3i_ref[...], y)
    hr_ref[0] = h[:q]
    hi_ref[0] = h[q:]


def _spec_mid(gf, m3, m3i, khat, p, q, c=HY_WIDTH):
    bp = gf.shape[0]
    g5 = gf.reshape(bp, 2, p, q, c)
    return pl.pallas_call(
        _spec_mid_body,
        grid=(bp, p),
        in_specs=[pl.BlockSpec((1, 2, 1, q, c), lambda b, ci: (b, 0, ci, 0, 0)), _full((2 * q, 2 * q)),
                  _full((2 * q, 2 * q)), pl.BlockSpec((1, 2 * q, c), lambda b, ci: (ci, 0, 0))],
        out_specs=[pl.BlockSpec((1, q, c), lambda b, ci: (b, 0, ci))] * 2,
        out_shape=[jax.ShapeDtypeStruct((bp, q, p * c), F32)] * 2,
        compiler_params=_cparams("arbitrary", "arbitrary"), name="dft_mid",
    )(g5, m3, m3i, khat)


def _filter_spec_body(g_ref, m3k_ref, s_ref, k_ref):
    q4 = m3k_ref.shape[1]
    k_ref[0] = _dot_hi(m3k_ref[...], g_ref[0, :, 0].reshape(q4, -1)) / s_ref[...]


def _filter_spec(gk, m3k, s, p, q, c=HY_WIDTH):
    g5 = gk.reshape(1, 4, p, q, c)
    return pl.pallas_call(
        _filter_spec_body,
        grid=(p,),
        in_specs=[pl.BlockSpec((1, 4, 1, q, c), lambda ci: (0, 0, ci, 0, 0)), _full((2 * q, 4 * q)), _full((1, c))],
        out_specs=pl.BlockSpec((1, 2 * q, c), lambda ci: (ci, 0, 0)),
        out_shape=jax.ShapeDtypeStruct((p, 2 * q, c), F32),
        compiler_params=_cparams("arbitrary"), name="filter_spectrum",
    )(g5, m3k, s)


def _direct_conv_body(z_ref, h0_ref, h1_ref, s_ref, tf_ref, tk_ref, ti_ref, o_ref):
    n2 = tf_ref.shape[0] // 2
    kh = _dot_hi(tk_ref[...], jnp.concatenate([h0_ref[0], h1_ref[0]], axis=0)) / s_ref[...]
    x = _dot_hi(tf_ref[...], z_ref[0])
    xr, xi, kr, ki = x[:n2], x[n2:], kh[:n2], kh[n2:]
    y = jnp.concatenate([xr * kr - xi * ki, xr * ki + xi * kr], axis=0)
    o_ref[0] = _dot_hi(ti_ref[...], y)


def _direct_conv(z, h0, h1, s, n, c=HY_WIDTH):
    bp = z.shape[0] // (2 * n)
    tf, tk, ti = _direct_tables(n)
    out = pl.pallas_call(
        _direct_conv_body,
        grid=(bp,),
        in_specs=[pl.BlockSpec((1, 2 * n, c), lambda b: (b, 0, 0)), _full((1, n, c)), _full((1, n, c)), _full((1, c)),
                  _full(tf.shape), _full(tk.shape), _full(ti.shape)],
        out_specs=pl.BlockSpec((1, 2 * n, c), lambda b: (b, 0, 0)),
        out_shape=jax.ShapeDtypeStruct((bp, 2 * n, c), F32),
        compiler_params=_cparams("arbitrary"), name="direct_conv",
    )(z.reshape(bp, 2 * n, c), h0, h1, s, tf, tk, ti)
    return out.reshape(bp * 2 * n, c)


def _cis(m, n_total):
    ang = (2.0 * np.pi / n_total) * (m % n_total).astype(np.float64)
    return np.cos(ang), -np.sin(ang)


@functools.lru_cache(maxsize=None)
def _direct_tables(n):
    nn = 2 * n
    f = np.arange(nn)[:, None]
    t = np.arange(n)[None, :]
    cr, ci = _cis(f * t, nn)
    tf = np.block([[cr, -ci], [ci, cr]])
    tk = np.block([[cr, cr], [ci, -ci]])
    ti = np.block([[cr.T, ci.T], [-ci.T, cr.T]]) / nn
    return tuple(a.astype(np.float32) for a in (tf, tk, ti))


@functools.lru_cache(maxsize=None)
def _twostage_tables(p, q):
    nn = p * q
    hp = p // 2
    s = np.arange(q)[:, None, None]
    c = np.arange(p)[None, :, None]
    a = np.arange(hp)[None, None, :]
    tr, ti = _cis(c * (q * a + s), nn)
    z = np.zeros_like(tr)
    t_data = np.concatenate([np.concatenate([tr, -ti], 2), np.concatenate([ti, tr], 2)], 1)
    t_k0 = np.concatenate([tr, ti, z, z], 1)
    t_k1 = np.concatenate([z, z, tr, -ti], 1)
    trt, tit = np.swapaxes(tr, 1, 2) / nn, np.swapaxes(ti, 1, 2) / nn
    t_inv_r = np.concatenate([trt, -tit], 1)
    t_inv_i = np.concatenate([tit, trt], 1)
    d = np.arange(q)[:, None]
    b = np.arange(q)[None, :]
    fr, fi = _cis(d * b, q)
    m3 = np.block([[fr, -fi], [fi, fr]])
    m3i = np.block([[fr, fi], [-fi, fr]])
    m3k = np.block([[fr, -fi, fr, fi], [fi, fr, -fi, fr]])
    return tuple(x.astype(np.float32) for x in (t_data, t_k0, t_k1, t_inv_r, t_inv_i, m3, m3i, m3k))


def _long_conv_spectrum(n, lw):
    p, q = DFT_P, DFT_Q
    h0, h1, s = _hyena_filter(n, q, lw)
    _, t_k0, t_k1, _, _, _, _, m3k = _twostage_tables(p, q)
    gk = _colslot_matmul([h0, h1], [t_k0, t_k1])
    return _filter_spec(gk, m3k, s, p, q)


def _long_conv(zt, khat):
    p, q = DFT_P, DFT_Q
    t_data, _, _, t_inv_r, t_inv_i, m3, m3i, _ = _twostage_tables(p, q)
    gf = _colslot_matmul([zt], [t_data])
    hr, hi = _spec_mid(gf, m3, m3i, khat, p, q)
    y = _colslot_matmul([hr, hi], [t_inv_r, t_inv_i])
    return y.reshape(-1, HY_WIDTH)


def _merge_body(tiles_per_group, fixed_group, x_ref, g1_ref, oa_ref, conv_ref, z_ref, x0_ref, gate_ref, bias_ref,
                wba_ref, wbh_ref, wout_ref, o_ref):
    group = fixed_group if fixed_group is not None else pl.program_id(0) // tiles_per_group
    o_hy = ((conv_ref[...] + z_ref[...] * bias_ref[...]) * x0_ref[...]).astype(BF16)
    d = D_MODEL
    merged = gate_ref[:, :d] * _dot(oa_ref[...], wba_ref[...]) + gate_ref[:, d:] * _dot(o_hy, wbh_ref[...])
    mix = _dot(merged.astype(BF16), wout_ref[...])
    o_ref[...] = x_ref[...] + _mod_row(g1_ref, group) * mix


def _merge(x, mod, o_attn, conv, z, x0, gates, lw, *, tiles_per_group, fixed_group):
    rows = x.shape[0]
    row_spec = lambda n: pl.BlockSpec((ROW_TILE, n), lambda t: (t, 0))
    c = HY_WIDTH
    return pl.pallas_call(
        functools.partial(_merge_body, tiles_per_group, fixed_group),
        grid=(rows // ROW_TILE,),
        in_specs=[row_spec(D_MODEL), _mod_spec(2), row_spec(N_HEADS * V_HEAD), row_spec(c), row_spec(c), row_spec(c),
                  row_spec(2 * D_MODEL), _full((1, c)), _full(lw["w_br_attn"].shape), _full(lw["w_br_hy"].shape),
                  _full(lw["w_out"].shape)],
        out_specs=row_spec(D_MODEL),
        out_shape=jax.ShapeDtypeStruct((rows, D_MODEL), F32),
        compiler_params=_cparams("arbitrary"), name="merge_out",
    )(x, mod, o_attn, conv, z, x0, gates, lw["hy_bias"], lw["w_br_attn"], lw["w_br_hy"], lw["w_out"])


def _swiglu_chunk(h, wg_ref, wu_ref, wd_ref):
    gate = _dot(h, wg_ref[0])
    up = _dot(h, wu_ref[0])
    return _dot((gate * jax.nn.sigmoid(gate) * up).astype(BF16), wd_ref[0])


def _ffn_body(tiles_per_group, fixed_group, x_ref, sh_ref, sc_ref, g2_ref, n2_ref, wg_ref, wu_ref, wd_ref, o_ref,
              h_scr, acc_scr):
    group = fixed_group if fixed_group is not None else pl.program_id(0) // tiles_per_group
    ci = pl.program_id(1)

    @pl.when(ci == 0)
    def _():
        h = _rms(x_ref[...], n2_ref[...]) * (1.0 + _mod_row(sc_ref, group)) + _mod_row(sh_ref, group)
        h_scr[...] = h.astype(BF16)
        acc_scr[...] = jnp.zeros_like(acc_scr)

    acc_scr[...] += _swiglu_chunk(h_scr[...], wg_ref, wu_ref, wd_ref)

    @pl.when(ci == pl.num_programs(1) - 1)
    def _():
        o_ref[...] = x_ref[...] + _mod_row(g2_ref, group) * acc_scr[...]


def _ffn(x, mod, norm_g, w_gate, w_up, w_down, *, tiles_per_group, fixed_group):
    rows = x.shape[0]
    tm = min(FFN_ROW_TILE, rows)
    n_chunks = D_FF // FFN_CHUNK
    row_spec = pl.BlockSpec((tm, D_MODEL), lambda t, ci: (t, 0))
    return pl.pallas_call(
        functools.partial(_ffn_body, tiles_per_group * ROW_TILE // tm if tiles_per_group else None, fixed_group),
        grid=(rows // tm, n_chunks),
        in_specs=[row_spec, _mod_spec(3), _mod_spec(4), _mod_spec(5), _full((1, D_MODEL)),
                  pl.BlockSpec((1, D_MODEL, FFN_CHUNK), lambda t, ci: (0, 0, ci)),
                  pl.BlockSpec((1, D_MODEL, FFN_CHUNK), lambda t, ci: (0, 0, ci)),
                  pl.BlockSpec((1, FFN_CHUNK, D_MODEL), lambda t, ci: (0, ci, 0))],
        out_specs=row_spec,
        out_shape=jax.ShapeDtypeStruct((rows, D_MODEL), F32),
        scratch_shapes=[pltpu.VMEM((tm, D_MODEL), BF16), pltpu.VMEM((tm, D_MODEL), F32)],
        compiler_params=_cparams("arbitrary", "arbitrary"), name="ffn",
    )(x, mod, mod, mod, norm_g, w_gate, w_up, w_down)


def _route_body(tiles_per_group, x_ref, sh_ref, sc_ref, n2_ref, router_ref, h_ref, route_ref):
    group = pl.program_id(0) // tiles_per_group
    h = _rms(x_ref[...], n2_ref[...]) * (1.0 + _mod_row(sc_ref, group)) + _mod_row(sh_ref, group)
    h_ref[...] = h
    logits = _dot_hi(h, router_ref[...])
    lane = lax.broadcasted_iota(jnp.int32, logits.shape, 1)
    logits = jnp.where(lane < N_EXPERTS, logits, -jnp.inf)
    m1 = jnp.max(logits, axis=-1, keepdims=True)
    i1 = jnp.min(jnp.where(logits == m1, lane, LANES), axis=-1, keepdims=True)
    rest = jnp.where(lane == i1, -jnp.inf, logits)
    m2 = jnp.max(rest, axis=-1, keepdims=True)
    i2 = jnp.min(jnp.where(rest == m2, lane, LANES), axis=-1, keepdims=True)
    e2 = jnp.exp(m2 - m1)
    w1 = 1.0 / (1.0 + e2)
    route_ref[...] = (jnp.where(lane == 0, i1.astype(F32), 0.0) + jnp.where(lane == 1, i2.astype(F32), 0.0)
                      + jnp.where(lane == 2, w1, 0.0) + jnp.where(lane == 3, e2 * w1, 0.0))


def _route(x, mod, norm_g, router, tiles_per_group):
    rows = x.shape[0]
    row_spec = lambda n: pl.BlockSpec((ROW_TILE, n), lambda t: (t, 0))
    return pl.pallas_call(
        functools.partial(_route_body, tiles_per_group),
        grid=(rows // ROW_TILE,),
        in_specs=[row_spec(D_MODEL), _mod_spec(3), _mod_spec(4), _full((1, D_MODEL)), _full(router.shape)],
        out_specs=[row_spec(D_MODEL), row_spec(LANES)],
        out_shape=[jax.ShapeDtypeStruct((rows, D_MODEL), F32), jax.ShapeDtypeStruct((rows, LANES), F32)],
        compiler_params=_cparams("arbitrary"), name="moe_route",
    )(x, mod, mod, norm_g, router)


def _gather_rows(idx_ref, n_rows, src_hbm, dst, sem):
    def issue(r, carry):
        pltpu.make_async_copy(src_hbm.at[pl.ds(idx_ref[0, 0, r], 1), :], dst.at[pl.ds(r, 1), :], sem).start()
        return carry
    lax.fori_loop(0, n_rows, issue, 0, unroll=8)


def _wait_rows(n_rows, src_hbm, dst, sem):
    pltpu.make_async_copy(src_hbm.at[pl.ds(0, n_rows), :], dst, sem).wait()


def _moe_group_body(te_ref, nu_ref, idx_ref, idx_next_ref, h_hbm, w_ref, wg_ref, wu_ref, wd_ref, y_ref,
                    xg_scr, xb_scr, acc_scr, sem):
    t, ci = pl.program_id(0), pl.program_id(1)
    n_tiles = pl.num_programs(0)
    tm = xb_scr.shape[0]
    slot = t % 2

    @pl.when(jnp.logical_and(t == 0, ci == 0))
    def _():
        _gather_rows(idx_ref, tm, h_hbm, xg_scr.at[0], sem.at[0])

    @pl.when(ci == 0)
    def _():
        _wait_rows(tm, h_hbm, xg_scr.at[slot], sem.at[slot])
        xb_scr[...] = xg_scr[slot].astype(BF16)
        acc_scr[...] = jnp.zeros_like(acc_scr)

        @pl.when(t + 1 < n_tiles)
        def _():
            _gather_rows(idx_next_ref, tm, h_hbm, xg_scr.at[1 - slot], sem.at[1 - slot])

    @pl.when(t < nu_ref[0])
    def _():
        acc_scr[...] += _swiglu_chunk(xb_scr[...], wg_ref, wu_ref, wd_ref)

    @pl.when(ci == pl.num_programs(1) - 1)
    def _():
        y_ref[...] = acc_scr[...] * w_ref[...]


def _moe_group(h, src, w_sorted, tile_expert, n_used, w_gate, w_up, w_down):
    tm = MOE_ROW_TILE
    n_rows = src.shape[0]
    n_tiles = n_rows // tm
    n_chunks = D_FF // FFN_CHUNK
    src3 = src.reshape(n_tiles, 1, tm)

    def chunk(t, ci, nu):
        return jnp.where(t < nu[0], ci, n_chunks - 1)
    grid_spec = pltpu.PrefetchScalarGridSpec(
        num_scalar_prefetch=2,
        grid=(n_tiles, n_chunks),
        in_specs=[pl.BlockSpec((1, 1, tm), lambda t, ci, te, nu: (t, 0, 0), memory_space=pltpu.SMEM),
                  pl.BlockSpec((1, 1, tm), lambda t, ci, te, nu: (jnp.minimum(t + 1, n_tiles - 1), 0, 0),
                               memory_space=pltpu.SMEM),
                  pl.BlockSpec(memory_space=pl.ANY),
                  pl.BlockSpec((tm, 1), lambda t, ci, te, nu: (t, 0)),
                  pl.BlockSpec((1, D_MODEL, FFN_CHUNK), lambda t, ci, te, nu: (te[t], 0, chunk(t, ci, nu))),
                  pl.BlockSpec((1, D_MODEL, FFN_CHUNK), lambda t, ci, te, nu: (te[t], 0, chunk(t, ci, nu))),
                  pl.BlockSpec((1, FFN_CHUNK, D_MODEL), lambda t, ci, te, nu: (te[t], chunk(t, ci, nu), 0))],
        out_specs=pl.BlockSpec((tm, D_MODEL), lambda t, ci, te, nu: (t, 0)),
        scratch_shapes=[pltpu.VMEM((2, tm, D_MODEL), F32), pltpu.VMEM((tm, D_MODEL), BF16),
                        pltpu.VMEM((tm, D_MODEL), F32), pltpu.SemaphoreType.DMA((2,))],
    )
    return pl.pallas_call(
        _moe_group_body, grid_spec=grid_spec,
        out_shape=jax.ShapeDtypeStruct((n_rows, D_MODEL), F32),
        compiler_params=_cparams("arbitrary", "arbitrary"), name="moe_group",
    )(tile_expert, n_used, src3, src3, h, w_sorted.reshape(n_rows, 1), w_gate, w_up, w_down)


def _moe_combine_body(tiles_per_group, idx_ref, idx_next_ref, x_ref, g2_ref, fg_ref, y_hbm, o_ref, yg_scr, sem):
    t = pl.program_id(0)
    n_tiles = pl.num_programs(0)
    group = t // tiles_per_group
    tm = x_ref.shape[0]
    slot = t % 2

    @pl.when(t == 0)
    def _():
        _gather_rows(idx_ref, 2 * tm, y_hbm, yg_scr.at[0], sem.at[0])

    _wait_rows(2 * tm, y_hbm, yg_scr.at[slot], sem.at[slot])

    @pl.when(t + 1 < n_tiles)
    def _():
        _gather_rows(idx_next_ref, 2 * tm, y_hbm, yg_scr.at[1 - slot], sem.at[1 - slot])

    y = yg_scr[slot, :tm] + yg_scr[slot, tm:]
    o_ref[...] = _rms(x_ref[...] + _mod_row(g2_ref, group) * y, fg_ref[...])


def _moe_combine(x, mod, final_g, y_sorted, dest, tiles_per_group):
    rows = x.shape[0]
    tm = ROW_TILE
    n_tiles = rows // tm
    dest3 = dest.reshape(n_tiles, tm, 2).transpose(0, 2, 1).reshape(n_tiles, 1, 2 * tm)
    return pl.pallas_call(
        functools.partial(_moe_combine_body, tiles_per_group),
        grid=(n_tiles,),
        in_specs=[pl.BlockSpec((1, 1, 2 * tm), lambda t: (t, 0, 0), memory_space=pltpu.SMEM),
                  pl.BlockSpec((1, 1, 2 * tm), lambda t: (jnp.minimum(t + 1, n_tiles - 1), 0, 0),
                               memory_space=pltpu.SMEM),
                  pl.BlockSpec((tm, D_MODEL), lambda t: (t, 0)), _mod_spec(5), _full((1, D_MODEL)),
                  pl.BlockSpec(memory_space=pl.ANY)],
        out_specs=pl.BlockSpec((tm, D_MODEL), lambda t: (t, 0)),
        out_shape=jax.ShapeDtypeStruct((rows, D_MODEL), F32),
        scratch_shapes=[pltpu.VMEM((2, 2 * tm, D_MODEL), F32), pltpu.SemaphoreType.DMA((2,))],
        compiler_params=_cparams("arbitrary"), name="moe_combine",
    )(dest3, dest3, x, mod, final_g, y_sorted)


def _moe_plan(route, tm):
    t = route.shape[0]
    experts = jnp.concatenate([route[:, 0], route[:, 1]]).astype(jnp.int32)
    weights = jnp.concatenate([route[:, 2], route[:, 3]])
    onehot = (experts[:, None] == jnp.arange(N_EXPERTS, dtype=jnp.int32)[None, :]).astype(jnp.int32)
    csum = jnp.cumsum(onehot, axis=0)
    rank = jnp.take_along_axis(csum, experts[:, None], axis=1)[:, 0] - 1
    padded = (csum[-1] + tm - 1) // tm * tm
    ends = jnp.cumsum(padded)
    dest = (ends - padded)[experts] + rank
    n_rows = 2 * t + N_EXPERTS * tm
    token = jnp.arange(2 * t, dtype=jnp.int32) % t
    src = jnp.zeros((n_rows,), jnp.int32).at[dest].set(token, unique_indices=True)
    w_sorted = jnp.zeros((n_rows,), F32).at[dest].set(weights, unique_indices=True)
    tile_start = jnp.arange(n_rows // tm, dtype=jnp.int32) * tm
    tile_expert = jnp.minimum(jnp.searchsorted(ends, tile_start, side="right"), N_EXPERTS - 1).astype(jnp.int32)
    n_used = (ends[-1:] // tm).astype(jnp.int32)
    return src, w_sorted, tile_expert, n_used, dest.reshape(2, t).T


def _moe(x, mod, norm_g, router, w_gate, w_up, w_down, final_g, tiles_per_group):
    h, route = _route(x, mod, norm_g, router, tiles_per_group)
    src, w_sorted, tile_expert, n_used, dest = _moe_plan(route, MOE_ROW_TILE)
    y_sorted = _moe_group(h, src, w_sorted, tile_expert, n_used, w_gate, w_up, w_down)
    return _moe_combine(x, mod, final_g, y_sorted, dest, tiles_per_group)


def _pad_heads(w, width):
    k = w.shape[0]
    w = w.reshape(k, N_HEADS, width)
    return jnp.pad(w, ((0, 0), (0, 0), (0, HEAD_PAD - width))).reshape(k, N_HEADS * HEAD_PAD)


def _rot_cols(w):
    half = QK_ROPE // 2
    return jnp.concatenate([-w[..., half:], w[..., :half]], axis=-1)


def _layer_weights(p, layer):
    w_in = p["w_in"][layer]
    w_uq = p["w_uq"][layer].reshape(Q_LORA, N_HEADS, QK_NOPE + QK_ROPE)
    zeros_nope = jnp.zeros((Q_LORA, N_HEADS, QK_NOPE), F32)
    w_uq_b = jnp.concatenate([zeros_nope, _rot_cols(w_uq[..., QK_NOPE:])], axis=-1)
    w_kr = w_in[:, KV_START + KV_LORA:HY_START]
    zk = jnp.zeros((D_MODEL, QK_NOPE), F32)
    zp = jnp.zeros((D_MODEL, HEAD_PAD - QK_NOPE - QK_ROPE), F32)
    w_kr2 = jnp.concatenate([zk, w_kr, zp, zk, _rot_cols(w_kr), zp], axis=-1)
    row = lambda v: v.reshape(1, -1)
    bf = lambda v: v.astype(BF16)
    return {
        "norm1_g": row(p["norm1_g"][layer]), "norm2_g": row(p["norm2_g"][layer]),
        "w_q": bf(w_in[:, :Q_LORA]), "q_norm_g": row(p["q_norm_g"][layer]),
        "w_uq_a": bf(_pad_heads(w_uq.reshape(Q_LORA, -1), QK_NOPE + QK_ROPE)),
        "w_uq_b": bf(_pad_heads(w_uq_b.reshape(Q_LORA, -1), QK_NOPE + QK_ROPE)),
        "w_kv": bf(w_in[:, KV_START:KV_START + KV_LORA]), "kv_norm_g": row(p["kv_norm_g"][layer]),
        "w_uk": bf(_pad_heads(p["w_uk"][layer], QK_NOPE)), "w_uv": bf(_pad_heads(p["w_uv"][layer], V_HEAD).T),
        "w_kr": bf(w_kr2),
        "w_hy": bf(w_in[:, HY_START:GATE_START]), "w_gate": bf(w_in[:, GATE_START:]),
        "hy_short_w": p["hy_short_w"][layer], "hy_short_b": p["hy_short_b"][layer],
        "hy_w1": p["hy_w1"][layer], "hy_b1": p["hy_b1"][layer], "hy_w2": p["hy_w2"][layer],
        "hy_b2": p["hy_b2"][layer], "hy_w3": p["hy_w3"][layer], "hy_freq": p["hy_freq"][layer],
        "hy_decay": p["hy_decay"][layer], "hy_bias": row(p["hy_bias"][layer]),
        "w_br_attn": bf(p["w_br_attn"][layer]), "w_br_hy": bf(p["w_br_hy"][layer]), "w_out": bf(p["w_out"][layer]),
    }


def _rope_tables(n):
    rows = n // GRID_W
    n_freq = QK_ROPE // 4
    inv_freq = ROPE_BASE ** (-jnp.arange(n_freq, dtype=F32) / n_freq)
    r = jnp.repeat(jnp.arange(rows, dtype=F32), GRID_W)
    col = jnp.tile(jnp.arange(GRID_W, dtype=F32), rows)
    ang = jnp.concatenate([r[:, None] * inv_freq, col[:, None] * inv_freq], axis=-1)
    cos, sin = jnp.cos(ang), jnp.sin(ang)
    ones = jnp.ones((n, QK_NOPE), F32)
    zeros = jnp.zeros((n, QK_NOPE), F32)
    pad = jnp.zeros((n, HEAD_PAD - QK_NOPE - QK_ROPE), F32)
    return (jnp.concatenate([ones, cos, cos, pad], axis=-1), jnp.concatenate([zeros, sin, sin, pad], axis=-1))


def _token_mixer(xs, mod, lw, khat, conv_short, *, batch, n, tiles_per_group, fixed_group, rope_tabs,
                 k_ctx=None, v_ctx=None):
    pr = _inproj(xs, mod, lw, tiles_per_group=tiles_per_group, fixed_group=fixed_group, rope_tabs=rope_tabs,
                 want_q=True, want_hg=True)
    k3 = pr["k"].reshape(batch, n, -1)
    v3 = pr["v"].reshape(batch, n // VT_BLOCK, N_HEADS * HEAD_PAD, VT_BLOCK)
    if k_ctx is None:
        o_attn = _attention(pr["q"], k3, v3, None, None, batch)
    else:
        o_attn = _attention(pr["q"], k_ctx, v_ctx, k3, v3, batch)
    if conv_short:
        z, x0 = _hyena_pre(pr["p_hy"], lw["hy_short_w"], lw["hy_short_b"], batch, None)
        conv = _direct_conv(z, *khat, n)
    else:
        z, x0, zt = _hyena_pre(pr["p_hy"], lw["hy_short_w"], lw["hy_short_b"], batch, DFT_Q)
        conv = _long_conv(zt, khat)
    x_new = _merge(xs, mod, o_attn, conv, z, x0, pr["gates"], lw, tiles_per_group=tiles_per_group,
                   fixed_group=fixed_group)
    return x_new, k3, v3


def kernel(x, c, ctx, c_ctx, w_mod, b_mod, norm1_g, norm2_g, w_in, q_norm_g, kv_norm_g, w_uq, w_uk, w_uv, hy_short_w, hy_short_b, hy_w1, hy_b1, hy_w2, hy_b2, hy_w3, hy_freq, hy_decay, hy_bias, w_br_attn, w_br_hy, w_out, ffn_w_gate, ffn_w_up, ffn_w_down, moe_router, moe_w_gate, moe_w_up, moe_w_down, final_g):
    p = dict(w_in=w_in, norm1_g=norm1_g, norm2_g=norm2_g, q_norm_g=q_norm_g, kv_norm_g=kv_norm_g, w_uq=w_uq,
             w_uk=w_uk, w_uv=w_uv, hy_short_w=hy_short_w, hy_short_b=hy_short_b, hy_w1=hy_w1, hy_b1=hy_b1,
             hy_w2=hy_w2, hy_b2=hy_b2, hy_w3=hy_w3, hy_freq=hy_freq, hy_decay=hy_decay, hy_bias=hy_bias,
             w_br_attn=w_br_attn, w_br_hy=w_br_hy, w_out=w_out)
    batch, seq, d = x.shape
    ctx_len = ctx.shape[1]
    depth = w_mod.shape[0]
    ctx_group = batch
    cond8 = jnp.zeros((SUBLANES, d), F32).at[:batch].set(c).at[ctx_group].set(c_ctx)
    rope_tabs = _rope_tables(seq)
    lat_tiles = seq // ROW_TILE
    xs = x.reshape(batch * seq, d)
    cs = ctx.reshape(batch * ctx_len, d)
    bf = lambda v: v.astype(BF16)
    for layer in range(depth):
        last = layer == depth - 1
        lw = _layer_weights(p, layer)
        mod = _adaln(cond8, w_mod[layer], b_mod[layer])
        khat_lat = _long_conv_spectrum(seq, lw)
        if last:
            pr = _inproj(cs, mod, lw, tiles_per_group=None, fixed_group=ctx_group, rope_tabs=None, want_q=False,
                         want_hg=False)
            k_ctx = pr["k"].reshape(batch, ctx_len, -1)
            v_ctx = pr["v"].reshape(batch, ctx_len // VT_BLOCK, N_HEADS * HEAD_PAD, VT_BLOCK)
        else:
            khat_ctx = _hyena_filter(ctx_len, ctx_len, lw)
            cs_mid, k_ctx, v_ctx = _token_mixer(cs, mod, lw, khat_ctx, True, batch=batch, n=ctx_len,
                                                tiles_per_group=None, fixed_group=ctx_group, rope_tabs=None)
        xs, _, _ = _token_mixer(xs, mod, lw, khat_lat, False, batch=batch, n=seq, tiles_per_group=lat_tiles,
                                fixed_group=None, rope_tabs=rope_tabs, k_ctx=k_ctx, v_ctx=v_ctx)
        i = layer // 2
        n2 = lw["norm2_g"]
        if layer % 2 == 0:
            assert not last
            wg, wu, wd = bf(ffn_w_gate[i])[None], bf(ffn_w_up[i])[None], bf(ffn_w_down[i])[None]
            xs = _ffn(xs, mod, n2, wg, wu, wd, tiles_per_group=lat_tiles, fixed_group=None)
            cs = _ffn(cs_mid, mod, n2, wg, wu, wd, tiles_per_group=None, fixed_group=ctx_group)
        else:
            assert last
            router = jnp.pad(moe_router[i], ((0, 0), (0, LANES - N_EXPERTS)))
            xs = _moe(xs, mod, n2, router, bf(moe_w_gate[i]), bf(moe_w_up[i]), bf(moe_w_down[i]),
                      final_g.reshape(1, d), lat_tiles)
    return xs.reshape(batch, seq, d)
```

```python
import functools
import math

import numpy as np
import jax
import jax.numpy as jnp
from jax import lax
from jax.experimental import pallas as pl
from jax.experimental.pallas import tpu as pltpu

F32 = jnp.float32
BF16 = jnp.bfloat16
HIGHEST = lax.Precision.HIGHEST

D_MODEL = 1024
GRID_W = 64
EPS = 1e-6
N_HEADS = 8
Q_LORA = 384
KV_LORA = 256
QK_NOPE = 64
QK_ROPE = 32
V_HEAD = 64
ROPE_BASE = 10000.0
ATTN_SCALE = (QK_NOPE + QK_ROPE) ** -0.5
HY_WIDTH = 512
HY_EMB = 17
HY_HIDDEN = 64
D_FF = 2816
N_EXPERTS = 8
KV_START = Q_LORA
HY_START = KV_START + KV_LORA + QK_ROPE
GATE_START = HY_START + 3 * HY_WIDTH

LANES = 128
SUBLANES = 8
HEAD_PAD = LANES
VMEM_LIMIT = 56 * 2**20

ROW_TILE = 256
FFN_ROW_TILE = 512
FFN_CHUNK = 1408
MOE_ROW_TILE = 512
VT_BLOCK = ROW_TILE
ATTN_KV = 512
ATTN_Q_TILE = 512
Q_SCALE = ATTN_SCALE * math.log2(math.e)
DFT_P = 64
DFT_Q = 128
COLSLOT_BQ = 8


def _cparams(*sem):
    return pltpu.CompilerParams(dimension_semantics=sem, vmem_limit_bytes=VMEM_LIMIT)


def _dot(a, b):
    return jnp.dot(a, b, preferred_element_type=F32)


def _dot_hi(a, b):
    return jnp.dot(a, b, precision=HIGHEST, preferred_element_type=F32)


def _dot_dft(table, x):
    return _dot(table, x.astype(BF16))


def _rms(xf, g):
    return xf * lax.rsqrt(jnp.mean(xf * xf, axis=-1, keepdims=True) + EPS) * g


def _full(shape):
    nd = len(shape)
    return pl.BlockSpec(shape, lambda *_: (0,) * nd)


def _adaln_body(c_ref, w_ref, b_ref, o_ref):
    c = c_ref[...]
    o_ref[...] = _dot_hi(c * jax.nn.sigmoid(c), w_ref[...]) + b_ref[...]


def _adaln(cond8, w, b):
    d, n = w.shape
    return pl.pallas_call(
        _adaln_body,
        grid=(n // d,),
        in_specs=[_full((SUBLANES, d)), pl.BlockSpec((d, d), lambda j: (0, j)),
                  pl.BlockSpec((1, d), lambda j: (0, j))],
        out_specs=pl.BlockSpec((SUBLANES, d), lambda j: (0, j)),
        out_shape=jax.ShapeDtypeStruct((SUBLANES, n), F32),
        compiler_params=_cparams("arbitrary"),
        name="adaln",
    )(cond8, w, b.reshape(1, n))


def _mod_spec(chunk):
    return pl.BlockSpec((SUBLANES, D_MODEL), lambda t, *_: (0, chunk))


def _mod_row(ref, group):
    return ref[pl.ds(group, 1), :]


def _inproj_body(tiles_per_group, fixed_group, use_rope, want_q, want_hg, *refs):
    it = iter(refs)
    x_ref, sh_ref, sc_ref, g1_ref = next(it), next(it), next(it), next(it)
    wkv_ref, kvg_ref, wuk_ref, wuv_ref, wkr_ref = next(it), next(it), next(it), next(it), next(it)
    if want_q:
        wq_ref, qg_ref, wuqa_ref, wuqb_ref = next(it), next(it), next(it), next(it)
    if want_hg:
        why_ref, wgate_ref = next(it), next(it)
    if use_rope:
        cos_ref, sin_ref = next(it), next(it)
    k_out, v_out = next(it), next(it)
    if want_q:
        q_out = next(it)
    if want_hg:
        phy_out, gate_out = next(it), next(it)

    group = fixed_group if fixed_group is not None else pl.program_id(0) // tiles_per_group
    xf = x_ref[...]
    h = _rms(xf, g1_ref[...]) * (1.0 + _mod_row(sc_ref, group)) + _mod_row(sh_ref, group)
    h = h.astype(BF16)
    if use_rope:
        cos, sin = cos_ref[...], sin_ref[...]

    def rope(a, b):
        return a * cos + b * sin if use_rope else a

    ckv = _rms(_dot(h, wkv_ref[...]), kvg_ref[...]).astype(BF16)
    vt = lax.dot_general(wuv_ref[...], ckv, (((1,), (1,)), ((), ())), preferred_element_type=F32)
    vrow = lax.broadcasted_iota(jnp.int32, vt.shape, 0) & (HEAD_PAD - 1)
    v_out[0] = jnp.where(vrow == V_HEAD, 1.0, vt).astype(BF16)
    k_nope = _dot(ckv, wuk_ref[...])
    kr = _dot(h, wkr_ref[...])
    k_rope = rope(kr[:, :HEAD_PAD], kr[:, HEAD_PAD:])
    for hd in range(N_HEADS):
        sl = slice(hd * HEAD_PAD, (hd + 1) * HEAD_PAD)
        k_out[:, sl] = (k_nope[:, sl] + k_rope).astype(BF16)
    if want_q:
        qn = _rms(_dot(h, wq_ref[...]), qg_ref[...]).astype(BF16)
        qa = _dot(qn, wuqa_ref[...])
        qb = _dot(qn, wuqb_ref[...]) if use_rope else None
        for hd in range(N_HEADS):
            sl = slice(hd * HEAD_PAD, (hd + 1) * HEAD_PAD)
            q_out[:, sl] = (rope(qa[:, sl], None if qb is None else qb[:, sl]) * Q_SCALE).astype(BF16)
    if want_hg:
        n_hy = why_ref.shape[1]
        for c0 in range(0, n_hy, 512):
            phy_out[:, c0:c0 + 512] = _dot(h, why_ref[:, c0:c0 + 512])
        n_g = wgate_ref.shape[1]
        for c0 in range(0, n_g, 512):
            gate_out[:, c0:c0 + 512] = jax.nn.sigmoid(_dot(h, wgate_ref[:, c0:c0 + 512]))


def _inproj(x, mod, lw, *, tiles_per_group, fixed_group, rope_tabs, want_q, want_hg):
    rows = x.shape[0]
    nt = rows // ROW_TILE
    use_rope = rope_tabs is not None
    row_spec = lambda n: pl.BlockSpec((ROW_TILE, n), lambda t: (t, 0))
    ins = [x, mod, mod, lw["norm1_g"], lw["w_kv"], lw["kv_norm_g"], lw["w_uk"], lw["w_uv"], lw["w_kr"]]
    specs = [row_spec(D_MODEL), _mod_spec(0), _mod_spec(1), _full((1, D_MODEL)),
             _full(lw["w_kv"].shape), _full((1, KV_LORA)), _full(lw["w_uk"].shape), _full(lw["w_uv"].shape),
             _full(lw["w_kr"].shape)]
    if want_q:
        ins += [lw["w_q"], lw["q_norm_g"], lw["w_uq_a"], lw["w_uq_b"]]
        specs += [_full(lw["w_q"].shape), _full((1, Q_LORA)), _full(lw["w_uq_a"].shape), _full(lw["w_uq_b"].shape)]
    if want_hg:
        ins += [lw["w_hy"], lw["w_gate"]]
        specs += [_full(lw["w_hy"].shape), _full(lw["w_gate"].shape)]
    if use_rope:
        seq_tiles = rope_tabs[0].shape[0] // ROW_TILE
        ins += list(rope_tabs)
        specs += [pl.BlockSpec((ROW_TILE, HEAD_PAD), lambda t: (t % seq_tiles, 0))] * 2
    hp = N_HEADS * HEAD_PAD
    out_shape = [jax.ShapeDtypeStruct((rows, hp), BF16), jax.ShapeDtypeStruct((nt, hp, ROW_TILE), BF16)]
    out_specs = [row_spec(hp), pl.BlockSpec((1, hp, ROW_TILE), lambda t: (t, 0, 0))]
    if want_q:
        out_shape.append(jax.ShapeDtypeStruct((rows, hp), BF16))
        out_specs.append(row_spec(hp))
    if want_hg:
        out_shape += [jax.ShapeDtypeStruct((rows, 3 * HY_WIDTH), F32), jax.ShapeDtypeStruct((rows, 2 * D_MODEL), F32)]
        out_specs += [row_spec(3 * HY_WIDTH), row_spec(2 * D_MODEL)]
    outs = pl.pallas_call(
        functools.partial(_inproj_body, tiles_per_group, fixed_group, use_rope, want_q, want_hg),
        grid=(nt,), in_specs=specs, out_specs=out_specs, out_shape=out_shape,
        compiler_params=_cparams("arbitrary"), name="inproj",
    )(*ins)
    res = {"k": outs[0], "v": outs[1]}
    i = 2
    if want_q:
        res["q"] = outs[i]
        i += 1
    if want_hg:
        res["p_hy"], res["gates"] = outs[i], outs[i + 1]
    return res


def _attn_body(n_lat_blocks, *refs):
    if n_lat_blocks:
        q_ref, kc_ref, vc_ref, kl_ref, vl_ref, o_ref = refs
    else:
        q_ref, kc_ref, vc_ref, o_ref = refs
    tq = q_ref.shape[0]
    q = q_ref[...]
    heads = tuple(slice(hd * HEAD_PAD, (hd + 1) * HEAD_PAD) for hd in range(2))
    sub = ATTN_KV // VT_BLOCK

    def scores(kblk):
        return tuple(lax.dot_general(kblk[:, sl], q[:, sl], (((1,), (1,)), ((), ())), preferred_element_type=F32)
                     for sl in heads)

    def update(st, vt_blocks, carry):
        m_new = [jnp.maximum(carry[hd][0], jnp.max(st[hd], axis=0, keepdims=True)) for hd in range(2)]
        p = [jnp.exp2(st[hd] - m_new[hd]).astype(BF16) for hd in range(2)]
        out = []
        for hd in range(2):
            m, acc = carry[hd]
            pv = None
            for j, vt in enumerate(vt_blocks):
                r = _dot(vt[heads[hd], :], p[hd][j * VT_BLOCK:(j + 1) * VT_BLOCK])
                pv = r if pv is None else pv + r
            out.append((m_new[hd], jnp.exp2(m - m_new[hd]) * acc + pv))
        return tuple(out)

    init = tuple((jnp.full((1, tq), -jnp.inf, F32), jnp.zeros((HEAD_PAD, tq), F32)) for _ in range(2))
    carry = update(scores(kc_ref[0]), [vc_ref[0, 0]], init)
    if n_lat_blocks:
        def k_block(i):
            return kl_ref[0, pl.ds(pl.multiple_of(i * ATTN_KV, ATTN_KV), ATTN_KV), :]

        def v_blocks(i):
            return [vl_ref[0, i * sub + j] for j in range(sub)]

        def body(i, c):
            st, carry = c
            st_next = scores(k_block(i + 1))
            return st_next, update(st, v_blocks(i), carry)
        st, carry = lax.fori_loop(0, n_lat_blocks - 1, body, (scores(k_block(0)), carry), unroll=True)
        carry = update(st, v_blocks(n_lat_blocks - 1), carry)
    o_t = jnp.concatenate([acc[:V_HEAD] / acc[V_HEAD:V_HEAD + 1] for _, acc in carry], axis=0)
    o_ref[...] = o_t.T.astype(BF16)


def _attention(q, k_ctx, vt_ctx, k_lat, vt_lat, batch):
    rows = q.shape[0]
    tq = min(ATTN_Q_TILE, rows // batch)
    qt_per_b = rows // batch // tq
    ctx_len = k_ctx.shape[1]
    assert ctx_len == VT_BLOCK
    pair_w = 2 * HEAD_PAD
    ins = [q, k_ctx, vt_ctx]
    specs = [pl.BlockSpec((tq, pair_w), lambda b, hp, t: (b * qt_per_b + t, hp)),
             pl.BlockSpec((1, ctx_len, pair_w), lambda b, hp, t: (b, 0, hp)),
             pl.BlockSpec((1, 1, pair_w, VT_BLOCK), lambda b, hp, t: (b, 0, hp, 0))]
    n_lat_blocks = 0
    if k_lat is not None:
        lat_len = k_lat.shape[1]
        n_lat_blocks = lat_len // ATTN_KV
        ins += [k_lat, vt_lat]
        specs += [pl.BlockSpec((1, lat_len, pair_w), lambda b, hp, t: (b, 0, hp)),
                  pl.BlockSpec((1, lat_len // VT_BLOCK, pair_w, VT_BLOCK), lambda b, hp, t: (b, 0, hp, 0))]
    return pl.pallas_call(
        functools.partial(_attn_body, n_lat_blocks),
        grid=(batch, N_HEADS // 2, qt_per_b), in_specs=specs,
        out_specs=pl.BlockSpec((tq, 2 * V_HEAD), lambda b, hp, t: (b * qt_per_b + t, hp)),
        out_shape=jax.ShapeDtypeStruct((rows, N_HEADS * V_HEAD), BF16),
        compiler_params=_cparams("arbitrary", "arbitrary", "arbitrary"), name="attention",
    )(*ins)


def _hypre_body(seq_tiles, n_chunks, p_ref, prev_ref, next_ref, w_ref, b_ref, z_ref, x0_ref, *zt_ref):
    j = pl.program_id(0) % seq_tiles
    p = p_ref[...]
    tm = p.shape[0]
    row = lax.broadcasted_iota(jnp.int32, (tm, 1), 0)
    prev_row = jnp.where(j != 0, prev_ref[SUBLANES - 1:SUBLANES, :], 0.0)
    next_row = jnp.where(j != seq_tiles - 1, next_ref[0:1, :], 0.0)
    up = jnp.where(row == 0, prev_row, pltpu.roll(p, 1, 0))
    dn = jnp.where(row == tm - 1, next_row, pltpu.roll(p, tm - 1, 0))
    u = up * w_ref[0:1, :] + p * w_ref[1:2, :] + dn * w_ref[2:3, :] + b_ref[...]
    c = HY_WIDTH
    z = u[:, :c] * u[:, c:2 * c]
    z_ref[...] = z
    x0_ref[...] = u[:, 2 * c:]
    if zt_ref:
        q = tm // n_chunks
        for a in range(n_chunks):
            zt_ref[0][0, :, a * c:(a + 1) * c] = z[a * q:(a + 1) * q, :].astype(BF16)


def _hyena_pre(p_hy, short_w, short_b, batch, dft_q):
    rows = p_hy.shape[0]
    n = rows // batch
    seq_tiles = n // ROW_TILE
    nt = rows // ROW_TILE
    c3 = 3 * HY_WIDTH
    halo = ROW_TILE // SUBLANES
    last8 = rows // SUBLANES - 1
    specs = [pl.BlockSpec((ROW_TILE, c3), lambda t: (t, 0)),
             pl.BlockSpec((SUBLANES, c3), lambda t: (jnp.maximum(t * halo - 1, 0), 0)),
             pl.BlockSpec((SUBLANES, c3), lambda t: (jnp.minimum((t + 1) * halo, last8), 0)),
             _full((3, c3)), _full((1, c3))]
    out_shape = [jax.ShapeDtypeStruct((rows, HY_WIDTH), F32)] * 2
    out_specs = [pl.BlockSpec((ROW_TILE, HY_WIDTH), lambda t: (t, 0))] * 2
    n_chunks = 1
    if dft_q is not None:
        n_chunks = ROW_TILE // dft_q
        half_p = n // dft_q
        out_shape.append(jax.ShapeDtypeStruct((batch // 2, dft_q, 2 * half_p * HY_WIDTH), BF16))

        def zt_map(t):
            b, jt = t // seq_tiles, t % seq_tiles
            return (b // 2, 0, (b % 2) * seq_tiles + jt)
        out_specs.append(pl.BlockSpec((1, dft_q, n_chunks * HY_WIDTH), zt_map))
    return pl.pallas_call(
        functools.partial(_hypre_body, seq_tiles, n_chunks),
        grid=(nt,), in_specs=specs, out_specs=out_specs, out_shape=out_shape,
        compiler_params=_cparams("arbitrary"), name="hyena_pre",
    )(p_hy, p_hy, p_hy, short_w, short_b.reshape(1, c3))


def _filter_body(emb_ref, w1_ref, b1_ref, w2_ref, b2_ref, w3_ref, freq_ref, decay_ref, h0_ref, h1_ref, s_ref):
    a = pl.program_id(0)
    emb = emb_ref[...]
    freq = freq_ref[...]
    h = jnp.sin(freq * (_dot_hi(emb, w1_ref[...]) + b1_ref[...]))
    h = jnp.sin(freq * (_dot_hi(h, w2_ref[...]) + b2_ref[...]))
    h = _dot_hi(h, w3_ref[...]) * jnp.exp(-emb[:, 0:1] * jnp.abs(decay_ref[...]))
    c = HY_WIDTH
    row = lax.broadcasted_iota(jnp.int32, (emb.shape[0], 1), 0)
    h0 = h[:, :c]
    h1 = jnp.where(jnp.logical_and(a == 0, row == 0), 0.0, h[:, c:])
    h0_ref[0] = h0.astype(BF16)
    h1_ref[0] = h1.astype(BF16)
    part = jnp.sum(jnp.abs(h0) + jnp.abs(h1), axis=0, keepdims=True)

    @pl.when(a == 0)
    def _():
        s_ref[...] = jnp.zeros_like(s_ref)
    s_ref[...] += part


def _hyena_filter(n, q, lw):
    f32 = F32
    bands = (HY_EMB - 1) // 2
    t = jnp.linspace(0.0, 1.0, n, dtype=f32)[:, None]
    phase = (2.0 * math.pi / n) * jnp.arange(n, dtype=f32)[:, None] * jnp.linspace(1e-4, bands - 1, bands, dtype=f32)
    emb = jnp.concatenate([t, jnp.cos(phase), -jnp.sin(phase), jnp.zeros((n, 32 - HY_EMB), f32)], axis=-1)
    w1 = jnp.concatenate([lw["hy_w1"], jnp.zeros((32 - HY_EMB, HY_HIDDEN), f32)], axis=0)
    c = HY_WIDTH
    slots = n // q
    row = lambda v: v.reshape(1, -1)
    return pl.pallas_call(
        _filter_body,
        grid=(slots,),
        in_specs=[pl.BlockSpec((q, 32), lambda a: (a, 0)), _full((32, HY_HIDDEN)), _full((1, HY_HIDDEN)),
                  _full((HY_HIDDEN, HY_HIDDEN)), _full((1, HY_HIDDEN)), _full((HY_HIDDEN, 2 * c)),
                  _full((1, HY_HIDDEN)), _full((1, 2 * c))],
        out_specs=[pl.BlockSpec((1, q, c), lambda a: (0, 0, a)), pl.BlockSpec((1, q, c), lambda a: (0, 0, a)),
                   _full((1, c))],
        out_shape=[jax.ShapeDtypeStruct((1, q, slots * c), BF16)] * 2 + [jax.ShapeDtypeStruct((1, c), f32)],
        compiler_params=_cparams("arbitrary"), name="hyena_filter",
    )(emb, w1, row(lw["hy_b1"]), lw["hy_w2"], row(lw["hy_b2"]), lw["hy_w3"], row(lw["hy_freq"]),
      row(lw["hy_decay"]))


def _colslot_body(nparts, bq, c, *refs):
    x_refs, t_refs, o_ref = refs[:nparts], refs[nparts:2 * nparts], refs[-1]
    for j in range(bq):
        acc = None
        for x_ref, t_ref in zip(x_refs, t_refs):
            r = _dot_dft(t_ref[j], x_ref[0, j])
            acc = r if acc is None else acc + r
        o_ref[0, :, j * c:(j + 1) * c] = acc.astype(o_ref.dtype)


def _colslot_matmul(xs, tabs, out_dtype, c=HY_WIDTH):
    tabs = [jnp.asarray(t, BF16) for t in tabs]
    g, q = xs[0].shape[0], xs[0].shape[1]
    m = tabs[0].shape[1]
    bq = COLSLOT_BQ
    xs4 = [x.reshape(g, q, -1, c) for x in xs]
    specs = [pl.BlockSpec((1, bq, x.shape[2], c), lambda gi, qi: (gi, qi, 0, 0)) for x in xs4]
    specs += [pl.BlockSpec((bq, m, t.shape[2]), lambda gi, qi: (qi, 0, 0)) for t in tabs]
    return pl.pallas_call(
        functools.partial(_colslot_body, len(xs), bq, c),
        grid=(g, q // bq), in_specs=specs,
        out_specs=pl.BlockSpec((1, m, bq * c), lambda gi, qi: (gi, 0, qi)),
        out_shape=jax.ShapeDtypeStruct((g, m, q * c), out_dtype),
        compiler_params=_cparams("arbitrary", "arbitrary"), name="dft_stride_stage",
    )(*xs4, *tabs)


def _spec_mid_body(g_ref, m3_ref, m3i_ref, k_ref, hr_ref, hi_ref):
    q = hr_ref.shape[1]
    x = _dot_dft(m3_ref[...], g_ref[0, :, 0].reshape(2 * q, -1))
    kh = k_ref[0]
    xr, xi, kr, ki = x[:q], x[q:], kh[:q], kh[q:]
    y = jnp.concatenate([xr * kr - xi * ki, xr * ki + xi * kr], axis=0)
    h = _dot_dft(m3i_ref[...], y)
    hr_ref[0] = h[:q].astype(BF16)
    hi_ref[0] = h[q:].astype(BF16)


def _spec_mid(gf, m3, m3i, khat, p, q, c=HY_WIDTH):
    bp = gf.shape[0]
    g5 = gf.reshape(bp, 2, p, q, c)
    return pl.pallas_call(
        _spec_mid_body,
        grid=(bp, p),
        in_specs=[pl.BlockSpec((1, 2, 1, q, c), lambda b, ci: (b, 0, ci, 0, 0)), _full((2 * q, 2 * q)),
                  _full((2 * q, 2 * q)), pl.BlockSpec((1, 2 * q, c), lambda b, ci: (ci, 0, 0))],
        out_specs=[pl.BlockSpec((1, q, c), lambda b, ci: (b, 0, ci))] * 2,
        out_shape=[jax.ShapeDtypeStruct((bp, q, p * c), BF16)] * 2,
        compiler_params=_cparams("arbitrary", "arbitrary"), name="dft_mid",
    )(g5, m3, m3i, khat)


def _filter_spec_body(g_ref, m3k_ref, s_ref, k_ref):
    q4 = m3k_ref.shape[1]
    k_ref[0] = _dot_dft(m3k_ref[...], g_ref[0, :, 0].reshape(q4, -1)) / s_ref[...]


def _filter_spec(gk, m3k, s, p, q, c=HY_WIDTH):
    g5 = gk.reshape(1, 4, p, q, c)
    return pl.pallas_call(
        _filter_spec_body,
        grid=(p,),
        in_specs=[pl.BlockSpec((1, 4, 1, q, c), lambda ci: (0, 0, ci, 0, 0)), _full((2 * q, 4 * q)), _full((1, c))],
        out_specs=pl.BlockSpec((1, 2 * q, c), lambda ci: (ci, 0, 0)),
        out_shape=jax.ShapeDtypeStruct((p, 2 * q, c), F32),
        compiler_params=_cparams("arbitrary"), name="filter_spectrum",
    )(g5, m3k, s)


def _direct_conv_body(z_ref, h0_ref, h1_ref, s_ref, tf_ref, tk_ref, ti_ref, o_ref):
    n2 = tf_ref.shape[0] // 2
    kh = _dot_dft(tk_ref[...], jnp.concatenate([h0_ref[0], h1_ref[0]], axis=0)) / s_ref[...]
    x = _dot_dft(tf_ref[...], z_ref[0])
    xr, xi, kr, ki = x[:n2], x[n2:], kh[:n2], kh[n2:]
    y = jnp.concatenate([xr * kr - xi * ki, xr * ki + xi * kr], axis=0)
    o_ref[0] = _dot_dft(ti_ref[...], y)


def _direct_conv(z, h0, h1, s, n, c=HY_WIDTH):
    bp = z.shape[0] // (2 * n)
    tf, tk, ti = (jnp.asarray(t, BF16) for t in _direct_tables(n))
    out = pl.pallas_call(
        _direct_conv_body,
        grid=(bp,),
        in_specs=[pl.BlockSpec((1, 2 * n, c), lambda b: (b, 0, 0)), _full((1, n, c)), _full((1, n, c)), _full((1, c)),
                  _full(tf.shape), _full(tk.shape), _full(ti.shape)],
        out_specs=pl.BlockSpec((1, 2 * n, c), lambda b: (b, 0, 0)),
        out_shape=jax.ShapeDtypeStruct((bp, 2 * n, c), F32),
        compiler_params=_cparams("arbitrary"), name="direct_conv",
    )(z.reshape(bp, 2 * n, c), h0, h1, s, tf, tk, ti)
    return out.reshape(bp * 2 * n, c)


def _cis(m, n_total):
    ang = (2.0 * np.pi / n_total) * (m % n_total).astype(np.float64)
    return np.cos(ang), -np.sin(ang)


@functools.lru_cache(maxsize=None)
def _direct_tables(n):
    nn = 2 * n
    f = np.arange(nn)[:, None]
    t = np.arange(n)[None, :]
    cr, ci = _cis(f * t, nn)
    tf = np.block([[cr, -ci], [ci, cr]])
    tk = np.block([[cr, cr], [ci, -ci]])
    ti = np.block([[cr.T, ci.T], [-ci.T, cr.T]]) / nn
    return tuple(a.astype(np.float32) for a in (tf, tk, ti))


@functools.lru_cache(maxsize=None)
def _twostage_tables(p, q):
    nn = p * q
    hp = p // 2
    s = np.arange(q)[:, None, None]
    c = np.arange(p)[None, :, None]
    a = np.arange(hp)[None, None, :]
    tr, ti = _cis(c * (q * a + s), nn)
    z = np.zeros_like(tr)
    t_data = np.concatenate([np.concatenate([tr, -ti], 2), np.concatenate([ti, tr], 2)], 1)
    t_k0 = np.concatenate([tr, ti, z, z], 1)
    t_k1 = np.concatenate([z, z, tr, -ti], 1)
    trt, tit = np.swapaxes(tr, 1, 2) / nn, np.swapaxes(ti, 1, 2) / nn
    t_inv_r = np.concatenate([trt, -tit], 1)
    t_inv_i = np.concatenate([tit, trt], 1)
    d = np.arange(q)[:, None]
    b = np.arange(q)[None, :]
    fr, fi = _cis(d * b, q)
    m3 = np.block([[fr, -fi], [fi, fr]])
    m3i = np.block([[fr, fi], [-fi, fr]])
    m3k = np.block([[fr, -fi, fr, fi], [fi, fr, -fi, fr]])
    return tuple(x.astype(np.float32) for x in (t_data, t_k0, t_k1, t_inv_r, t_inv_i, m3, m3i, m3k))


def _long_conv_spectrum(n, lw):
    p, q = DFT_P, DFT_Q
    h0, h1, s = _hyena_filter(n, q, lw)
    _, t_k0, t_k1, _, _, _, _, m3k = _twostage_tables(p, q)
    gk = _colslot_matmul([h0, h1], [t_k0, t_k1], BF16)
    return _filter_spec(gk, jnp.asarray(m3k, BF16), s, p, q)


def _long_conv(zt, khat):
    p, q = DFT_P, DFT_Q
    t_data, _, _, t_inv_r, t_inv_i, m3, m3i, _ = _twostage_tables(p, q)
    gf = _colslot_matmul([zt], [t_data], BF16)
    hr, hi = _spec_mid(gf, jnp.asarray(m3, BF16), jnp.asarray(m3i, BF16), khat, p, q)
    y = _colslot_matmul([hr, hi], [t_inv_r, t_inv_i], F32)
    return y.reshape(-1, HY_WIDTH)


def _merge_body(tiles_per_group, fixed_group, x_ref, g1_ref, oa_ref, conv_ref, z_ref, x0_ref, gate_ref, bias_ref,
                wba_ref, wbh_ref, wout_ref, o_ref):
    group = fixed_group if fixed_group is not None else pl.program_id(0) // tiles_per_group
    o_hy = ((conv_ref[...] + z_ref[...] * bias_ref[...]) * x0_ref[...]).astype(BF16)
    d = D_MODEL
    merged = gate_ref[:, :d] * _dot(oa_ref[...], wba_ref[...]) + gate_ref[:, d:] * _dot(o_hy, wbh_ref[...])
    mix = _dot(merged.astype(BF16), wout_ref[...])
    o_ref[...] = x_ref[...] + _mod_row(g1_ref, group) * mix


def _merge(x, mod, o_attn, conv, z, x0, gates, lw, *, tiles_per_group, fixed_group):
    rows = x.shape[0]
    row_spec = lambda n: pl.BlockSpec((ROW_TILE, n), lambda t: (t, 0))
    c = HY_WIDTH
    return pl.pallas_call(
        functools.partial(_merge_body, tiles_per_group, fixed_group),
        grid=(rows // ROW_TILE,),
        in_specs=[row_spec(D_MODEL), _mod_spec(2), row_spec(N_HEADS * V_HEAD), row_spec(c), row_spec(c), row_spec(c),
                  row_spec(2 * D_MODEL), _full((1, c)), _full(lw["w_br_attn"].shape), _full(lw["w_br_hy"].shape),
                  _full(lw["w_out"].shape)],
        out_specs=row_spec(D_MODEL),
        out_shape=jax.ShapeDtypeStruct((rows, D_MODEL), F32),
        compiler_params=_cparams("arbitrary"), name="merge_out",
    )(x, mod, o_attn, conv, z, x0, gates, lw["hy_bias"], lw["w_br_attn"], lw["w_br_hy"], lw["w_out"])


def _swiglu_chunk(h, wg_ref, wu_ref, wd_ref):
    gate = _dot(h, wg_ref[0])
    up = _dot(h, wu_ref[0])
    return _dot((gate * jax.nn.sigmoid(gate) * up).astype(BF16), wd_ref[0])


def _ffn_body(tiles_per_group, fixed_group, x_ref, sh_ref, sc_ref, g2_ref, n2_ref, wg_ref, wu_ref, wd_ref, o_ref,
              h_scr, acc_scr):
    group = fixed_group if fixed_group is not None else pl.program_id(0) // tiles_per_group
    ci = pl.program_id(1)

    @pl.when(ci == 0)
    def _():
        h = _rms(x_ref[...], n2_ref[...]) * (1.0 + _mod_row(sc_ref, group)) + _mod_row(sh_ref, group)
        h_scr[...] = h.astype(BF16)
        acc_scr[...] = jnp.zeros_like(acc_scr)

    acc_scr[...] += _swiglu_chunk(h_scr[...], wg_ref, wu_ref, wd_ref)

    @pl.when(ci == pl.num_programs(1) - 1)
    def _():
        o_ref[...] = x_ref[...] + _mod_row(g2_ref, group) * acc_scr[...]


def _ffn(x, mod, norm_g, w_gate, w_up, w_down, *, tiles_per_group, fixed_group):
    rows = x.shape[0]
    tm = min(FFN_ROW_TILE, rows)
    n_chunks = D_FF // FFN_CHUNK
    row_spec = pl.BlockSpec((tm, D_MODEL), lambda t, ci: (t, 0))
    return pl.pallas_call(
        functools.partial(_ffn_body, tiles_per_group * ROW_TILE // tm if tiles_per_group else None, fixed_group),
        grid=(rows // tm, n_chunks),
        in_specs=[row_spec, _mod_spec(3), _mod_spec(4), _mod_spec(5), _full((1, D_MODEL)),
                  pl.BlockSpec((1, D_MODEL, FFN_CHUNK), lambda t, ci: (0, 0, ci)),
                  pl.BlockSpec((1, D_MODEL, FFN_CHUNK), lambda t, ci: (0, 0, ci)),
                  pl.BlockSpec((1, FFN_CHUNK, D_MODEL), lambda t, ci: (0, ci, 0))],
        out_specs=row_spec,
        out_shape=jax.ShapeDtypeStruct((rows, D_MODEL), F32),
        scratch_shapes=[pltpu.VMEM((tm, D_MODEL), BF16), pltpu.VMEM((tm, D_MODEL), F32)],
        compiler_params=_cparams("arbitrary", "arbitrary"), name="ffn",
    )(x, mod, mod, mod, norm_g, w_gate, w_up, w_down)


def _route_body(tiles_per_group, x_ref, sh_ref, sc_ref, n2_ref, router_ref, h_ref, route_ref):
    group = pl.program_id(0) // tiles_per_group
    h = _rms(x_ref[...], n2_ref[...]) * (1.0 + _mod_row(sc_ref, group)) + _mod_row(sh_ref, group)
    h_ref[...] = h
    logits = _dot_hi(h, router_ref[...])
    lane = lax.broadcasted_iota(jnp.int32, logits.shape, 1)
    logits = jnp.where(lane < N_EXPERTS, logits, -jnp.inf)
    m1 = jnp.max(logits, axis=-1, keepdims=True)
    i1 = jnp.min(jnp.where(logits == m1, lane, LANES), axis=-1, keepdims=True)
    rest = jnp.where(lane == i1, -jnp.inf, logits)
    m2 = jnp.max(rest, axis=-1, keepdims=True)
    i2 = jnp.min(jnp.where(rest == m2, lane, LANES), axis=-1, keepdims=True)
    e2 = jnp.exp(m2 - m1)
    w1 = 1.0 / (1.0 + e2)
    route_ref[...] = (jnp.where(lane == 0, i1.astype(F32), 0.0) + jnp.where(lane == 1, i2.astype(F32), 0.0)
                      + jnp.where(lane == 2, w1, 0.0) + jnp.where(lane == 3, e2 * w1, 0.0))


def _route(x, mod, norm_g, router, tiles_per_group):
    rows = x.shape[0]
    row_spec = lambda n: pl.BlockSpec((ROW_TILE, n), lambda t: (t, 0))
    return pl.pallas_call(
        functools.partial(_route_body, tiles_per_group),
        grid=(rows // ROW_TILE,),
        in_specs=[row_spec(D_MODEL), _mod_spec(3), _mod_spec(4), _full((1, D_MODEL)), _full(router.shape)],
        out_specs=[row_spec(D_MODEL), row_spec(LANES)],
        out_shape=[jax.ShapeDtypeStruct((rows, D_MODEL), F32), jax.ShapeDtypeStruct((rows, LANES), F32)],
        compiler_params=_cparams("arbitrary"), name="moe_route",
    )(x, mod, mod, norm_g, router)


def _gather_rows(idx_ref, n_rows, src_hbm, dst, sem):
    def issue(r, carry):
        pltpu.make_async_copy(src_hbm.at[pl.ds(idx_ref[0, 0, r], 1), :], dst.at[pl.ds(r, 1), :], sem).start()
        return carry
    lax.fori_loop(0, n_rows, issue, 0, unroll=8)


def _wait_rows(n_rows, src_hbm, dst, sem):
    pltpu.make_async_copy(src_hbm.at[pl.ds(0, n_rows), :], dst, sem).wait()


def _moe_dispatch_body(idx_ref, h_hbm, xs_in_hbm, xs_hbm, sem):
    del xs_in_hbm
    t = pl.program_id(0)
    tm = idx_ref.shape[2] // 2
    slot = t % 2

    def wait(s):
        pltpu.make_async_copy(h_hbm.at[pl.ds(0, 2 * tm), :], xs_hbm.at[pl.ds(0, 2 * tm), :], sem.at[s]).wait()

    @pl.when(t > 0)
    def _():
        wait(1 - slot)

    def issue(r, carry):
        row = h_hbm.at[pl.ds(t * tm + r, 1), :]
        pltpu.make_async_copy(row, xs_hbm.at[pl.ds(idx_ref[0, 0, r], 1), :], sem.at[slot]).start()
        pltpu.make_async_copy(row, xs_hbm.at[pl.ds(idx_ref[0, 0, tm + r], 1), :], sem.at[slot]).start()
        return carry
    lax.fori_loop(0, tm, issue, 0, unroll=4)

    @pl.when(t == pl.num_programs(0) - 1)
    def _():
        wait(slot)


def _tile_index_blocks(dest, tm):
    n_tiles = dest.shape[0] // tm
    return dest.reshape(n_tiles, tm, 2).transpose(0, 2, 1).reshape(n_tiles, 1, 2 * tm)


def _moe_dispatch(h, dest, n_rows):
    tm = ROW_TILE
    idx = _tile_index_blocks(dest, tm)
    n_tiles = idx.shape[0]
    return pl.pallas_call(
        _moe_dispatch_body,
        grid=(n_tiles,),
        in_specs=[pl.BlockSpec((1, 1, 2 * tm), lambda t: (t, 0, 0), memory_space=pltpu.SMEM),
                  pl.BlockSpec(memory_space=pl.ANY), pl.BlockSpec(memory_space=pl.ANY)],
        out_specs=pl.BlockSpec(memory_space=pl.ANY),
        out_shape=jax.ShapeDtypeStruct((n_rows, D_MODEL), F32),
        scratch_shapes=[pltpu.SemaphoreType.DMA((2,))],
        input_output_aliases={2: 0},
        compiler_params=_cparams("arbitrary"), name="moe_dispatch",
    )(idx, h, jnp.zeros((n_rows, D_MODEL), F32))


def _moe_group_body(te_ref, nu_ref, x_ref, wg_ref, wu_ref, wd_ref, y_ref, xb_scr, acc_scr):
    t, ci = pl.program_id(0), pl.program_id(1)

    @pl.when(ci == 0)
    def _():
        xb_scr[...] = x_ref[...].astype(BF16)
        acc_scr[...] = jnp.zeros_like(acc_scr)

    @pl.when(t < nu_ref[0])
    def _():
        acc_scr[...] += _swiglu_chunk(xb_scr[...], wg_ref, wu_ref, wd_ref)

    @pl.when(ci == pl.num_programs(1) - 1)
    def _():
        y_ref[...] = acc_scr[...]


def _moe_group(xs, tile_expert, n_used, w_gate, w_up, w_down):
    tm = MOE_ROW_TILE
    n_rows = xs.shape[0]
    n_chunks = D_FF // FFN_CHUNK

    def chunk(t, ci, nu):
        return jnp.where(t < nu[0], ci, n_chunks - 1)
    grid_spec = pltpu.PrefetchScalarGridSpec(
        num_scalar_prefetch=2,
        grid=(n_rows // tm, n_chunks),
        in_specs=[pl.BlockSpec((tm, D_MODEL), lambda t, ci, te, nu: (jnp.minimum(t, nu[0] - 1), 0)),
                  pl.BlockSpec((1, D_MODEL, FFN_CHUNK), lambda t, ci, te, nu: (te[t], 0, chunk(t, ci, nu))),
                  pl.BlockSpec((1, D_MODEL, FFN_CHUNK), lambda t, ci, te, nu: (te[t], 0, chunk(t, ci, nu))),
                  pl.BlockSpec((1, FFN_CHUNK, D_MODEL), lambda t, ci, te, nu: (te[t], chunk(t, ci, nu), 0))],
        out_specs=pl.BlockSpec((tm, D_MODEL), lambda t, ci, te, nu: (t, 0)),
        scratch_shapes=[pltpu.VMEM((tm, D_MODEL), BF16), pltpu.VMEM((tm, D_MODEL), F32)],
    )
    return pl.pallas_call(
        _moe_group_body, grid_spec=grid_spec,
        out_shape=jax.ShapeDtypeStruct((n_rows, D_MODEL), F32),
        compiler_params=_cparams("arbitrary", "arbitrary"), name="moe_group",
    )(tile_expert, n_used, xs, w_gate, w_up, w_down)


def _moe_combine_body(tiles_per_group, idx_ref, idx_next_ref, x_ref, g2_ref, fg_ref, route_ref, y_hbm, o_ref,
                      yg_scr, sem):
    t = pl.program_id(0)
    n_tiles = pl.num_programs(0)
    group = t // tiles_per_group
    tm = x_ref.shape[0]
    slot = t % 2

    @pl.when(t == 0)
    def _():
        _gather_rows(idx_ref, 2 * tm, y_hbm, yg_scr.at[0], sem.at[0])

    _wait_rows(2 * tm, y_hbm, yg_scr.at[slot], sem.at[slot])

    @pl.when(t + 1 < n_tiles)
    def _():
        _gather_rows(idx_next_ref, 2 * tm, y_hbm, yg_scr.at[1 - slot], sem.at[1 - slot])

    route = route_ref[...]
    y = route[:, 2:3] * yg_scr[slot, :tm] + route[:, 3:4] * yg_scr[slot, tm:]
    o_ref[...] = _rms(x_ref[...] + _mod_row(g2_ref, group) * y, fg_ref[...])


def _moe_combine(x, mod, final_g, route, y_sorted, dest, tiles_per_group):
    rows = x.shape[0]
    tm = ROW_TILE
    idx = _tile_index_blocks(dest, tm)
    n_tiles = idx.shape[0]
    return pl.pallas_call(
        functools.partial(_moe_combine_body, tiles_per_group),
        grid=(n_tiles,),
        in_specs=[pl.BlockSpec((1, 1, 2 * tm), lambda t: (t, 0, 0), memory_space=pltpu.SMEM),
                  pl.BlockSpec((1, 1, 2 * tm), lambda t: (jnp.minimum(t + 1, n_tiles - 1), 0, 0),
                               memory_space=pltpu.SMEM),
                  pl.BlockSpec((tm, D_MODEL), lambda t: (t, 0)), _mod_spec(5), _full((1, D_MODEL)),
                  pl.BlockSpec((tm, LANES), lambda t: (t, 0)), pl.BlockSpec(memory_space=pl.ANY)],
        out_specs=pl.BlockSpec((tm, D_MODEL), lambda t: (t, 0)),
        out_shape=jax.ShapeDtypeStruct((rows, D_MODEL), F32),
        scratch_shapes=[pltpu.VMEM((2, 2 * tm, D_MODEL), F32), pltpu.SemaphoreType.DMA((2,))],
        compiler_params=_cparams("arbitrary"), name="moe_combine",
    )(idx, idx, x, mod, final_g, route, y_sorted)


def _moe_plan(route, tm):
    t = route.shape[0]
    experts = jnp.concatenate([route[:, 0], route[:, 1]]).astype(jnp.int32)
    onehot = (experts[:, None] == jnp.arange(N_EXPERTS, dtype=jnp.int32)[None, :]).astype(jnp.int32)
    csum = jnp.cumsum(onehot, axis=0)
    rank = jnp.sum(csum * onehot, axis=1) - 1
    padded = (csum[-1] + tm - 1) // tm * tm
    ends = jnp.cumsum(padded)
    dest = jnp.sum((ends - padded)[None, :] * onehot, axis=1) + rank
    n_rows = 2 * t + N_EXPERTS * tm
    tile_start = jnp.arange(n_rows // tm, dtype=jnp.int32) * tm
    tile_expert = jnp.minimum(jnp.sum((tile_start[:, None] >= ends[None, :]).astype(jnp.int32), axis=1), N_EXPERTS - 1)
    n_used = (ends[-1:] // tm).astype(jnp.int32)
    return dest.reshape(2, t).T, tile_expert, n_used, n_rows


def _moe(x, mod, norm_g, router, w_gate, w_up, w_down, final_g, tiles_per_group):
    h, route = _route(x, mod, norm_g, router, tiles_per_group)
    dest, tile_expert, n_used, n_rows = _moe_plan(route, MOE_ROW_TILE)
    xs = _moe_dispatch(h, dest, n_rows)
    y_sorted = _moe_group(xs, tile_expert, n_used, w_gate, w_up, w_down)
    return _moe_combine(x, mod, final_g, route, y_sorted, dest, tiles_per_group)


def _pad_heads(w, width):
    k = w.shape[0]
    w = w.reshape(k, N_HEADS, width)
    return jnp.pad(w, ((0, 0), (0, 0), (0, HEAD_PAD - width))).reshape(k, N_HEADS * HEAD_PAD)


def _rot_cols(w):
    half = QK_ROPE // 2
    return jnp.concatenate([-w[..., half:], w[..., :half]], axis=-1)


def _layer_weights(p, layer):
    w_in = p["w_in"][layer]
    w_uq = p["w_uq"][layer].reshape(Q_LORA, N_HEADS, QK_NOPE + QK_ROPE)
    zeros_nope = jnp.zeros((Q_LORA, N_HEADS, QK_NOPE), F32)
    w_uq_b = jnp.concatenate([zeros_nope, _rot_cols(w_uq[..., QK_NOPE:])], axis=-1)
    w_kr = w_in[:, KV_START + KV_LORA:HY_START]
    zk = jnp.zeros((D_MODEL, QK_NOPE), F32)
    zp = jnp.zeros((D_MODEL, HEAD_PAD - QK_NOPE - QK_ROPE), F32)
    w_kr2 = jnp.concatenate([zk, w_kr, zp, zk, _rot_cols(w_kr), zp], axis=-1)
    row = lambda v: v.reshape(1, -1)
    bf = lambda v: v.astype(BF16)
    return {
        "norm1_g": row(p["norm1_g"][layer]), "norm2_g": row(p["norm2_g"][layer]),
        "w_q": bf(w_in[:, :Q_LORA]), "q_norm_g": row(p["q_norm_g"][layer]),
        "w_uq_a": bf(_pad_heads(w_uq.reshape(Q_LORA, -1), QK_NOPE + QK_ROPE)),
        "w_uq_b": bf(_pad_heads(w_uq_b.reshape(Q_LORA, -1), QK_NOPE + QK_ROPE)),
        "w_kv": bf(w_in[:, KV_START:KV_START + KV_LORA]), "kv_norm_g": row(p["kv_norm_g"][layer]),
        "w_uk": bf(_pad_heads(p["w_uk"][layer], QK_NOPE)), "w_uv": bf(_pad_heads(p["w_uv"][layer], V_HEAD).T),
        "w_kr": bf(w_kr2),
        "w_hy": bf(w_in[:, HY_START:GATE_START]), "w_gate": bf(w_in[:, GATE_START:]),
        "hy_short_w": p["hy_short_w"][layer], "hy_short_b": p["hy_short_b"][layer],
        "hy_w1": p["hy_w1"][layer], "hy_b1": p["hy_b1"][layer], "hy_w2": p["hy_w2"][layer],
        "hy_b2": p["hy_b2"][layer], "hy_w3": p["hy_w3"][layer], "hy_freq": p["hy_freq"][layer],
        "hy_decay": p["hy_decay"][layer], "hy_bias": row(p["hy_bias"][layer]),
        "w_br_attn": bf(p["w_br_attn"][layer]), "w_br_hy": bf(p["w_br_hy"][layer]), "w_out": bf(p["w_out"][layer]),
    }


def _rope_tables(n):
    rows = n // GRID_W
    n_freq = QK_ROPE // 4
    inv_freq = ROPE_BASE ** (-jnp.arange(n_freq, dtype=F32) / n_freq)
    r = jnp.repeat(jnp.arange(rows, dtype=F32), GRID_W)
    col = jnp.tile(jnp.arange(GRID_W, dtype=F32), rows)
    ang = jnp.concatenate([r[:, None] * inv_freq, col[:, None] * inv_freq], axis=-1)
    cos, sin = jnp.cos(ang), jnp.sin(ang)
    ones = jnp.ones((n, QK_NOPE), F32)
    zeros = jnp.zeros((n, QK_NOPE), F32)
    pad = jnp.zeros((n, HEAD_PAD - QK_NOPE - QK_ROPE), F32)
    return (jnp.concatenate([ones, cos, cos, pad], axis=-1), jnp.concatenate([zeros, sin, sin, pad], axis=-1))


def _token_mixer(xs, mod, lw, khat, conv_short, *, batch, n, tiles_per_group, fixed_group, rope_tabs,
                 k_ctx=None, v_ctx=None):
    pr = _inproj(xs, mod, lw, tiles_per_group=tiles_per_group, fixed_group=fixed_group, rope_tabs=rope_tabs,
                 want_q=True, want_hg=True)
    k3 = pr["k"].reshape(batch, n, -1)
    v3 = pr["v"].reshape(batch, n // VT_BLOCK, N_HEADS * HEAD_PAD, VT_BLOCK)
    if k_ctx is None:
        o_attn = _attention(pr["q"], k3, v3, None, None, batch)
    else:
        o_attn = _attention(pr["q"], k_ctx, v_ctx, k3, v3, batch)
    if conv_short:
        z, x0 = _hyena_pre(pr["p_hy"], lw["hy_short_w"], lw["hy_short_b"], batch, None)
        conv = _direct_conv(z, *khat, n)
    else:
        z, x0, zt = _hyena_pre(pr["p_hy"], lw["hy_short_w"], lw["hy_short_b"], batch, DFT_Q)
        conv = _long_conv(zt, khat)
    x_new = _merge(xs, mod, o_attn, conv, z, x0, pr["gates"], lw, tiles_per_group=tiles_per_group,
                   fixed_group=fixed_group)
    return x_new, k3, v3


def kernel(x, c, ctx, c_ctx, w_mod, b_mod, norm1_g, norm2_g, w_in, q_norm_g, kv_norm_g, w_uq, w_uk, w_uv, hy_short_w, hy_short_b, hy_w1, hy_b1, hy_w2, hy_b2, hy_w3, hy_freq, hy_decay, hy_bias, w_br_attn, w_br_hy, w_out, ffn_w_gate, ffn_w_up, ffn_w_down, moe_router, moe_w_gate, moe_w_up, moe_w_down, final_g):
    p = dict(w_in=w_in, norm1_g=norm1_g, norm2_g=norm2_g, q_norm_g=q_norm_g, kv_norm_g=kv_norm_g, w_uq=w_uq,
             w_uk=w_uk, w_uv=w_uv, hy_short_w=hy_short_w, hy_short_b=hy_short_b, hy_w1=hy_w1, hy_b1=hy_b1,
             hy_w2=hy_w2, hy_b2=hy_b2, hy_w3=hy_w3, hy_freq=hy_freq, hy_decay=hy_decay, hy_bias=hy_bias,
             w_br_attn=w_br_attn, w_br_hy=w_br_hy, w_out=w_out)
    batch, seq, d = x.shape
    ctx_len = ctx.shape[1]
    depth = w_mod.shape[0]
    ctx_group = batch
    cond8 = jnp.zeros((SUBLANES, d), F32).at[:batch].set(c).at[ctx_group].set(c_ctx)
    rope_tabs = _rope_tables(seq)
    lat_tiles = seq // ROW_TILE
    xs = x.reshape(batch * seq, d)
    cs = ctx.reshape(batch * ctx_len, d)
    bf = lambda v: v.astype(BF16)
    for layer in range(depth):
        last = layer == depth - 1
        lw = _layer_weights(p, layer)
        mod = _adaln(cond8, w_mod[layer], b_mod[layer])
        khat_lat = _long_conv_spectrum(seq, lw)
        if last:
            pr = _inproj(cs, mod, lw, tiles_per_group=None, fixed_group=ctx_group, rope_tabs=None, want_q=False,
                         want_hg=False)
            k_ctx = pr["k"].reshape(batch, ctx_len, -1)
            v_ctx = pr["v"].reshape(batch, ctx_len // VT_BLOCK, N_HEADS * HEAD_PAD, VT_BLOCK)
        else:
            khat_ctx = _hyena_filter(ctx_len, ctx_len, lw)
            cs_mid, k_ctx, v_ctx = _token_mixer(cs, mod, lw, khat_ctx, True, batch=batch, n=ctx_len,
                                                tiles_per_group=None, fixed_group=ctx_group, rope_tabs=None)
        xs, _, _ = _token_mixer(xs, mod, lw, khat_lat, False, batch=batch, n=seq, tiles_per_group=lat_tiles,
                                fixed_group=None, rope_tabs=rope_tabs, k_ctx=k_ctx, v_ctx=v_ctx)
        i = layer // 2
        n2 = lw["norm2_g"]
        if layer % 2 == 0:
            assert not last
            wg, wu, wd = bf(ffn_w_gate[i])[None], bf(ffn_w_up[i])[None], bf(ffn_w_down[i])[None]
            xs = _ffn(xs, mod, n2, wg, wu, wd, tiles_per_group=lat_tiles, fixed_group=None)
            cs = _ffn(cs_mid, mod, n2, wg, wu, wd, tiles_per_group=None, fixed_group=ctx_group)
        else:
            assert last
            router = jnp.pad(moe_router[i], ((0, 0), (0, LANES - N_EXPERTS)))
            xs = _moe(xs, mod, n2, router, bf(moe_w_gate[i]), bf(moe_w_up[i]), bf(moe_w_down[i]),
                      final_g.reshape(1, d), lat_tiles)
    return xs.reshape(batch, seq, d)
```

```python
import functools
import math

import numpy as np
import jax
import jax.numpy as jnp
from jax import lax
from jax.experimental import pallas as pl
from jax.experimental.pallas import tpu as pltpu

F32 = jnp.float32
BF16 = jnp.bfloat16
HIGHEST = lax.Precision.HIGHEST

D_MODEL = 1024
GRID_W = 64
EPS = 1e-6
N_HEADS = 8
Q_LORA = 384
KV_LORA = 256
QK_NOPE = 64
QK_ROPE = 32
V_HEAD = 64
ROPE_BASE = 10000.0
ATTN_SCALE = (QK_NOPE + QK_ROPE) ** -0.5
HY_WIDTH = 512
HY_EMB = 17
HY_HIDDEN = 64
D_FF = 2816
N_EXPERTS = 8
KV_START = Q_LORA
HY_START = KV_START + KV_LORA + QK_ROPE
GATE_START = HY_START + 3 * HY_WIDTH

LANES = 128
SUBLANES = 8
HEAD_PAD = LANES
VMEM_LIMIT = 56 * 2**20

ROW_TILE = 256
FFN_ROW_TILE = 512
FFN_CHUNK = 1408
MOE_ROW_TILE = 512
VT_BLOCK = ROW_TILE
ATTN_KV = 512
ATTN_Q_TILE = 512
Q_SCALE = ATTN_SCALE * math.log2(math.e)
DFT_P = 64
DFT_Q = 128
COLSLOT_BQ = 8


def _cparams(*sem):
    return pltpu.CompilerParams(dimension_semantics=sem, vmem_limit_bytes=VMEM_LIMIT)


def _dot(a, b):
    return jnp.dot(a, b, preferred_element_type=F32)


def _dot_hi(a, b):
    return jnp.dot(a, b, precision=HIGHEST, preferred_element_type=F32)


def _dot_dft(table, x):
    return _dot(table, x.astype(BF16))


def _rms(xf, g):
    return xf * lax.rsqrt(jnp.mean(xf * xf, axis=-1, keepdims=True) + EPS) * g


def _full(shape):
    nd = len(shape)
    return pl.BlockSpec(shape, lambda *_: (0,) * nd)


def _adaln_body(c_ref, w_ref, b_ref, o_ref):
    c = c_ref[...]
    o_ref[...] = _dot_hi(c * jax.nn.sigmoid(c), w_ref[...]) + b_ref[...]


def _adaln(cond8, w, b):
    d, n = w.shape
    return pl.pallas_call(
        _adaln_body,
        grid=(n // d,),
        in_specs=[_full((SUBLANES, d)), pl.BlockSpec((d, d), lambda j: (0, j)),
                  pl.BlockSpec((1, d), lambda j: (0, j))],
        out_specs=pl.BlockSpec((SUBLANES, d), lambda j: (0, j)),
        out_shape=jax.ShapeDtypeStruct((SUBLANES, n), F32),
        compiler_params=_cparams("arbitrary"),
        name="adaln",
    )(cond8, w, b.reshape(1, n))


def _mod_spec(chunk):
    return pl.BlockSpec((SUBLANES, D_MODEL), lambda t, *_: (0, chunk))


def _mod_row(ref, group):
    return ref[pl.ds(group, 1), :]


def _inproj_body(tiles_per_group, fixed_group, use_rope, want_q, want_hg, *refs):
    it = iter(refs)
    x_ref, sh_ref, sc_ref, g1_ref = next(it), next(it), next(it), next(it)
    wkv_ref, kvg_ref, wuk_ref, wuv_ref, wkr_ref = next(it), next(it), next(it), next(it), next(it)
    if want_q:
        wq_ref, qg_ref, wuqa_ref, wuqb_ref = next(it), next(it), next(it), next(it)
    if want_hg:
        why_ref, wgate_ref = next(it), next(it)
    if use_rope:
        cos_ref, sin_ref = next(it), next(it)
    k_out, v_out = next(it), next(it)
    if want_q:
        q_out = next(it)
    if want_hg:
        phy_out, gate_out = next(it), next(it)

    group = fixed_group if fixed_group is not None else pl.program_id(0) // tiles_per_group
    xf = x_ref[...]
    h = _rms(xf, g1_ref[...]) * (1.0 + _mod_row(sc_ref, group)) + _mod_row(sh_ref, group)
    h = h.astype(BF16)
    if use_rope:
        cos, sin = cos_ref[...], sin_ref[...]

    def rope(a, b):
        return a * cos + b * sin if use_rope else a

    ckv = _rms(_dot(h, wkv_ref[...]), kvg_ref[...]).astype(BF16)
    vt = lax.dot_general(wuv_ref[...], ckv, (((1,), (1,)), ((), ())), preferred_element_type=F32)
    vrow = lax.broadcasted_iota(jnp.int32, vt.shape, 0) & (HEAD_PAD - 1)
    v_out[0] = jnp.where(vrow == V_HEAD, 1.0, vt).astype(BF16)
    k_nope = _dot(ckv, wuk_ref[...])
    kr = _dot(h, wkr_ref[...])
    k_rope = rope(kr[:, :HEAD_PAD], kr[:, HEAD_PAD:])
    for hd in range(N_HEADS):
        sl = slice(hd * HEAD_PAD, (hd + 1) * HEAD_PAD)
        k_out[:, sl] = (k_nope[:, sl] + k_rope).astype(BF16)
    if want_q:
        qn = _rms(_dot(h, wq_ref[...]), qg_ref[...]).astype(BF16)
        qa = _dot(qn, wuqa_ref[...])
        qb = _dot(qn, wuqb_ref[...]) if use_rope else None
        for hd in range(N_HEADS):
            sl = slice(hd * HEAD_PAD, (hd + 1) * HEAD_PAD)
            q_out[:, sl] = (rope(qa[:, sl], None if qb is None else qb[:, sl]) * Q_SCALE).astype(BF16)
    if want_hg:
        n_hy = why_ref.shape[1]
        for c0 in range(0, n_hy, 512):
            phy_out[:, c0:c0 + 512] = _dot(h, why_ref[:, c0:c0 + 512])
        n_g = wgate_ref.shape[1]
        for c0 in range(0, n_g, 512):
            gate_out[:, c0:c0 + 512] = jax.nn.sigmoid(_dot(h, wgate_ref[:, c0:c0 + 512]))


def _inproj(x, mod, lw, *, tiles_per_group, fixed_group, rope_tabs, want_q, want_hg):
    rows = x.shape[0]
    nt = rows // ROW_TILE
    use_rope = rope_tabs is not None
    row_spec = lambda n: pl.BlockSpec((ROW_TILE, n), lambda t: (t, 0))
    ins = [x, mod, mod, lw["norm1_g"], lw["w_kv"], lw["kv_norm_g"], lw["w_uk"], lw["w_uv"], lw["w_kr"]]
    specs = [row_spec(D_MODEL), _mod_spec(0), _mod_spec(1), _full((1, D_MODEL)),
             _full(lw["w_kv"].shape), _full((1, KV_LORA)), _full(lw["w_uk"].shape), _full(lw["w_uv"].shape),
             _full(lw["w_kr"].shape)]
    if want_q:
        ins += [lw["w_q"], lw["q_norm_g"], lw["w_uq_a"], lw["w_uq_b"]]
        specs += [_full(lw["w_q"].shape), _full((1, Q_LORA)), _full(lw["w_uq_a"].shape), _full(lw["w_uq_b"].shape)]
    if want_hg:
        ins += [lw["w_hy"], lw["w_gate"]]
        specs += [_full(lw["w_hy"].shape), _full(lw["w_gate"].shape)]
    if use_rope:
        seq_tiles = rope_tabs[0].shape[0] // ROW_TILE
        ins += list(rope_tabs)
        specs += [pl.BlockSpec((ROW_TILE, HEAD_PAD), lambda t: (t % seq_tiles, 0))] * 2
    hp = N_HEADS * HEAD_PAD
    out_shape = [jax.ShapeDtypeStruct((rows, hp), BF16), jax.ShapeDtypeStruct((nt, hp, ROW_TILE), BF16)]
    out_specs = [row_spec(hp), pl.BlockSpec((1, hp, ROW_TILE), lambda t: (t, 0, 0))]
    if want_q:
        out_shape.append(jax.ShapeDtypeStruct((rows, hp), BF16))
        out_specs.append(row_spec(hp))
    if want_hg:
        out_shape += [jax.ShapeDtypeStruct((rows, 3 * HY_WIDTH), F32), jax.ShapeDtypeStruct((rows, 2 * D_MODEL), F32)]
        out_specs += [row_spec(3 * HY_WIDTH), row_spec(2 * D_MODEL)]
    outs = pl.pallas_call(
        functools.partial(_inproj_body, tiles_per_group, fixed_group, use_rope, want_q, want_hg),
        grid=(nt,), in_specs=specs, out_specs=out_specs, out_shape=out_shape,
        compiler_params=_cparams("arbitrary"), name="inproj",
    )(*ins)
    res = {"k": outs[0], "v": outs[1]}
    i = 2
    if want_q:
        res["q"] = outs[i]
        i += 1
    if want_hg:
        res["p_hy"], res["gates"] = outs[i], outs[i + 1]
    return res


def _attn_body(n_lat_blocks, *refs):
    if n_lat_blocks:
        q_ref, kc_ref, vc_ref, kl_ref, vl_ref, o_ref = refs
    else:
        q_ref, kc_ref, vc_ref, o_ref = refs
    tq = q_ref.shape[0]
    q = q_ref[...]
    heads = tuple(slice(hd * HEAD_PAD, (hd + 1) * HEAD_PAD) for hd in range(2))
    sub = ATTN_KV // VT_BLOCK

    def scores(kblk):
        return tuple(lax.dot_general(kblk[:, sl], q[:, sl], (((1,), (1,)), ((), ())), preferred_element_type=F32)
                     for sl in heads)

    def update(st, vt_blocks, carry):
        m_new = [jnp.maximum(carry[hd][0], jnp.max(st[hd], axis=0, keepdims=True)) for hd in range(2)]
        p = [jnp.exp2(st[hd] - m_new[hd]).astype(BF16) for hd in range(2)]
        out = []
        for hd in range(2):
            m, acc = carry[hd]
            pv = None
            for j, vt in enumerate(vt_blocks):
                r = _dot(vt[heads[hd], :], p[hd][j * VT_BLOCK:(j + 1) * VT_BLOCK])
                pv = r if pv is None else pv + r
            out.append((m_new[hd], jnp.exp2(m - m_new[hd]) * acc + pv))
        return tuple(out)

    init = tuple((jnp.full((1, tq), -jnp.inf, F32), jnp.zeros((HEAD_PAD, tq), F32)) for _ in range(2))
    carry = update(scores(kc_ref[0]), [vc_ref[0, 0]], init)
    if n_lat_blocks:
        def k_block(i):
            return kl_ref[0, pl.ds(pl.multiple_of(i * ATTN_KV, ATTN_KV), ATTN_KV), :]

        def v_blocks(i):
            return [vl_ref[0, i * sub + j] for j in range(sub)]

        def body(i, c):
            st, carry = c
            st_next = scores(k_block(i + 1))
            return st_next, update(st, v_blocks(i), carry)
        st, carry = lax.fori_loop(0, n_lat_blocks - 1, body, (scores(k_block(0)), carry), unroll=True)
        carry = update(st, v_blocks(n_lat_blocks - 1), carry)
    o_t = jnp.concatenate([acc[:V_HEAD] / acc[V_HEAD:V_HEAD + 1] for _, acc in carry], axis=0)
    o_ref[...] = o_t.T.astype(BF16)


def _attention(q, k_ctx, vt_ctx, k_lat, vt_lat, batch):
    rows = q.shape[0]
    tq = min(ATTN_Q_TILE, rows // batch)
    qt_per_b = rows // batch // tq
    ctx_len = k_ctx.shape[1]
    assert ctx_len == VT_BLOCK
    pair_w = 2 * HEAD_PAD
    ins = [q, k_ctx, vt_ctx]
    specs = [pl.BlockSpec((tq, pair_w), lambda b, hp, t: (b * qt_per_b + t, hp)),
             pl.BlockSpec((1, ctx_len, pair_w), lambda b, hp, t: (b, 0, hp)),
             pl.BlockSpec((1, 1, pair_w, VT_BLOCK), lambda b, hp, t: (b, 0, hp, 0))]
    n_lat_blocks = 0
    if k_lat is not None:
        lat_len = k_lat.shape[1]
        n_lat_blocks = lat_len // ATTN_KV
        ins += [k_lat, vt_lat]
        specs += [pl.BlockSpec((1, lat_len, pair_w), lambda b, hp, t: (b, 0, hp)),
                  pl.BlockSpec((1, lat_len // VT_BLOCK, pair_w, VT_BLOCK), lambda b, hp, t: (b, 0, hp, 0))]
    return pl.pallas_call(
        functools.partial(_attn_body, n_lat_blocks),
        grid=(batch, N_HEADS // 2, qt_per_b), in_specs=specs,
        out_specs=pl.BlockSpec((tq, 2 * V_HEAD), lambda b, hp, t: (b * qt_per_b + t, hp)),
        out_shape=jax.ShapeDtypeStruct((rows, N_HEADS * V_HEAD), BF16),
        compiler_params=_cparams("arbitrary", "arbitrary", "arbitrary"), name="attention",
    )(*ins)


def _hypre_body(seq_tiles, n_chunks, p_ref, prev_ref, next_ref, w_ref, b_ref, z_ref, x0_ref, *zt_ref):
    j = pl.program_id(0) % seq_tiles
    p = p_ref[...]
    tm = p.shape[0]
    row = lax.broadcasted_iota(jnp.int32, (tm, 1), 0)
    prev_row = jnp.where(j != 0, prev_ref[SUBLANES - 1:SUBLANES, :], 0.0)
    next_row = jnp.where(j != seq_tiles - 1, next_ref[0:1, :], 0.0)
    up = jnp.where(row == 0, prev_row, pltpu.roll(p, 1, 0))
    dn = jnp.where(row == tm - 1, next_row, pltpu.roll(p, tm - 1, 0))
    u = up * w_ref[0:1, :] + p * w_ref[1:2, :] + dn * w_ref[2:3, :] + b_ref[...]
    c = HY_WIDTH
    z = u[:, :c] * u[:, c:2 * c]
    z_ref[...] = z
    x0_ref[...] = u[:, 2 * c:]
    if zt_ref:
        q = tm // n_chunks
        for a in range(n_chunks):
            zt_ref[0][0, :, a * c:(a + 1) * c] = z[a * q:(a + 1) * q, :].astype(BF16)


def _hyena_pre(p_hy, short_w, short_b, batch, dft_q):
    rows = p_hy.shape[0]
    n = rows // batch
    seq_tiles = n // ROW_TILE
    nt = rows // ROW_TILE
    c3 = 3 * HY_WIDTH
    halo = ROW_TILE // SUBLANES
    last8 = rows // SUBLANES - 1
    specs = [pl.BlockSpec((ROW_TILE, c3), lambda t: (t, 0)),
             pl.BlockSpec((SUBLANES, c3), lambda t: (jnp.maximum(t * halo - 1, 0), 0)),
             pl.BlockSpec((SUBLANES, c3), lambda t: (jnp.minimum((t + 1) * halo, last8), 0)),
             _full((3, c3)), _full((1, c3))]
    out_shape = [jax.ShapeDtypeStruct((rows, HY_WIDTH), F32)] * 2
    out_specs = [pl.BlockSpec((ROW_TILE, HY_WIDTH), lambda t: (t, 0))] * 2
    n_chunks = 1
    if dft_q is not None:
        n_chunks = ROW_TILE // dft_q
        half_p = n // dft_q
        out_shape.append(jax.ShapeDtypeStruct((batch // 2, dft_q, 2 * half_p * HY_WIDTH), BF16))

        def zt_map(t):
            b, jt = t // seq_tiles, t % seq_tiles
            return (b // 2, 0, (b % 2) * seq_tiles + jt)
        out_specs.append(pl.BlockSpec((1, dft_q, n_chunks * HY_WIDTH), zt_map))
    return pl.pallas_call(
        functools.partial(_hypre_body, seq_tiles, n_chunks),
        grid=(nt,), in_specs=specs, out_specs=out_specs, out_shape=out_shape,
        compiler_params=_cparams("arbitrary"), name="hyena_pre",
    )(p_hy, p_hy, p_hy, short_w, short_b.reshape(1, c3))


def _filter_body(emb_ref, w1_ref, b1_ref, w2_ref, b2_ref, w3_ref, freq_ref, decay_ref, h0_ref, h1_ref, s_ref):
    a = pl.program_id(0)
    emb = emb_ref[...]
    freq = freq_ref[...]
    h = jnp.sin(freq * (_dot_hi(emb, w1_ref[...]) + b1_ref[...]))
    h = jnp.sin(freq * (_dot_hi(h, w2_ref[...]) + b2_ref[...]))
    h = _dot_hi(h, w3_ref[...]) * jnp.exp(-emb[:, 0:1] * jnp.abs(decay_ref[...]))
    c = HY_WIDTH
    row = lax.broadcasted_iota(jnp.int32, (emb.shape[0], 1), 0)
    h0 = h[:, :c]
    h1 = jnp.where(jnp.logical_and(a == 0, row == 0), 0.0, h[:, c:])
    h0_ref[0] = h0.astype(BF16)
    h1_ref[0] = h1.astype(BF16)
    part = jnp.sum(jnp.abs(h0) + jnp.abs(h1), axis=0, keepdims=True)

    @pl.when(a == 0)
    def _():
        s_ref[...] = jnp.zeros_like(s_ref)
    s_ref[...] += part


def _hyena_filter(n, q, lw):
    f32 = F32
    bands = (HY_EMB - 1) // 2
    t = jnp.linspace(0.0, 1.0, n, dtype=f32)[:, None]
    phase = (2.0 * math.pi / n) * jnp.arange(n, dtype=f32)[:, None] * jnp.linspace(1e-4, bands - 1, bands, dtype=f32)
    emb = jnp.concatenate([t, jnp.cos(phase), -jnp.sin(phase), jnp.zeros((n, 32 - HY_EMB), f32)], axis=-1)
    w1 = jnp.concatenate([lw["hy_w1"], jnp.zeros((32 - HY_EMB, HY_HIDDEN), f32)], axis=0)
    c = HY_WIDTH
    slots = n // q
    row = lambda v: v.reshape(1, -1)
    return pl.pallas_call(
        _filter_body,
        grid=(slots,),
        in_specs=[pl.BlockSpec((q, 32), lambda a: (a, 0)), _full((32, HY_HIDDEN)), _full((1, HY_HIDDEN)),
                  _full((HY_HIDDEN, HY_HIDDEN)), _full((1, HY_HIDDEN)), _full((HY_HIDDEN, 2 * c)),
                  _full((1, HY_HIDDEN)), _full((1, 2 * c))],
        out_specs=[pl.BlockSpec((1, q, c), lambda a: (0, 0, a)), pl.BlockSpec((1, q, c), lambda a: (0, 0, a)),
                   _full((1, c))],
        out_shape=[jax.ShapeDtypeStruct((1, q, slots * c), BF16)] * 2 + [jax.ShapeDtypeStruct((1, c), f32)],
        compiler_params=_cparams("arbitrary"), name="hyena_filter",
    )(emb, w1, row(lw["hy_b1"]), lw["hy_w2"], row(lw["hy_b2"]), lw["hy_w3"], row(lw["hy_freq"]),
      row(lw["hy_decay"]))


def _colslot_body(nparts, bq, c, *refs):
    x_refs, t_refs, o_ref = refs[:nparts], refs[nparts:2 * nparts], refs[-1]
    for j in range(bq):
        acc = None
        for x_ref, t_ref in zip(x_refs, t_refs):
            r = _dot_dft(t_ref[j], x_ref[0, j])
            acc = r if acc is None else acc + r
        o_ref[0, :, j * c:(j + 1) * c] = acc.astype(o_ref.dtype)


def _colslot_matmul(xs, tabs, out_dtype, c=HY_WIDTH):
    tabs = [jnp.asarray(t, BF16) for t in tabs]
    g, q = xs[0].shape[0], xs[0].shape[1]
    m = tabs[0].shape[1]
    bq = COLSLOT_BQ
    xs4 = [x.reshape(g, q, -1, c) for x in xs]
    specs = [pl.BlockSpec((1, bq, x.shape[2], c), lambda gi, qi: (gi, qi, 0, 0)) for x in xs4]
    specs += [pl.BlockSpec((bq, m, t.shape[2]), lambda gi, qi: (qi, 0, 0)) for t in tabs]
    return pl.pallas_call(
        functools.partial(_colslot_body, len(xs), bq, c),
        grid=(g, q // bq), in_specs=specs,
        out_specs=pl.BlockSpec((1, m, bq * c), lambda gi, qi: (gi, 0, qi)),
        out_shape=jax.ShapeDtypeStruct((g, m, q * c), out_dtype),
        compiler_params=_cparams("arbitrary", "arbitrary"), name="dft_stride_stage",
    )(*xs4, *tabs)


def _spec_mid_body(g_ref, m3_ref, m3i_ref, k_ref, hr_ref, hi_ref):
    q = hr_ref.shape[1]
    x = _dot_dft(m3_ref[...], g_ref[0, :, 0].reshape(2 * q, -1))
    kh = k_ref[0]
    xr, xi, kr, ki = x[:q], x[q:], kh[:q], kh[q:]
    y = jnp.concatenate([xr * kr - xi * ki, xr * ki + xi * kr], axis=0)
    h = _dot_dft(m3i_ref[...], y)
    hr_ref[0] = h[:q].astype(BF16)
    hi_ref[0] = h[q:].astype(BF16)


def _spec_mid(gf, m3, m3i, khat, p, q, c=HY_WIDTH):
    bp = gf.shape[0]
    g5 = gf.reshape(bp, 2, p, q, c)
    return pl.pallas_call(
        _spec_mid_body,
        grid=(bp, p),
        in_specs=[pl.BlockSpec((1, 2, 1, q, c), lambda b, ci: (b, 0, ci, 0, 0)), _full((2 * q, 2 * q)),
                  _full((2 * q, 2 * q)), pl.BlockSpec((1, 2 * q, c), lambda b, ci: (ci, 0, 0))],
        out_specs=[pl.BlockSpec((1, q, c), lambda b, ci: (b, 0, ci))] * 2,
        out_shape=[jax.ShapeDtypeStruct((bp, q, p * c), BF16)] * 2,
        compiler_params=_cparams("arbitrary", "arbitrary"), name="dft_mid",
    )(g5, m3, m3i, khat)


def _filter_spec_body(g_ref, m3k_ref, s_ref, k_ref):
    q4 = m3k_ref.shape[1]
    k_ref[0] = _dot_dft(m3k_ref[...], g_ref[0, :, 0].reshape(q4, -1)) / s_ref[...]


def _filter_spec(gk, m3k, s, p, q, c=HY_WIDTH):
    g5 = gk.reshape(1, 4, p, q, c)
    return pl.pallas_call(
        _filter_spec_body,
        grid=(p,),
        in_specs=[pl.BlockSpec((1, 4, 1, q, c), lambda ci: (0, 0, ci, 0, 0)), _full((2 * q, 4 * q)), _full((1, c))],
        out_specs=pl.BlockSpec((1, 2 * q, c), lambda ci: (ci, 0, 0)),
        out_shape=jax.ShapeDtypeStruct((p, 2 * q, c), F32),
        compiler_params=_cparams("arbitrary"), name="filter_spectrum",
    )(g5, m3k, s)


def _direct_conv_body(z_ref, h0_ref, h1_ref, s_ref, tf_ref, tk_ref, ti_ref, o_ref):
    n2 = tf_ref.shape[0] // 2
    kh = _dot_dft(tk_ref[...], jnp.concatenate([h0_ref[0], h1_ref[0]], axis=0)) / s_ref[...]
    x = _dot_dft(tf_ref[...], z_ref[0])
    xr, xi, kr, ki = x[:n2], x[n2:], kh[:n2], kh[n2:]
    y = jnp.concatenate([xr * kr - xi * ki, xr * ki + xi * kr], axis=0)
    o_ref[0] = _dot_dft(ti_ref[...], y)


def _direct_conv(z, h0, h1, s, n, c=HY_WIDTH):
    bp = z.shape[0] // (2 * n)
    tf, tk, ti = (jnp.asarray(t, BF16) for t in _direct_tables(n))
    out = pl.pallas_call(
        _direct_conv_body,
        grid=(bp,),
        in_specs=[pl.BlockSpec((1, 2 * n, c), lambda b: (b, 0, 0)), _full((1, n, c)), _full((1, n, c)), _full((1, c)),
                  _full(tf.shape), _full(tk.shape), _full(ti.shape)],
        out_specs=pl.BlockSpec((1, 2 * n, c), lambda b: (b, 0, 0)),
        out_shape=jax.ShapeDtypeStruct((bp, 2 * n, c), F32),
        compiler_params=_cparams("arbitrary"), name="direct_conv",
    )(z.reshape(bp, 2 * n, c), h0, h1, s, tf, tk, ti)
    return out.reshape(bp * 2 * n, c)


def _cis(m, n_total):
    ang = (2.0 * np.pi / n_total) * (m % n_total).astype(np.float64)
    return np.cos(ang), -np.sin(ang)


@functools.lru_cache(maxsize=None)
def _direct_tables(n):
    nn = 2 * n
    f = np.arange(nn)[:, None]
    t = np.arange(n)[None, :]
    cr, ci = _cis(f * t, nn)
    tf = np.block([[cr, -ci], [ci, cr]])
    tk = np.block([[cr, cr], [ci, -ci]])
    ti = np.block([[cr.T, ci.T], [-ci.T, cr.T]]) / nn
    return tuple(a.astype(np.float32) for a in (tf, tk, ti))


@functools.lru_cache(maxsize=None)
def _twostage_tables(p, q):
    nn = p * q
    hp = p // 2
    s = np.arange(q)[:, None, None]
    c = np.arange(p)[None, :, None]
    a = np.arange(hp)[None, None, :]
    tr, ti = _cis(c * (q * a + s), nn)
    z = np.zeros_like(tr)
    t_data = np.concatenate([np.concatenate([tr, -ti], 2), np.concatenate([ti, tr], 2)], 1)
    t_k0 = np.concatenate([tr, ti, z, z], 1)
    t_k1 = np.concatenate([z, z, tr, -ti], 1)
    trt, tit = np.swapaxes(tr, 1, 2) / nn, np.swapaxes(ti, 1, 2) / nn
    t_inv_r = np.concatenate([trt, -tit], 1)
    t_inv_i = np.concatenate([tit, trt], 1)
    d = np.arange(q)[:, None]
    b = np.arange(q)[None, :]
    fr, fi = _cis(d * b, q)
    m3 = np.block([[fr, -fi], [fi, fr]])
    m3i = np.block([[fr, fi], [-fi, fr]])
    m3k = np.block([[fr, -fi, fr, fi], [fi, fr, -fi, fr]])
    return tuple(x.astype(np.float32) for x in (t_data, t_k0, t_k1, t_inv_r, t_inv_i, m3, m3i, m3k))


def _long_conv_spectrum(n, lw):
    p, q = DFT_P, DFT_Q
    h0, h1, s = _hyena_filter(n, q, lw)
    _, t_k0, t_k1, _, _, _, _, m3k = _twostage_tables(p, q)
    gk = _colslot_matmul([h0, h1], [t_k0, t_k1], BF16)
    return _filter_spec(gk, jnp.asarray(m3k, BF16), s, p, q)


def _long_conv(zt, khat):
    p, q = DFT_P, DFT_Q
    t_data, _, _, t_inv_r, t_inv_i, m3, m3i, _ = _twostage_tables(p, q)
    gf = _colslot_matmul([zt], [t_data], BF16)
    hr, hi = _spec_mid(gf, jnp.asarray(m3, BF16), jnp.asarray(m3i, BF16), khat, p, q)
    y = _colslot_matmul([hr, hi], [t_inv_r, t_inv_i], F32)
    return y.reshape(-1, HY_WIDTH)


def _merge_body(tiles_per_group, fixed_group, x_ref, g1_ref, oa_ref, conv_ref, z_ref, x0_ref, gate_ref, bias_ref,
                wba_ref, wbh_ref, wout_ref, o_ref):
    group = fixed_group if fixed_group is not None else pl.program_id(0) // tiles_per_group
    o_hy = ((conv_ref[...] + z_ref[...] * bias_ref[...]) * x0_ref[...]).astype(BF16)
    d = D_MODEL
    merged = gate_ref[:, :d] * _dot(oa_ref[...], wba_ref[...]) + gate_ref[:, d:] * _dot(o_hy, wbh_ref[...])
    mix = _dot(merged.astype(BF16), wout_ref[...])
    o_ref[...] = x_ref[...] + _mod_row(g1_ref, group) * mix


def _merge(x, mod, o_attn, conv, z, x0, gates, lw, *, tiles_per_group, fixed_group):
    rows = x.shape[0]
    row_spec = lambda n: pl.BlockSpec((ROW_TILE, n), lambda t: (t, 0))
    c = HY_WIDTH
    return pl.pallas_call(
        functools.partial(_merge_body, tiles_per_group, fixed_group),
        grid=(rows // ROW_TILE,),
        in_specs=[row_spec(D_MODEL), _mod_spec(2), row_spec(N_HEADS * V_HEAD), row_spec(c), row_spec(c), row_spec(c),
                  row_spec(2 * D_MODEL), _full((1, c)), _full(lw["w_br_attn"].shape), _full(lw["w_br_hy"].shape),
                  _full(lw["w_out"].shape)],
        out_specs=row_spec(D_MODEL),
        out_shape=jax.ShapeDtypeStruct((rows, D_MODEL), F32),
        compiler_params=_cparams("arbitrary"), name="merge_out",
    )(x, mod, o_attn, conv, z, x0, gates, lw["hy_bias"], lw["w_br_attn"], lw["w_br_hy"], lw["w_out"])


def _swiglu_chunk(h, wg_ref, wu_ref, wd_ref):
    gate = _dot(h, wg_ref[0])
    up = _dot(h, wu_ref[0])
    return _dot((gate * jax.nn.sigmoid(gate) * up).astype(BF16), wd_ref[0])


def _ffn_body(tiles_per_group, fixed_group, x_ref, sh_ref, sc_ref, g2_ref, n2_ref, wg_ref, wu_ref, wd_ref, o_ref,
              h_scr, acc_scr):
    group = fixed_group if fixed_group is not None else pl.program_id(0) // tiles_per_group
    ci = pl.program_id(1)

    @pl.when(ci == 0)
    def _():
        h = _rms(x_ref[...], n2_ref[...]) * (1.0 + _mod_row(sc_ref, group)) + _mod_row(sh_ref, group)
        h_scr[...] = h.astype(BF16)
        acc_scr[...] = jnp.zeros_like(acc_scr)

    acc_scr[...] += _swiglu_chunk(h_scr[...], wg_ref, wu_ref, wd_ref)

    @pl.when(ci == pl.num_programs(1) - 1)
    def _():
        o_ref[...] = x_ref[...] + _mod_row(g2_ref, group) * acc_scr[...]


def _ffn(x, mod, norm_g, w_gate, w_up, w_down, *, tiles_per_group, fixed_group):
    rows = x.shape[0]
    tm = min(FFN_ROW_TILE, rows)
    n_chunks = D_FF // FFN_CHUNK
    row_spec = pl.BlockSpec((tm, D_MODEL), lambda t, ci: (t, 0))
    return pl.pallas_call(
        functools.partial(_ffn_body, tiles_per_group * ROW_TILE // tm if tiles_per_group else None, fixed_group),
        grid=(rows // tm, n_chunks),
        in_specs=[row_spec, _mod_spec(3), _mod_spec(4), _mod_spec(5), _full((1, D_MODEL)),
                  pl.BlockSpec((1, D_MODEL, FFN_CHUNK), lambda t, ci: (0, 0, ci)),
                  pl.BlockSpec((1, D_MODEL, FFN_CHUNK), lambda t, ci: (0, 0, ci)),
                  pl.BlockSpec((1, FFN_CHUNK, D_MODEL), lambda t, ci: (0, ci, 0))],
        out_specs=row_spec,
        out_shape=jax.ShapeDtypeStruct((rows, D_MODEL), F32),
        scratch_shapes=[pltpu.VMEM((tm, D_MODEL), BF16), pltpu.VMEM((tm, D_MODEL), F32)],
        compiler_params=_cparams("arbitrary", "arbitrary"), name="ffn",
    )(x, mod, mod, mod, norm_g, w_gate, w_up, w_down)


def _route_body(tiles_per_group, x_ref, sh_ref, sc_ref, n2_ref, router_ref, h_ref, route_ref):
    group = pl.program_id(0) // tiles_per_group
    h = _rms(x_ref[...], n2_ref[...]) * (1.0 + _mod_row(sc_ref, group)) + _mod_row(sh_ref, group)
    h_ref[...] = h
    logits = _dot_hi(h, router_ref[...])
    lane = lax.broadcasted_iota(jnp.int32, logits.shape, 1)
    logits = jnp.where(lane < N_EXPERTS, logits, -jnp.inf)
    m1 = jnp.max(logits, axis=-1, keepdims=True)
    i1 = jnp.min(jnp.where(logits == m1, lane, LANES), axis=-1, keepdims=True)
    rest = jnp.where(lane == i1, -jnp.inf, logits)
    m2 = jnp.max(rest, axis=-1, keepdims=True)
    i2 = jnp.min(jnp.where(rest == m2, lane, LANES), axis=-1, keepdims=True)
    e2 = jnp.exp(m2 - m1)
    w1 = 1.0 / (1.0 + e2)
    route_ref[...] = (jnp.where(lane == 0, i1.astype(F32), 0.0) + jnp.where(lane == 1, i2.astype(F32), 0.0)
                      + jnp.where(lane == 2, w1, 0.0) + jnp.where(lane == 3, e2 * w1, 0.0))


def _route(x, mod, norm_g, router, tiles_per_group):
    rows = x.shape[0]
    row_spec = lambda n: pl.BlockSpec((ROW_TILE, n), lambda t: (t, 0))
    return pl.pallas_call(
        functools.partial(_route_body, tiles_per_group),
        grid=(rows // ROW_TILE,),
        in_specs=[row_spec(D_MODEL), _mod_spec(3), _mod_spec(4), _full((1, D_MODEL)), _full(router.shape)],
        out_specs=[row_spec(D_MODEL), row_spec(LANES)],
        out_shape=[jax.ShapeDtypeStruct((rows, D_MODEL), F32), jax.ShapeDtypeStruct((rows, LANES), F32)],
        compiler_params=_cparams("arbitrary"), name="moe_route",
    )(x, mod, mod, norm_g, router)


def _gather_rows(idx_ref, n_rows, src_hbm, dst, sem):
    def issue(i, carry):
        for prio in range(2):
            r = 2 * i + prio
            pltpu.make_async_copy(src_hbm.at[pl.ds(idx_ref[0, 0, r], 1), :], dst.at[pl.ds(r, 1), :],
                                  sem).start(priority=prio)
        return carry
    lax.fori_loop(0, n_rows // 2, issue, 0, unroll=4)


def _wait_rows(n_rows, src_hbm, dst, sem):
    pltpu.make_async_copy(src_hbm.at[pl.ds(0, n_rows), :], dst, sem).wait()


def _moe_dispatch_body(idx_ref, h_ref, xs_in_hbm, xs_hbm, stage, sem):
    del xs_in_hbm
    t = pl.program_id(0)
    n_tiles = pl.num_programs(0)
    tm = h_ref.shape[0]
    slot = t % 2

    def wait(s):
        for _ in range(2):
            pltpu.make_async_copy(stage.at[s], xs_hbm.at[pl.ds(0, tm), :], sem.at[s]).wait()

    @pl.when(t >= 2)
    def _():
        wait(slot)

    stage[slot] = h_ref[...]

    def issue(r, carry):
        row = stage.at[slot, pl.ds(r, 1), :]
        pltpu.make_async_copy(row, xs_hbm.at[pl.ds(idx_ref[0, 0, r], 1), :], sem.at[slot]).start(priority=0)
        pltpu.make_async_copy(row, xs_hbm.at[pl.ds(idx_ref[0, 0, tm + r], 1), :], sem.at[slot]).start(priority=1)
        return carry
    lax.fori_loop(0, tm, issue, 0, unroll=4)

    @pl.when(t == n_tiles - 1)
    def _():
        wait(slot)
        wait(1 - slot)


def _tile_index_blocks(dest, tm):
    n_tiles = dest.shape[0] // tm
    return dest.reshape(n_tiles, tm, 2).transpose(0, 2, 1).reshape(n_tiles, 1, 2 * tm)


def _moe_dispatch(h, dest, n_rows):
    tm = ROW_TILE
    idx = _tile_index_blocks(dest, tm)
    n_tiles = idx.shape[0]
    assert n_tiles >= 2
    return pl.pallas_call(
        _moe_dispatch_body,
        grid=(n_tiles,),
        in_specs=[pl.BlockSpec((1, 1, 2 * tm), lambda t: (t, 0, 0), memory_space=pltpu.SMEM),
                  pl.BlockSpec((tm, D_MODEL), lambda t: (t, 0)), pl.BlockSpec(memory_space=pl.ANY)],
        out_specs=pl.BlockSpec(memory_space=pl.ANY),
        out_shape=jax.ShapeDtypeStruct((n_rows, D_MODEL), F32),
        scratch_shapes=[pltpu.VMEM((2, tm, D_MODEL), F32), pltpu.SemaphoreType.DMA((2,))],
        input_output_aliases={2: 0},
        compiler_params=_cparams("arbitrary"), name="moe_dispatch",
    )(idx, h, jnp.zeros((n_rows, D_MODEL), F32))


def _moe_group_body(te_ref, nu_ref, x_ref, wg_ref, wu_ref, wd_ref, y_ref, xb_scr, acc_scr):
    t, ci = pl.program_id(0), pl.program_id(1)

    @pl.when(ci == 0)
    def _():
        xb_scr[...] = x_ref[...].astype(BF16)
        acc_scr[...] = jnp.zeros_like(acc_scr)

    @pl.when(t < nu_ref[0])
    def _():
        acc_scr[...] += _swiglu_chunk(xb_scr[...], wg_ref, wu_ref, wd_ref)

    @pl.when(ci == pl.num_programs(1) - 1)
    def _():
        y_ref[...] = acc_scr[...]


def _moe_group(xs, tile_expert, n_used, w_gate, w_up, w_down):
    tm = MOE_ROW_TILE
    n_rows = xs.shape[0]
    n_chunks = D_FF // FFN_CHUNK

    def chunk(t, ci, nu):
        return jnp.where(t < nu[0], ci, n_chunks - 1)
    grid_spec = pltpu.PrefetchScalarGridSpec(
        num_scalar_prefetch=2,
        grid=(n_rows // tm, n_chunks),
        in_specs=[pl.BlockSpec((tm, D_MODEL), lambda t, ci, te, nu: (jnp.minimum(t, nu[0] - 1), 0)),
                  pl.BlockSpec((1, D_MODEL, FFN_CHUNK), lambda t, ci, te, nu: (te[t], 0, chunk(t, ci, nu))),
                  pl.BlockSpec((1, D_MODEL, FFN_CHUNK), lambda t, ci, te, nu: (te[t], 0, chunk(t, ci, nu))),
                  pl.BlockSpec((1, FFN_CHUNK, D_MODEL), lambda t, ci, te, nu: (te[t], chunk(t, ci, nu), 0))],
        out_specs=pl.BlockSpec((tm, D_MODEL), lambda t, ci, te, nu: (t, 0)),
        scratch_shapes=[pltpu.VMEM((tm, D_MODEL), BF16), pltpu.VMEM((tm, D_MODEL), F32)],
    )
    return pl.pallas_call(
        _moe_group_body, grid_spec=grid_spec,
        out_shape=jax.ShapeDtypeStruct((n_rows, D_MODEL), F32),
        compiler_params=_cparams("arbitrary", "arbitrary"), name="moe_group",
    )(tile_expert, n_used, xs, w_gate, w_up, w_down)


def _moe_combine_body(tiles_per_group, idx_ref, idx_next_ref, x_ref, g2_ref, fg_ref, route_ref, y_hbm, o_ref,
                      yg_scr, sem):
    t = pl.program_id(0)
    n_tiles = pl.num_programs(0)
    group = t // tiles_per_group
    tm = x_ref.shape[0]
    slot = t % 2

    @pl.when(t == 0)
    def _():
        _gather_rows(idx_ref, 2 * tm, y_hbm, yg_scr.at[0], sem.at[0])

    _wait_rows(2 * tm, y_hbm, yg_scr.at[slot], sem.at[slot])

    @pl.when(t + 1 < n_tiles)
    def _():
        _gather_rows(idx_next_ref, 2 * tm, y_hbm, yg_scr.at[1 - slot], sem.at[1 - slot])

    route = route_ref[...]
    y = route[:, 2:3] * yg_scr[slot, :tm] + route[:, 3:4] * yg_scr[slot, tm:]
    o_ref[...] = _rms(x_ref[...] + _mod_row(g2_ref, group) * y, fg_ref[...])


def _moe_combine(x, mod, final_g, route, y_sorted, dest, tiles_per_group):
    rows = x.shape[0]
    tm = ROW_TILE
    idx = _tile_index_blocks(dest, tm)
    n_tiles = idx.shape[0]
    return pl.pallas_call(
        functools.partial(_moe_combine_body, tiles_per_group),
        grid=(n_tiles,),
        in_specs=[pl.BlockSpec((1, 1, 2 * tm), lambda t: (t, 0, 0), memory_space=pltpu.SMEM),
                  pl.BlockSpec((1, 1, 2 * tm), lambda t: (jnp.minimum(t + 1, n_tiles - 1), 0, 0),
                               memory_space=pltpu.SMEM),
                  pl.BlockSpec((tm, D_MODEL), lambda t: (t, 0)), _mod_spec(5), _full((1, D_MODEL)),
                  pl.BlockSpec((tm, LANES), lambda t: (t, 0)), pl.BlockSpec(memory_space=pl.ANY)],
        out_specs=pl.BlockSpec((tm, D_MODEL), lambda t: (t, 0)),
        out_shape=jax.ShapeDtypeStruct((rows, D_MODEL), F32),
        scratch_shapes=[pltpu.VMEM((2, 2 * tm, D_MODEL), F32), pltpu.SemaphoreType.DMA((2,))],
        compiler_params=_cparams("arbitrary"), name="moe_combine",
    )(idx, idx, x, mod, final_g, route, y_sorted)


def _moe_plan(route, tm):
    t = route.shape[0]
    experts = jnp.concatenate([route[:, 0], route[:, 1]]).astype(jnp.int32)
    onehot = (experts[:, None] == jnp.arange(N_EXPERTS, dtype=jnp.int32)[None, :]).astype(jnp.int32)
    csum = jnp.cumsum(onehot, axis=0)
    rank = jnp.sum(csum * onehot, axis=1) - 1
    padded = (csum[-1] + tm - 1) // tm * tm
    ends = jnp.cumsum(padded)
    dest = jnp.sum((ends - padded)[None, :] * onehot, axis=1) + rank
    n_rows = 2 * t + N_EXPERTS * tm
    tile_start = jnp.arange(n_rows // tm, dtype=jnp.int32) * tm
    tile_expert = jnp.minimum(jnp.sum((tile_start[:, None] >= ends[None, :]).astype(jnp.int32), axis=1), N_EXPERTS - 1)
    n_used = (ends[-1:] // tm).astype(jnp.int32)
    return dest.reshape(2, t).T, tile_expert, n_used, n_rows


def _moe(x, mod, norm_g, router, w_gate, w_up, w_down, final_g, tiles_per_group):
    h, route = _route(x, mod, norm_g, router, tiles_per_group)
    dest, tile_expert, n_used, n_rows = _moe_plan(route, MOE_ROW_TILE)
    xs = _moe_dispatch(h, dest, n_rows)
    y_sorted = _moe_group(xs, tile_expert, n_used, w_gate, w_up, w_down)
    return _moe_combine(x, mod, final_g, route, y_sorted, dest, tiles_per_group)


def _pad_heads(w, width):
    k = w.shape[0]
    w = w.reshape(k, N_HEADS, width)
    return jnp.pad(w, ((0, 0), (0, 0), (0, HEAD_PAD - width))).reshape(k, N_HEADS * HEAD_PAD)


def _rot_cols(w):
    half = QK_ROPE // 2
    return jnp.concatenate([-w[..., half:], w[..., :half]], axis=-1)


def _layer_weights(p, layer):
    w_in = p["w_in"][layer]
    w_uq = p["w_uq"][layer].reshape(Q_LORA, N_HEADS, QK_NOPE + QK_ROPE)
    zeros_nope = jnp.zeros((Q_LORA, N_HEADS, QK_NOPE), F32)
    w_uq_b = jnp.concatenate([zeros_nope, _rot_cols(w_uq[..., QK_NOPE:])], axis=-1)
    w_kr = w_in[:, KV_START + KV_LORA:HY_START]
    zk = jnp.zeros((D_MODEL, QK_NOPE), F32)
    zp = jnp.zeros((D_MODEL, HEAD_PAD - QK_NOPE - QK_ROPE), F32)
    w_kr2 = jnp.concatenate([zk, w_kr, zp, zk, _rot_cols(w_kr), zp], axis=-1)
    row = lambda v: v.reshape(1, -1)
    bf = lambda v: v.astype(BF16)
    return {
        "norm1_g": row(p["norm1_g"][layer]), "norm2_g": row(p["norm2_g"][layer]),
        "w_q": bf(w_in[:, :Q_LORA]), "q_norm_g": row(p["q_norm_g"][layer]),
        "w_uq_a": bf(_pad_heads(w_uq.reshape(Q_LORA, -1), QK_NOPE + QK_ROPE)),
        "w_uq_b": bf(_pad_heads(w_uq_b.reshape(Q_LORA, -1), QK_NOPE + QK_ROPE)),
        "w_kv": bf(w_in[:, KV_START:KV_START + KV_LORA]), "kv_norm_g": row(p["kv_norm_g"][layer]),
        "w_uk": bf(_pad_heads(p["w_uk"][layer], QK_NOPE)), "w_uv": bf(_pad_heads(p["w_uv"][layer], V_HEAD).T),
        "w_kr": bf(w_kr2),
        "w_hy": bf(w_in[:, HY_START:GATE_START]), "w_gate": bf(w_in[:, GATE_START:]),
        "hy_short_w": p["hy_short_w"][layer], "hy_short_b": p["hy_short_b"][layer],
        "hy_w1": p["hy_w1"][layer], "hy_b1": p["hy_b1"][layer], "hy_w2": p["hy_w2"][layer],
        "hy_b2": p["hy_b2"][layer], "hy_w3": p["hy_w3"][layer], "hy_freq": p["hy_freq"][layer],
        "hy_decay": p["hy_decay"][layer], "hy_bias": row(p["hy_bias"][layer]),
        "w_br_attn": bf(p["w_br_attn"][layer]), "w_br_hy": bf(p["w_br_hy"][layer]), "w_out": bf(p["w_out"][layer]),
    }


def _rope_tables(n):
    rows = n // GRID_W
    n_freq = QK_ROPE // 4
    inv_freq = ROPE_BASE ** (-jnp.arange(n_freq, dtype=F32) / n_freq)
    r = jnp.repeat(jnp.arange(rows, dtype=F32), GRID_W)
    col = jnp.tile(jnp.arange(GRID_W, dtype=F32), rows)
    ang = jnp.concatenate([r[:, None] * inv_freq, col[:, None] * inv_freq], axis=-1)
    cos, sin = jnp.cos(ang), jnp.sin(ang)
    ones = jnp.ones((n, QK_NOPE), F32)
    zeros = jnp.zeros((n, QK_NOPE), F32)
    pad = jnp.zeros((n, HEAD_PAD - QK_NOPE - QK_ROPE), F32)
    return (jnp.concatenate([ones, cos, cos, pad], axis=-1), jnp.concatenate([zeros, sin, sin, pad], axis=-1))


def _token_mixer(xs, mod, lw, khat, conv_short, *, batch, n, tiles_per_group, fixed_group, rope_tabs,
                 k_ctx=None, v_ctx=None):
    pr = _inproj(xs, mod, lw, tiles_per_group=tiles_per_group, fixed_group=fixed_group, rope_tabs=rope_tabs,
                 want_q=True, want_hg=True)
    k3 = pr["k"].reshape(batch, n, -1)
    v3 = pr["v"].reshape(batch, n // VT_BLOCK, N_HEADS * HEAD_PAD, VT_BLOCK)
    if k_ctx is None:
        o_attn = _attention(pr["q"], k3, v3, None, None, batch)
    else:
        o_attn = _attention(pr["q"], k_ctx, v_ctx, k3, v3, batch)
    if conv_short:
        z, x0 = _hyena_pre(pr["p_hy"], lw["hy_short_w"], lw["hy_short_b"], batch, None)
        conv = _direct_conv(z, *khat, n)
    else:
        z, x0, zt = _hyena_pre(pr["p_hy"], lw["hy_short_w"], lw["hy_short_b"], batch, DFT_Q)
        conv = _long_conv(zt, khat)
    x_new = _merge(xs, mod, o_attn, conv, z, x0, pr["gates"], lw, tiles_per_group=tiles_per_group,
                   fixed_group=fixed_group)
    return x_new, k3, v3


def kernel(x, c, ctx, c_ctx, w_mod, b_mod, norm1_g, norm2_g, w_in, q_norm_g, kv_norm_g, w_uq, w_uk, w_uv, hy_short_w, hy_short_b, hy_w1, hy_b1, hy_w2, hy_b2, hy_w3, hy_freq, hy_decay, hy_bias, w_br_attn, w_br_hy, w_out, ffn_w_gate, ffn_w_up, ffn_w_down, moe_router, moe_w_gate, moe_w_up, moe_w_down, final_g):
    p = dict(w_in=w_in, norm1_g=norm1_g, norm2_g=norm2_g, q_norm_g=q_norm_g, kv_norm_g=kv_norm_g, w_uq=w_uq,
             w_uk=w_uk, w_uv=w_uv, hy_short_w=hy_short_w, hy_short_b=hy_short_b, hy_w1=hy_w1, hy_b1=hy_b1,
             hy_w2=hy_w2, hy_b2=hy_b2, hy_w3=hy_w3, hy_freq=hy_freq, hy_decay=hy_decay, hy_bias=hy_bias,
             w_br_attn=w_br_attn, w_br_hy=w_br_hy, w_out=w_out)
    batch, seq, d = x.shape
    ctx_len = ctx.shape[1]
    depth = w_mod.shape[0]
    ctx_group = batch
    cond8 = jnp.zeros((SUBLANES, d), F32).at[:batch].set(c).at[ctx_group].set(c_ctx)
    rope_tabs = _rope_tables(seq)
    lat_tiles = seq // ROW_TILE
    xs = x.reshape(batch * seq, d)
    cs = ctx.reshape(batch * ctx_len, d)
    bf = lambda v: v.astype(BF16)
    for layer in range(depth):
        last = layer == depth - 1
        lw = _layer_weights(p, layer)
        mod = _adaln(cond8, w_mod[layer], b_mod[layer])
        khat_lat = _long_conv_spectrum(seq, lw)
        if last:
            pr = _inproj(cs, mod, lw, tiles_per_group=None, fixed_group=ctx_group, rope_tabs=None, want_q=False,
                         want_hg=False)
            k_ctx = pr["k"].reshape(batch, ctx_len, -1)
            v_ctx = pr["v"].reshape(batch, ctx_len // VT_BLOCK, N_HEADS * HEAD_PAD, VT_BLOCK)
        else:
            khat_ctx = _hyena_filter(ctx_len, ctx_len, lw)
            cs_mid, k_ctx, v_ctx = _token_mixer(cs, mod, lw, khat_ctx, True, batch=batch, n=ctx_len,
                                                tiles_per_group=None, fixed_group=ctx_group, rope_tabs=None)
        xs, _, _ = _token_mixer(xs, mod, lw, khat_lat, False, batch=batch, n=seq, tiles_per_group=lat_tiles,
                                fixed_group=None, rope_tabs=rope_tabs, k_ctx=k_ctx, v_ctx=v_ctx)
        i = layer // 2
        n2 = lw["norm2_g"]
        if layer % 2 == 0:
            assert not last
            wg, wu, wd = bf(ffn_w_gate[i])[None], bf(ffn_w_up[i])[None], bf(ffn_w_down[i])[None]
            xs = _ffn(xs, mod, n2, wg, wu, wd, tiles_per_group=lat_tiles, fixed_group=None)
            cs = _ffn(cs_mid, mod, n2, wg, wu, wd, tiles_per_group=None, fixed_group=ctx_group)
        else:
            assert last
            router = jnp.pad(moe_router[i], ((0, 0), (0, LANES - N_EXPERTS)))
            xs = _moe(xs, mod, n2, router, bf(moe_w_gate[i]), bf(moe_w_up[i]), bf(moe_w_down[i]),
                      final_g.reshape(1, d), lat_tiles)
    return xs.reshape(batch, seq, d)
```

```python
import functools
import math

import numpy as np
import jax
import jax.numpy as jnp
from jax import lax
from jax.experimental import pallas as pl
from jax.experimental.pallas import tpu as pltpu

F32 = jnp.float32
BF16 = jnp.bfloat16
HIGHEST = lax.Precision.HIGHEST

D_MODEL = 1024
GRID_W = 64
EPS = 1e-6
N_HEADS = 8
Q_LORA = 384
KV_LORA = 256
QK_NOPE = 64
QK_ROPE = 32
V_HEAD = 64
ROPE_BASE = 10000.0
ATTN_SCALE = (QK_NOPE + QK_ROPE) ** -0.5
HY_WIDTH = 512
HY_EMB = 17
HY_HIDDEN = 64
D_FF = 2816
N_EXPERTS = 8
KV_START = Q_LORA
HY_START = KV_START + KV_LORA + QK_ROPE
GATE_START = HY_START + 3 * HY_WIDTH

LANES = 128
SUBLANES = 8
HEAD_PAD = LANES
VMEM_LIMIT = 56 * 2**20

ROW_TILE = 256
INPROJ_ROW_TILE = 512
FFN_ROW_TILE = 512
FFN_CHUNK = 1408
MOE_ROW_TILE = 512
VT_BLOCK = ROW_TILE
ATTN_KV = 512
ATTN_Q_TILE = 1024
Q_SCALE = ATTN_SCALE * math.log2(math.e)
DFT_P = 64
DFT_Q = 128
COLSLOT_BQ = 16
DFT_CB = 4


def _cparams(*sem):
    return pltpu.CompilerParams(dimension_semantics=sem, vmem_limit_bytes=VMEM_LIMIT)


def _dot(a, b):
    return jnp.dot(a, b, preferred_element_type=F32)


def _dot_hi(a, b):
    return jnp.dot(a, b, precision=HIGHEST, preferred_element_type=F32)


def _dot_dft(table, x):
    return _dot(table, x.astype(BF16))


def _rms(xf, g):
    return xf * lax.rsqrt(jnp.mean(xf * xf, axis=-1, keepdims=True) + EPS) * g


def _full(shape):
    nd = len(shape)
    return pl.BlockSpec(shape, lambda *_: (0,) * nd, pipeline_mode=pl.Buffered(1))


def _adaln_body(c_ref, w_ref, b_ref, o_ref):
    c = c_ref[...]
    o_ref[...] = _dot_hi(c * jax.nn.sigmoid(c), w_ref[...]) + b_ref[...]


def _adaln(cond8, w, b):
    d, n = w.shape
    return pl.pallas_call(
        _adaln_body,
        grid=(n // d,),
        in_specs=[_full((SUBLANES, d)), pl.BlockSpec((d, d), lambda j: (0, j)),
                  pl.BlockSpec((1, d), lambda j: (0, j))],
        out_specs=pl.BlockSpec((SUBLANES, d), lambda j: (0, j)),
        out_shape=jax.ShapeDtypeStruct((SUBLANES, n), F32),
        compiler_params=_cparams("arbitrary"),
        name="adaln",
    )(cond8, w, b.reshape(1, n))


def _mod_spec(chunk):
    return pl.BlockSpec((SUBLANES, D_MODEL), lambda t, *_: (0, chunk))


def _mod_row(ref, group):
    return ref[pl.ds(group, 1), :]


def _inproj_body(tiles_per_group, fixed_group, use_rope, want_q, want_hg, *refs):
    it = iter(refs)
    x_ref, sh_ref, sc_ref, g1_ref = next(it), next(it), next(it), next(it)
    wkv_ref, kvg_ref, wuk_ref, wuv_ref, wkr_ref = next(it), next(it), next(it), next(it), next(it)
    if want_q:
        wq_ref, qg_ref, wuqa_ref, wuqb_ref = next(it), next(it), next(it), next(it)
    if want_hg:
        why_ref, wgate_ref = next(it), next(it)
    if use_rope:
        cos_ref, sin_ref = next(it), next(it)
    k_out, v_out = next(it), next(it)
    if want_q:
        q_out = next(it)
    if want_hg:
        phy_out, gate_out = next(it), next(it)

    group = fixed_group if fixed_group is not None else pl.program_id(0) // tiles_per_group
    xf = x_ref[...]
    h = _rms(xf, g1_ref[...]) * (1.0 + _mod_row(sc_ref, group)) + _mod_row(sh_ref, group)
    h = h.astype(BF16)
    if use_rope:
        cos, sin = cos_ref[...], sin_ref[...]

    def rope(a, b):
        return a * cos + b * sin if use_rope else a

    ckv = _rms(_dot(h, wkv_ref[...]), kvg_ref[...]).astype(BF16)
    vt = lax.dot_general(wuv_ref[...], ckv, (((1,), (1,)), ((), ())), preferred_element_type=F32)
    vrow = lax.broadcasted_iota(jnp.int32, vt.shape, 0) & (HEAD_PAD - 1)
    vt = jnp.where(vrow == V_HEAD, 1.0, vt).astype(BF16)
    for j in range(v_out.shape[0]):
        v_out[j] = vt[:, j * VT_BLOCK:(j + 1) * VT_BLOCK]
    k_nope = _dot(ckv, wuk_ref[...])
    kr = _dot(h, wkr_ref[...])
    k_rope = rope(kr[:, :HEAD_PAD], kr[:, HEAD_PAD:])
    for hd in range(N_HEADS):
        sl = slice(hd * HEAD_PAD, (hd + 1) * HEAD_PAD)
        k_out[:, sl] = (k_nope[:, sl] + k_rope).astype(BF16)
    if want_q:
        qn = _rms(_dot(h, wq_ref[...]), qg_ref[...]).astype(BF16)
        qa = _dot(qn, wuqa_ref[...])
        qb = _dot(qn, wuqb_ref[...]) if use_rope else None
        for hd in range(N_HEADS):
            sl = slice(hd * HEAD_PAD, (hd + 1) * HEAD_PAD)
            q_out[:, sl] = (rope(qa[:, sl], None if qb is None else qb[:, sl]) * Q_SCALE).astype(BF16)
    if want_hg:
        n_hy = why_ref.shape[1]
        for c0 in range(0, n_hy, 512):
            phy_out[:, c0:c0 + 512] = _dot(h, why_ref[:, c0:c0 + 512])
        n_g = wgate_ref.shape[1]
        for c0 in range(0, n_g, 512):
            gate_out[:, c0:c0 + 512] = jax.nn.sigmoid(_dot(h, wgate_ref[:, c0:c0 + 512])).astype(BF16)


def _inproj(x, mod, lw, *, tiles_per_group, fixed_group, rope_tabs, want_q, want_hg):
    rows = x.shape[0]
    tm = INPROJ_ROW_TILE
    nt = rows // tm
    if tiles_per_group is not None:
        tiles_per_group = tiles_per_group * ROW_TILE // tm
    use_rope = rope_tabs is not None
    row_spec = lambda n: pl.BlockSpec((tm, n), lambda t: (t, 0))
    ins = [x, mod, mod, lw["norm1_g"], lw["w_kv"], lw["kv_norm_g"], lw["w_uk"], lw["w_uv"], lw["w_kr"]]
    specs = [row_spec(D_MODEL), _mod_spec(0), _mod_spec(1), _full((1, D_MODEL)),
             _full(lw["w_kv"].shape), _full((1, KV_LORA)), _full(lw["w_uk"].shape), _full(lw["w_uv"].shape),
             _full(lw["w_kr"].shape)]
    if want_q:
        ins += [lw["w_q"], lw["q_norm_g"], lw["w_uq_a"], lw["w_uq_b"]]
        specs += [_full(lw["w_q"].shape), _full((1, Q_LORA)), _full(lw["w_uq_a"].shape), _full(lw["w_uq_b"].shape)]
    if want_hg:
        ins += [lw["w_hy"], lw["w_gate"]]
        specs += [_full(lw["w_hy"].shape), _full(lw["w_gate"].shape)]
    if use_rope:
        seq_tiles = rope_tabs[0].shape[0] // tm
        ins += list(rope_tabs)
        specs += [pl.BlockSpec((tm, HEAD_PAD), lambda t: (t % seq_tiles, 0))] * 2
    hp = N_HEADS * HEAD_PAD
    vt_blocks = tm // VT_BLOCK
    out_shape = [jax.ShapeDtypeStruct((rows, hp), BF16), jax.ShapeDtypeStruct((rows // VT_BLOCK, hp, VT_BLOCK), BF16)]
    out_specs = [row_spec(hp), pl.BlockSpec((vt_blocks, hp, VT_BLOCK), lambda t: (t, 0, 0))]
    if want_q:
        out_shape.append(jax.ShapeDtypeStruct((rows, hp), BF16))
        out_specs.append(row_spec(hp))
    if want_hg:
        out_shape += [jax.ShapeDtypeStruct((rows, 3 * HY_WIDTH), F32), jax.ShapeDtypeStruct((rows, 2 * D_MODEL), BF16)]
        out_specs += [row_spec(3 * HY_WIDTH), row_spec(2 * D_MODEL)]
    outs = pl.pallas_call(
        functools.partial(_inproj_body, tiles_per_group, fixed_group, use_rope, want_q, want_hg),
        grid=(nt,), in_specs=specs, out_specs=out_specs, out_shape=out_shape,
        compiler_params=_cparams("arbitrary"), name="inproj",
    )(*ins)
    res = {"k": outs[0], "v": outs[1]}
    i = 2
    if want_q:
        res["q"] = outs[i]
        i += 1
    if want_hg:
        res["p_hy"], res["gates"] = outs[i], outs[i + 1]
    return res


def _attn_body(n_lat_blocks, *refs):
    if n_lat_blocks:
        q_ref, kc_ref, vc_ref, kl_ref, vl_ref, o_ref = refs
    else:
        q_ref, kc_ref, vc_ref, o_ref = refs
    tq = q_ref.shape[0]
    q = q_ref[...]
    heads = tuple(slice(hd * HEAD_PAD, (hd + 1) * HEAD_PAD) for hd in range(2))
    sub = ATTN_KV // VT_BLOCK

    def scores(kblk):
        return tuple(lax.dot_general(kblk[:, sl], q[:, sl], (((1,), (1,)), ((), ())), preferred_element_type=F32)
                     for sl in heads)

    def update(st, vt_blocks, carry):
        m_new = [jnp.maximum(carry[hd][0], jnp.max(st[hd], axis=0, keepdims=True)) for hd in range(2)]
        p = [jnp.exp2(st[hd] - m_new[hd]).astype(BF16) for hd in range(2)]
        out = []
        for hd in range(2):
            m, acc = carry[hd]
            pv = None
            for j, vt in enumerate(vt_blocks):
                r = _dot(vt[heads[hd], :], p[hd][j * VT_BLOCK:(j + 1) * VT_BLOCK])
                pv = r if pv is None else pv + r
            out.append((m_new[hd], jnp.exp2(m - m_new[hd]) * acc + pv))
        return tuple(out)

    init = tuple((jnp.full((1, tq), -jnp.inf, F32), jnp.zeros((HEAD_PAD, tq), F32)) for _ in range(2))
    carry = update(scores(kc_ref[0]), [vc_ref[0, 0]], init)
    if n_lat_blocks:
        def k_block(i):
            return kl_ref[0, pl.ds(pl.multiple_of(i * ATTN_KV, ATTN_KV), ATTN_KV), :]

        def v_blocks(i):
            return [vl_ref[0, i * sub + j] for j in range(sub)]

        def body(i, c):
            st, carry = c
            st_next = scores(k_block(i + 1))
            return st_next, update(st, v_blocks(i), carry)
        st, carry = lax.fori_loop(0, n_lat_blocks - 1, body, (scores(k_block(0)), carry), unroll=True)
        carry = update(st, v_blocks(n_lat_blocks - 1), carry)
    o_t = jnp.concatenate([acc[:V_HEAD] / acc[V_HEAD:V_HEAD + 1] for _, acc in carry], axis=0)
    o_ref[...] = o_t.T.astype(BF16)


def _attention(q, k_ctx, vt_ctx, k_lat, vt_lat, batch):
    rows = q.shape[0]
    tq = min(ATTN_Q_TILE, rows // batch)
    qt_per_b = rows // batch // tq
    ctx_len = k_ctx.shape[1]
    assert ctx_len == VT_BLOCK
    pair_w = 2 * HEAD_PAD
    ins = [q, k_ctx, vt_ctx]
    specs = [pl.BlockSpec((tq, pair_w), lambda b, hp, t: (b * qt_per_b + t, hp)),
             pl.BlockSpec((1, ctx_len, pair_w), lambda b, hp, t: (b, 0, hp)),
             pl.BlockSpec((1, 1, pair_w, VT_BLOCK), lambda b, hp, t: (b, 0, hp, 0))]
    n_lat_blocks = 0
    if k_lat is not None:
        lat_len = k_lat.shape[1]
        n_lat_blocks = lat_len // ATTN_KV
        ins += [k_lat, vt_lat]
        specs += [pl.BlockSpec((1, lat_len, pair_w), lambda b, hp, t: (b, 0, hp)),
                  pl.BlockSpec((1, lat_len // VT_BLOCK, pair_w, VT_BLOCK), lambda b, hp, t: (b, 0, hp, 0))]
    return pl.pallas_call(
        functools.partial(_attn_body, n_lat_blocks),
        grid=(batch, N_HEADS // 2, qt_per_b), in_specs=specs,
        out_specs=pl.BlockSpec((tq, 2 * V_HEAD), lambda b, hp, t: (b * qt_per_b + t, hp)),
        out_shape=jax.ShapeDtypeStruct((rows, N_HEADS * V_HEAD), BF16),
        compiler_params=_cparams("arbitrary", "arbitrary", "arbitrary"), name="attention",
    )(*ins)


def _hypre_body(seq_tiles, n_chunks, p_ref, prev_ref, next_ref, w_ref, b_ref, z_ref, x0_ref, *zt_ref):
    j = pl.program_id(0) % seq_tiles
    p = p_ref[...]
    tm = p.shape[0]
    row = lax.broadcasted_iota(jnp.int32, (tm, 1), 0)
    prev_row = jnp.where(j != 0, prev_ref[SUBLANES - 1:SUBLANES, :], 0.0)
    next_row = jnp.where(j != seq_tiles - 1, next_ref[0:1, :], 0.0)
    up = jnp.where(row == 0, prev_row, pltpu.roll(p, 1, 0))
    dn = jnp.where(row == tm - 1, next_row, pltpu.roll(p, tm - 1, 0))
    u = up * w_ref[0:1, :] + p * w_ref[1:2, :] + dn * w_ref[2:3, :] + b_ref[...]
    c = HY_WIDTH
    z = u[:, :c] * u[:, c:2 * c]
    z_ref[...] = z
    x0_ref[...] = u[:, 2 * c:]
    if zt_ref:
        q = tm // n_chunks
        for a in range(n_chunks):
            zt_ref[0][0, :, a * c:(a + 1) * c] = z[a * q:(a + 1) * q, :].astype(BF16)


def _hyena_pre(p_hy, short_w, short_b, batch, dft_q):
    rows = p_hy.shape[0]
    n = rows // batch
    seq_tiles = n // ROW_TILE
    nt = rows // ROW_TILE
    c3 = 3 * HY_WIDTH
    halo = ROW_TILE // SUBLANES
    last8 = rows // SUBLANES - 1
    specs = [pl.BlockSpec((ROW_TILE, c3), lambda t: (t, 0)),
             pl.BlockSpec((SUBLANES, c3), lambda t: (jnp.maximum(t * halo - 1, 0), 0)),
             pl.BlockSpec((SUBLANES, c3), lambda t: (jnp.minimum((t + 1) * halo, last8), 0)),
             _full((3, c3)), _full((1, c3))]
    out_shape = [jax.ShapeDtypeStruct((rows, HY_WIDTH), F32)] * 2
    out_specs = [pl.BlockSpec((ROW_TILE, HY_WIDTH), lambda t: (t, 0))] * 2
    n_chunks = 1
    if dft_q is not None:
        n_chunks = ROW_TILE // dft_q
        half_p = n // dft_q
        out_shape.append(jax.ShapeDtypeStruct((batch // 2, dft_q, 2 * half_p * HY_WIDTH), BF16))

        def zt_map(t):
            b, jt = t // seq_tiles, t % seq_tiles
            return (b // 2, 0, (b % 2) * seq_tiles + jt)
        out_specs.append(pl.BlockSpec((1, dft_q, n_chunks * HY_WIDTH), zt_map))
    return pl.pallas_call(
        functools.partial(_hypre_body, seq_tiles, n_chunks),
        grid=(nt,), in_specs=specs, out_specs=out_specs, out_shape=out_shape,
        compiler_params=_cparams("arbitrary"), name="hyena_pre",
    )(p_hy, p_hy, p_hy, short_w, short_b.reshape(1, c3))


def _filter_body(emb_ref, w1_ref, b1_ref, w2_ref, b2_ref, w3_ref, freq_ref, decay_ref, h0_ref, h1_ref, s_ref):
    a = pl.program_id(0)
    emb = emb_ref[...]
    freq = freq_ref[...]
    h = jnp.sin(freq * (_dot_hi(emb, w1_ref[...]) + b1_ref[...]))
    h = jnp.sin(freq * (_dot_hi(h, w2_ref[...]) + b2_ref[...]))
    h = _dot_hi(h, w3_ref[...]) * jnp.exp(-emb[:, 0:1] * jnp.abs(decay_ref[...]))
    c = HY_WIDTH
    row = lax.broadcasted_iota(jnp.int32, (emb.shape[0], 1), 0)
    h0 = h[:, :c]
    h1 = jnp.where(jnp.logical_and(a == 0, row == 0), 0.0, h[:, c:])
    h0_ref[0] = h0.astype(BF16)
    h1_ref[0] = h1.astype(BF16)
    part = jnp.sum(jnp.abs(h0) + jnp.abs(h1), axis=0, keepdims=True)

    @pl.when(a == 0)
    def _():
        s_ref[...] = jnp.zeros_like(s_ref)
    s_ref[...] += part


def _hyena_filter(n, q, lw):
    f32 = F32
    bands = (HY_EMB - 1) // 2
    t = jnp.linspace(0.0, 1.0, n, dtype=f32)[:, None]
    phase = (2.0 * math.pi / n) * jnp.arange(n, dtype=f32)[:, None] * jnp.linspace(1e-4, bands - 1, bands, dtype=f32)
    emb = jnp.concatenate([t, jnp.cos(phase), -jnp.sin(phase), jnp.zeros((n, 32 - HY_EMB), f32)], axis=-1)
    w1 = jnp.concatenate([lw["hy_w1"], jnp.zeros((32 - HY_EMB, HY_HIDDEN), f32)], axis=0)
    c = HY_WIDTH
    slots = n // q
    row = lambda v: v.reshape(1, -1)
    return pl.pallas_call(
        _filter_body,
        grid=(slots,),
        in_specs=[pl.BlockSpec((q, 32), lambda a: (a, 0)), _full((32, HY_HIDDEN)), _full((1, HY_HIDDEN)),
                  _full((HY_HIDDEN, HY_HIDDEN)), _full((1, HY_HIDDEN)), _full((HY_HIDDEN, 2 * c)),
                  _full((1, HY_HIDDEN)), _full((1, 2 * c))],
        out_specs=[pl.BlockSpec((1, q, c), lambda a: (0, 0, a)), pl.BlockSpec((1, q, c), lambda a: (0, 0, a)),
                   pl.BlockSpec((1, c), lambda a: (0, 0))],
        out_shape=[jax.ShapeDtypeStruct((1, q, slots * c), BF16)] * 2 + [jax.ShapeDtypeStruct((1, c), f32)],
        compiler_params=_cparams("arbitrary"), name="hyena_filter",
    )(emb, w1, row(lw["hy_b1"]), lw["hy_w2"], row(lw["hy_b2"]), lw["hy_w3"], row(lw["hy_freq"]),
      row(lw["hy_decay"]))


def _colslot_body(nparts, bq, c, *refs):
    x_refs, t_refs, o_ref = refs[:nparts], refs[nparts:2 * nparts], refs[-1]
    for j in range(bq):
        acc = None
        for x_ref, t_ref in zip(x_refs, t_refs):
            r = _dot_dft(t_ref[j], x_ref[0, j])
            acc = r if acc is None else acc + r
        o_ref[0, :, j * c:(j + 1) * c] = acc.astype(o_ref.dtype)


def _colslot_matmul(xs, tabs, out_dtype, c=HY_WIDTH):
    tabs = [jnp.asarray(t, BF16) for t in tabs]
    g, q = xs[0].shape[0], xs[0].shape[1]
    m = tabs[0].shape[1]
    bq = COLSLOT_BQ
    xs4 = [x.reshape(g, q, -1, c) for x in xs]
    specs = [pl.BlockSpec((1, bq, x.shape[2], c), lambda gi, qi: (gi, qi, 0, 0)) for x in xs4]
    specs += [pl.BlockSpec((bq, m, t.shape[2]), lambda gi, qi: (qi, 0, 0)) for t in tabs]
    return pl.pallas_call(
        functools.partial(_colslot_body, len(xs), bq, c),
        grid=(g, q // bq), in_specs=specs,
        out_specs=pl.BlockSpec((1, m, bq * c), lambda gi, qi: (gi, 0, qi)),
        out_shape=jax.ShapeDtypeStruct((g, m, q * c), out_dtype),
        compiler_params=_cparams("arbitrary", "arbitrary"), name="dft_stride_stage",
    )(*xs4, *tabs)


def _spec_mid_body(g_ref, m3_ref, m3i_ref, k_ref, hr_ref, hi_ref):
    bp, q = hr_ref.shape[0], hr_ref.shape[1]
    c = hr_ref.shape[2] // DFT_CB
    for j in range(DFT_CB):
        kh = k_ref[j]
        kr, ki = kh[:q], kh[q:]
        for b in range(bp):
            x = _dot_dft(m3_ref[...], g_ref[b, :, j].reshape(2 * q, -1))
            xr, xi = x[:q], x[q:]
            y = jnp.concatenate([xr * kr - xi * ki, xr * ki + xi * kr], axis=0)
            h = _dot_dft(m3i_ref[...], y)
            hr_ref[b, :, j * c:(j + 1) * c] = h[:q].astype(BF16)
            hi_ref[b, :, j * c:(j + 1) * c] = h[q:].astype(BF16)


def _spec_mid(gf, m3, m3i, khat, p, q, c=HY_WIDTH):
    bp = gf.shape[0]
    g5 = gf.reshape(bp, 2, p, q, c)
    cb = DFT_CB
    return pl.pallas_call(
        _spec_mid_body,
        grid=(p // cb,),
        in_specs=[pl.BlockSpec((bp, 2, cb, q, c), lambda ci: (0, 0, ci, 0, 0)), _full((2 * q, 2 * q)),
                  _full((2 * q, 2 * q)), pl.BlockSpec((cb, 2 * q, c), lambda ci: (ci, 0, 0))],
        out_specs=[pl.BlockSpec((bp, q, cb * c), lambda ci: (0, 0, ci))] * 2,
        out_shape=[jax.ShapeDtypeStruct((bp, q, p * c), BF16)] * 2,
        compiler_params=_cparams("arbitrary"), name="dft_mid",
    )(g5, m3, m3i, khat)


def _filter_spec_body(g_ref, m3k_ref, s_ref, k_ref):
    q4 = m3k_ref.shape[1]
    inv = 1.0 / s_ref[...]
    for j in range(DFT_CB):
        k_ref[j] = _dot_dft(m3k_ref[...], g_ref[0, :, j].reshape(q4, -1)) * inv


def _filter_spec(gk, m3k, s, p, q, c=HY_WIDTH):
    g5 = gk.reshape(1, 4, p, q, c)
    cb = DFT_CB
    return pl.pallas_call(
        _filter_spec_body,
        grid=(p // cb,),
        in_specs=[pl.BlockSpec((1, 4, cb, q, c), lambda ci: (0, 0, ci, 0, 0)), _full((2 * q, 4 * q)), _full((1, c))],
        out_specs=pl.BlockSpec((cb, 2 * q, c), lambda ci: (ci, 0, 0)),
        out_shape=jax.ShapeDtypeStruct((p, 2 * q, c), F32),
        compiler_params=_cparams("arbitrary"), name="filter_spectrum",
    )(g5, m3k, s)


def _direct_conv_body(z_ref, h0_ref, h1_ref, s_ref, tf_ref, tk_ref, ti_ref, o_ref):
    n2 = tf_ref.shape[0] // 2
    kh = _dot_dft(tk_ref[...], jnp.concatenate([h0_ref[0], h1_ref[0]], axis=0)) / s_ref[...]
    x = _dot_dft(tf_ref[...], z_ref[0])
    xr, xi, kr, ki = x[:n2], x[n2:], kh[:n2], kh[n2:]
    y = jnp.concatenate([xr * kr - xi * ki, xr * ki + xi * kr], axis=0)
    o_ref[0] = _dot_dft(ti_ref[...], y)


def _direct_conv(z, h0, h1, s, n, c=HY_WIDTH):
    bp = z.shape[0] // (2 * n)
    tf, tk, ti = (jnp.asarray(t, BF16) for t in _direct_tables(n))
    out = pl.pallas_call(
        _direct_conv_body,
        grid=(bp,),
        in_specs=[pl.BlockSpec((1, 2 * n, c), lambda b: (b, 0, 0)), _full((1, n, c)), _full((1, n, c)), _full((1, c)),
                  _full(tf.shape), _full(tk.shape), _full(ti.shape)],
        out_specs=pl.BlockSpec((1, 2 * n, c), lambda b: (b, 0, 0)),
        out_shape=jax.ShapeDtypeStruct((bp, 2 * n, c), F32),
        compiler_params=_cparams("arbitrary"), name="direct_conv",
    )(z.reshape(bp, 2 * n, c), h0, h1, s, tf, tk, ti)
    return out.reshape(bp * 2 * n, c)


def _cis(m, n_total):
    ang = (2.0 * np.pi / n_total) * (m % n_total).astype(np.float64)
    return np.cos(ang), -np.sin(ang)


@functools.lru_cache(maxsize=None)
def _direct_tables(n):
    nn = 2 * n
    f = np.arange(nn)[:, None]
    t = np.arange(n)[None, :]
    cr, ci = _cis(f * t, nn)
    tf = np.block([[cr, -ci], [ci, cr]])
    tk = np.block([[cr, cr], [ci, -ci]])
    ti = np.block([[cr.T, ci.T], [-ci.T, cr.T]]) / nn
    return tuple(a.astype(np.float32) for a in (tf, tk, ti))


@functools.lru_cache(maxsize=None)
def _twostage_tables(p, q):
    nn = p * q
    hp = p // 2
    s = np.arange(q)[:, None, None]
    c = np.arange(p)[None, :, None]
    a = np.arange(hp)[None, None, :]
    tr, ti = _cis(c * (q * a + s), nn)
    z = np.zeros_like(tr)
    t_data = np.concatenate([np.concatenate([tr, -ti], 2), np.concatenate([ti, tr], 2)], 1)
    t_k0 = np.concatenate([tr, ti, z, z], 1)
    t_k1 = np.concatenate([z, z, tr, -ti], 1)
    trt, tit = np.swapaxes(tr, 1, 2) / nn, np.swapaxes(ti, 1, 2) / nn
    t_inv_r = np.concatenate([trt, -tit], 1)
    t_inv_i = np.concatenate([tit, trt], 1)
    d = np.arange(q)[:, None]
    b = np.arange(q)[None, :]
    fr, fi = _cis(d * b, q)
    m3 = np.block([[fr, -fi], [fi, fr]])
    m3i = np.block([[fr, fi], [-fi, fr]])
    m3k = np.block([[fr, -fi, fr, fi], [fi, fr, -fi, fr]])
    return tuple(x.astype(np.float32) for x in (t_data, t_k0, t_k1, t_inv_r, t_inv_i, m3, m3i, m3k))


def _long_conv_spectrum(n, lw):
    p, q = DFT_P, DFT_Q
    h0, h1, s = _hyena_filter(n, q, lw)
    _, t_k0, t_k1, _, _, _, _, m3k = _twostage_tables(p, q)
    gk = _colslot_matmul([h0, h1], [t_k0, t_k1], BF16)
    return _filter_spec(gk, jnp.asarray(m3k, BF16), s, p, q)


def _long_conv(zt, khat):
    p, q = DFT_P, DFT_Q
    t_data, _, _, t_inv_r, t_inv_i, m3, m3i, _ = _twostage_tables(p, q)
    gf = _colslot_matmul([zt], [t_data], BF16)
    hr, hi = _spec_mid(gf, jnp.asarray(m3, BF16), jnp.asarray(m3i, BF16), khat, p, q)
    y = _colslot_matmul([hr, hi], [t_inv_r, t_inv_i], F32)
    return y.reshape(-1, HY_WIDTH)


def _merge_body(tiles_per_group, fixed_group, x_ref, g1_ref, oa_ref, conv_ref, z_ref, x0_ref, gate_ref, bias_ref,
                wba_ref, wbh_ref, wout_ref, o_ref):
    group = fixed_group if fixed_group is not None else pl.program_id(0) // tiles_per_group
    o_hy = ((conv_ref[...] + z_ref[...] * bias_ref[...]) * x0_ref[...]).astype(BF16)
    d = D_MODEL
    merged = gate_ref[:, :d] * _dot(oa_ref[...], wba_ref[...]) + gate_ref[:, d:] * _dot(o_hy, wbh_ref[...])
    mix = _dot(merged.astype(BF16), wout_ref[...])
    o_ref[...] = x_ref[...] + _mod_row(g1_ref, group) * mix


def _merge(x, mod, o_attn, conv, z, x0, gates, lw, *, tiles_per_group, fixed_group):
    rows = x.shape[0]
    row_spec = lambda n: pl.BlockSpec((ROW_TILE, n), lambda t: (t, 0))
    c = HY_WIDTH
    return pl.pallas_call(
        functools.partial(_merge_body, tiles_per_group, fixed_group),
        grid=(rows // ROW_TILE,),
        in_specs=[row_spec(D_MODEL), _mod_spec(2), row_spec(N_HEADS * V_HEAD), row_spec(c), row_spec(c), row_spec(c),
                  row_spec(2 * D_MODEL), _full((1, c)), _full(lw["w_br_attn"].shape), _full(lw["w_br_hy"].shape),
                  _full(lw["w_out"].shape)],
        out_specs=row_spec(D_MODEL),
        out_shape=jax.ShapeDtypeStruct((rows, D_MODEL), F32),
        compiler_params=_cparams("arbitrary"), name="merge_out",
    )(x, mod, o_attn, conv, z, x0, gates, lw["hy_bias"], lw["w_br_attn"], lw["w_br_hy"], lw["w_out"])


def _swiglu_chunk(h, wg_ref, wu_ref, wd_ref):
    gate = _dot(h, wg_ref[0])
    up = _dot(h, wu_ref[0])
    return _dot((gate * jax.nn.sigmoid(gate) * up).astype(BF16), wd_ref[0])


def _ffn_body(tiles_per_group, fixed_group, x_ref, sh_ref, sc_ref, g2_ref, n2_ref, wg_ref, wu_ref, wd_ref, o_ref,
              h_scr, acc_scr):
    group = fixed_group if fixed_group is not None else pl.program_id(0) // tiles_per_group
    ci = pl.program_id(1)

    @pl.when(ci == 0)
    def _():
        h = _rms(x_ref[...], n2_ref[...]) * (1.0 + _mod_row(sc_ref, group)) + _mod_row(sh_ref, group)
        h_scr[...] = h.astype(BF16)
        acc_scr[...] = jnp.zeros_like(acc_scr)

    acc_scr[...] += _swiglu_chunk(h_scr[...], wg_ref, wu_ref, wd_ref)

    @pl.when(ci == pl.num_programs(1) - 1)
    def _():
        o_ref[...] = x_ref[...] + _mod_row(g2_ref, group) * acc_scr[...]


def _ffn(x, mod, norm_g, w_gate, w_up, w_down, *, tiles_per_group, fixed_group):
    rows = x.shape[0]
    tm = min(FFN_ROW_TILE, rows)
    n_chunks = D_FF // FFN_CHUNK
    row_spec = pl.BlockSpec((tm, D_MODEL), lambda t, ci: (t, 0))
    return pl.pallas_call(
        functools.partial(_ffn_body, tiles_per_group * ROW_TILE // tm if tiles_per_group else None, fixed_group),
        grid=(rows // tm, n_chunks),
        in_specs=[row_spec, _mod_spec(3), _mod_spec(4), _mod_spec(5), _full((1, D_MODEL)),
                  pl.BlockSpec((1, D_MODEL, FFN_CHUNK), lambda t, ci: (0, 0, ci)),
                  pl.BlockSpec((1, D_MODEL, FFN_CHUNK), lambda t, ci: (0, 0, ci)),
                  pl.BlockSpec((1, FFN_CHUNK, D_MODEL), lambda t, ci: (0, ci, 0))],
        out_specs=row_spec,
        out_shape=jax.ShapeDtypeStruct((rows, D_MODEL), F32),
        scratch_shapes=[pltpu.VMEM((tm, D_MODEL), BF16), pltpu.VMEM((tm, D_MODEL), F32)],
        compiler_params=_cparams("arbitrary", "arbitrary"), name="ffn",
    )(x, mod, mod, mod, norm_g, w_gate, w_up, w_down)


def _route_body(tiles_per_group, x_ref, sh_ref, sc_ref, n2_ref, router_ref, h_ref, route_ref):
    group = pl.program_id(0) // tiles_per_group
    h = _rms(x_ref[...], n2_ref[...]) * (1.0 + _mod_row(sc_ref, group)) + _mod_row(sh_ref, group)
    h_ref[...] = h
    logits = _dot_hi(h, router_ref[...])
    lane = lax.broadcasted_iota(jnp.int32, logits.shape, 1)
    logits = jnp.where(lane < N_EXPERTS, logits, -jnp.inf)
    m1 = jnp.max(logits, axis=-1, keepdims=True)
    i1 = jnp.min(jnp.where(logits == m1, lane, LANES), axis=-1, keepdims=True)
    rest = jnp.where(lane == i1, -jnp.inf, logits)
    m2 = jnp.max(rest, axis=-1, keepdims=True)
    i2 = jnp.min(jnp.where(rest == m2, lane, LANES), axis=-1, keepdims=True)
    e2 = jnp.exp(m2 - m1)
    w1 = 1.0 / (1.0 + e2)
    route_ref[...] = (jnp.where(lane == 0, i1.astype(F32), 0.0) + jnp.where(lane == 1, i2.astype(F32), 0.0)
                      + jnp.where(lane == 2, w1, 0.0) + jnp.where(lane == 3, e2 * w1, 0.0))


def _route(x, mod, norm_g, router, tiles_per_group):
    rows = x.shape[0]
    row_spec = lambda n: pl.BlockSpec((ROW_TILE, n), lambda t: (t, 0))
    return pl.pallas_call(
        functools.partial(_route_body, tiles_per_group),
        grid=(rows // ROW_TILE,),
        in_specs=[row_spec(D_MODEL), _mod_spec(3), _mod_spec(4), _full((1, D_MODEL)), _full(router.shape)],
        out_specs=[row_spec(D_MODEL), row_spec(LANES)],
        out_shape=[jax.ShapeDtypeStruct((rows, D_MODEL), F32), jax.ShapeDtypeStruct((rows, LANES), F32)],
        compiler_params=_cparams("arbitrary"), name="moe_route",
    )(x, mod, mod, norm_g, router)


def _gather_rows(idx_ref, n_rows, src_hbm, dst, sem):
    def issue(i, carry):
        for prio in range(2):
            r = 2 * i + prio
            pltpu.make_async_copy(src_hbm.at[pl.ds(idx_ref[0, 0, r], 1), :], dst.at[pl.ds(r, 1), :],
                                  sem).start(priority=prio)
        return carry
    lax.fori_loop(0, n_rows // 2, issue, 0, unroll=4)


def _wait_rows(n_rows, src_hbm, dst, sem):
    pltpu.make_async_copy(src_hbm.at[pl.ds(0, n_rows), :], dst, sem).wait()


def _moe_dispatch_body(idx_ref, h_ref, xs_in_hbm, xs_hbm, stage, sem):
    del xs_in_hbm
    t = pl.program_id(0)
    n_tiles = pl.num_programs(0)
    tm = h_ref.shape[0]
    slot = t % 2

    def wait(s):
        for _ in range(2):
            pltpu.make_async_copy(stage.at[s], xs_hbm.at[pl.ds(0, tm), :], sem.at[s]).wait()

    @pl.when(t >= 2)
    def _():
        wait(slot)

    stage[slot] = h_ref[...]

    def issue(r, carry):
        row = stage.at[slot, pl.ds(r, 1), :]
        pltpu.make_async_copy(row, xs_hbm.at[pl.ds(idx_ref[0, 0, r], 1), :], sem.at[slot]).start(priority=0)
        pltpu.make_async_copy(row, xs_hbm.at[pl.ds(idx_ref[0, 0, tm + r], 1), :], sem.at[slot]).start(priority=1)
        return carry
    lax.fori_loop(0, tm, issue, 0, unroll=4)

    @pl.when(t == n_tiles - 1)
    def _():
        wait(slot)
        wait(1 - slot)


def _tile_index_blocks(dest, tm):
    n_tiles = dest.shape[0] // tm
    return dest.reshape(n_tiles, tm, 2).transpose(0, 2, 1).reshape(n_tiles, 1, 2 * tm)


def _moe_dispatch(h, dest, n_rows):
    tm = ROW_TILE
    idx = _tile_index_blocks(dest, tm)
    n_tiles = idx.shape[0]
    assert n_tiles >= 2
    return pl.pallas_call(
        _moe_dispatch_body,
        grid=(n_tiles,),
        in_specs=[pl.BlockSpec((1, 1, 2 * tm), lambda t: (t, 0, 0), memory_space=pltpu.SMEM),
                  pl.BlockSpec((tm, D_MODEL), lambda t: (t, 0)), pl.BlockSpec(memory_space=pl.ANY)],
        out_specs=pl.BlockSpec(memory_space=pl.ANY),
        out_shape=jax.ShapeDtypeStruct((n_rows, D_MODEL), F32),
        scratch_shapes=[pltpu.VMEM((2, tm, D_MODEL), F32), pltpu.SemaphoreType.DMA((2,))],
        input_output_aliases={2: 0},
        compiler_params=_cparams("arbitrary"), name="moe_dispatch",
    )(idx, h, jnp.zeros((n_rows, D_MODEL), F32))


def _moe_group_body(te_ref, nu_ref, x_ref, wg_ref, wu_ref, wd_ref, y_ref, xb_scr, acc_scr):
    t, ci = pl.program_id(0), pl.program_id(1)

    @pl.when(ci == 0)
    def _():
        xb_scr[...] = x_ref[...].astype(BF16)
        acc_scr[...] = jnp.zeros_like(acc_scr)

    @pl.when(t < nu_ref[0])
    def _():
        acc_scr[...] += _swiglu_chunk(xb_scr[...], wg_ref, wu_ref, wd_ref)

    @pl.when(ci == pl.num_programs(1) - 1)
    def _():
        y_ref[...] = acc_scr[...]


def _moe_group(xs, tile_expert, n_used, w_gate, w_up, w_down):
    tm = MOE_ROW_TILE
    n_rows = xs.shape[0]
    n_chunks = D_FF // FFN_CHUNK

    def chunk(t, ci, nu):
        return jnp.where(t < nu[0], ci, n_chunks - 1)
    grid_spec = pltpu.PrefetchScalarGridSpec(
        num_scalar_prefetch=2,
        grid=(n_rows // tm, n_chunks),
        in_specs=[pl.BlockSpec((tm, D_MODEL), lambda t, ci, te, nu: (jnp.minimum(t, nu[0] - 1), 0)),
                  pl.BlockSpec((1, D_MODEL, FFN_CHUNK), lambda t, ci, te, nu: (te[t], 0, chunk(t, ci, nu))),
                  pl.BlockSpec((1, D_MODEL, FFN_CHUNK), lambda t, ci, te, nu: (te[t], 0, chunk(t, ci, nu))),
                  pl.BlockSpec((1, FFN_CHUNK, D_MODEL), lambda t, ci, te, nu: (te[t], chunk(t, ci, nu), 0))],
        out_specs=pl.BlockSpec((tm, D_MODEL), lambda t, ci, te, nu: (t, 0)),
        scratch_shapes=[pltpu.VMEM((tm, D_MODEL), BF16), pltpu.VMEM((tm, D_MODEL), F32)],
    )
    return pl.pallas_call(
        _moe_group_body, grid_spec=grid_spec,
        out_shape=jax.ShapeDtypeStruct((n_rows, D_MODEL), F32),
        compiler_params=_cparams("arbitrary", "arbitrary"), name="moe_group",
    )(tile_expert, n_used, xs, w_gate, w_up, w_down)


def _moe_combine_body(tiles_per_group, idx_ref, idx_next_ref, x_ref, g2_ref, fg_ref, route_ref, y_hbm, o_ref,
                      yg_scr, sem):
    t = pl.program_id(0)
    n_tiles = pl.num_programs(0)
    group = t // tiles_per_group
    tm = x_ref.shape[0]
    slot = t % 2

    @pl.when(t == 0)
    def _():
        _gather_rows(idx_ref, 2 * tm, y_hbm, yg_scr.at[0], sem.at[0])

    _wait_rows(2 * tm, y_hbm, yg_scr.at[slot], sem.at[slot])

    @pl.when(t + 1 < n_tiles)
    def _():
        _gather_rows(idx_next_ref, 2 * tm, y_hbm, yg_scr.at[1 - slot], sem.at[1 - slot])

    route = route_ref[...]
    y = route[:, 2:3] * yg_scr[slot, :tm] + route[:, 3:4] * yg_scr[slot, tm:]
    o_ref[...] = _rms(x_ref[...] + _mod_row(g2_ref, group) * y, fg_ref[...])


def _moe_combine(x, mod, final_g, route, y_sorted, dest, tiles_per_group):
    rows = x.shape[0]
    tm = ROW_TILE
    idx = _tile_index_blocks(dest, tm)
    n_tiles = idx.shape[0]
    return pl.pallas_call(
        functools.partial(_moe_combine_body, tiles_per_group),
        grid=(n_tiles,),
        in_specs=[pl.BlockSpec((1, 1, 2 * tm), lambda t: (t, 0, 0), memory_space=pltpu.SMEM),
                  pl.BlockSpec((1, 1, 2 * tm), lambda t: (jnp.minimum(t + 1, n_tiles - 1), 0, 0),
                               memory_space=pltpu.SMEM),
                  pl.BlockSpec((tm, D_MODEL), lambda t: (t, 0)), _mod_spec(5), _full((1, D_MODEL)),
                  pl.BlockSpec((tm, LANES), lambda t: (t, 0)), pl.BlockSpec(memory_space=pl.ANY)],
        out_specs=pl.BlockSpec((tm, D_MODEL), lambda t: (t, 0)),
        out_shape=jax.ShapeDtypeStruct((rows, D_MODEL), F32),
        scratch_shapes=[pltpu.VMEM((2, 2 * tm, D_MODEL), F32), pltpu.SemaphoreType.DMA((2,))],
        compiler_params=_cparams("arbitrary"), name="moe_combine",
    )(idx, idx, x, mod, final_g, route, y_sorted)


def _moe_plan(route, tm):
    t = route.shape[0]
    experts = jnp.concatenate([route[:, 0], route[:, 1]]).astype(jnp.int32)
    onehot = (experts[:, None] == jnp.arange(N_EXPERTS, dtype=jnp.int32)[None, :]).astype(jnp.int32)
    csum = jnp.cumsum(onehot, axis=0)
    rank = jnp.sum(csum * onehot, axis=1) - 1
    padded = (csum[-1] + tm - 1) // tm * tm
    ends = jnp.cumsum(padded)
    dest = jnp.sum((ends - padded)[None, :] * onehot, axis=1) + rank
    n_rows = 2 * t + N_EXPERTS * tm
    tile_start = jnp.arange(n_rows // tm, dtype=jnp.int32) * tm
    tile_expert = jnp.minimum(jnp.sum((tile_start[:, None] >= ends[None, :]).astype(jnp.int32), axis=1), N_EXPERTS - 1)
    n_used = (ends[-1:] // tm).astype(jnp.int32)
    return dest.reshape(2, t).T, tile_expert, n_used, n_rows


def _moe(x, mod, norm_g, router, w_gate, w_up, w_down, final_g, tiles_per_group):
    h, route = _route(x, mod, norm_g, router, tiles_per_group)
    dest, tile_expert, n_used, n_rows = _moe_plan(route, MOE_ROW_TILE)
    xs = _moe_dispatch(h, dest, n_rows)
    y_sorted = _moe_group(xs, tile_expert, n_used, w_gate, w_up, w_down)
    return _moe_combine(x, mod, final_g, route, y_sorted, dest, tiles_per_group)


def _pad_heads(w, width):
    k = w.shape[0]
    w = w.reshape(k, N_HEADS, width)
    return jnp.pad(w, ((0, 0), (0, 0), (0, HEAD_PAD - width))).reshape(k, N_HEADS * HEAD_PAD)


def _rot_cols(w):
    half = QK_ROPE // 2
    return jnp.concatenate([-w[..., half:], w[..., :half]], axis=-1)


def _layer_weights(p, layer):
    w_in = p["w_in"][layer]
    w_uq = p["w_uq"][layer].reshape(Q_LORA, N_HEADS, QK_NOPE + QK_ROPE)
    zeros_nope = jnp.zeros((Q_LORA, N_HEADS, QK_NOPE), F32)
    w_uq_b = jnp.concatenate([zeros_nope, _rot_cols(w_uq[..., QK_NOPE:])], axis=-1)
    w_kr = w_in[:, KV_START + KV_LORA:HY_START]
    zk = jnp.zeros((D_MODEL, QK_NOPE), F32)
    zp = jnp.zeros((D_MODEL, HEAD_PAD - QK_NOPE - QK_ROPE), F32)
    w_kr2 = jnp.concatenate([zk, w_kr, zp, zk, _rot_cols(w_kr), zp], axis=-1)
    row = lambda v: v.reshape(1, -1)
    bf = lambda v: v.astype(BF16)
    return {
        "norm1_g": row(p["norm1_g"][layer]), "norm2_g": row(p["norm2_g"][layer]),
        "w_q": bf(w_in[:, :Q_LORA]), "q_norm_g": row(p["q_norm_g"][layer]),
        "w_uq_a": bf(_pad_heads(w_uq.reshape(Q_LORA, -1), QK_NOPE + QK_ROPE)),
        "w_uq_b": bf(_pad_heads(w_uq_b.reshape(Q_LORA, -1), QK_NOPE + QK_ROPE)),
        "w_kv": bf(w_in[:, KV_START:KV_START + KV_LORA]), "kv_norm_g": row(p["kv_norm_g"][layer]),
        "w_uk": bf(_pad_heads(p["w_uk"][layer], QK_NOPE)), "w_uv": bf(_pad_heads(p["w_uv"][layer], V_HEAD).T),
        "w_kr": bf(w_kr2),
        "w_hy": bf(w_in[:, HY_START:GATE_START]), "w_gate": bf(w_in[:, GATE_START:]),
        "hy_short_w": p["hy_short_w"][layer], "hy_short_b": p["hy_short_b"][layer],
        "hy_w1": p["hy_w1"][layer], "hy_b1": p["hy_b1"][layer], "hy_w2": p["hy_w2"][layer],
        "hy_b2": p["hy_b2"][layer], "hy_w3": p["hy_w3"][layer], "hy_freq": p["hy_freq"][layer],
        "hy_decay": p["hy_decay"][layer], "hy_bias": row(p["hy_bias"][layer]),
        "w_br_attn": bf(p["w_br_attn"][layer]), "w_br_hy": bf(p["w_br_hy"][layer]), "w_out": bf(p["w_out"][layer]),
    }


def _rope_tables(n):
    rows = n // GRID_W
    n_freq = QK_ROPE // 4
    inv_freq = ROPE_BASE ** (-jnp.arange(n_freq, dtype=F32) / n_freq)
    r = jnp.repeat(jnp.arange(rows, dtype=F32), GRID_W)
    col = jnp.tile(jnp.arange(GRID_W, dtype=F32), rows)
    ang = jnp.concatenate([r[:, None] * inv_freq, col[:, None] * inv_freq], axis=-1)
    cos, sin = jnp.cos(ang), jnp.sin(ang)
    ones = jnp.ones((n, QK_NOPE), F32)
    zeros = jnp.zeros((n, QK_NOPE), F32)
    pad = jnp.zeros((n, HEAD_PAD - QK_NOPE - QK_ROPE), F32)
    return (jnp.concatenate([ones, cos, cos, pad], axis=-1), jnp.concatenate([zeros, sin, sin, pad], axis=-1))


def _token_mixer(xs, mod, lw, khat, conv_short, *, batch, n, tiles_per_group, fixed_group, rope_tabs,
                 k_ctx=None, v_ctx=None):
    pr = _inproj(xs, mod, lw, tiles_per_group=tiles_per_group, fixed_group=fixed_group, rope_tabs=rope_tabs,
                 want_q=True, want_hg=True)
    k3 = pr["k"].reshape(batch, n, -1)
    v3 = pr["v"].reshape(batch, n // VT_BLOCK, N_HEADS * HEAD_PAD, VT_BLOCK)
    if k_ctx is None:
        o_attn = _attention(pr["q"], k3, v3, None, None, batch)
    else:
        o_attn = _attention(pr["q"], k_ctx, v_ctx, k3, v3, batch)
    if conv_short:
        z, x0 = _hyena_pre(pr["p_hy"], lw["hy_short_w"], lw["hy_short_b"], batch, None)
        conv = _direct_conv(z, *khat, n)
    else:
        z, x0, zt = _hyena_pre(pr["p_hy"], lw["hy_short_w"], lw["hy_short_b"], batch, DFT_Q)
        conv = _long_conv(zt, khat)
    x_new = _merge(xs, mod, o_attn, conv, z, x0, pr["gates"], lw, tiles_per_group=tiles_per_group,
                   fixed_group=fixed_group)
    return x_new, k3, v3


def kernel(x, c, ctx, c_ctx, w_mod, b_mod, norm1_g, norm2_g, w_in, q_norm_g, kv_norm_g, w_uq, w_uk, w_uv, hy_short_w, hy_short_b, hy_w1, hy_b1, hy_w2, hy_b2, hy_w3, hy_freq, hy_decay, hy_bias, w_br_attn, w_br_hy, w_out, ffn_w_gate, ffn_w_up, ffn_w_down, moe_router, moe_w_gate, moe_w_up, moe_w_down, final_g):
    p = dict(w_in=w_in, norm1_g=norm1_g, norm2_g=norm2_g, q_norm_g=q_norm_g, kv_norm_g=kv_norm_g, w_uq=w_uq,
             w_uk=w_uk, w_uv=w_uv, hy_short_w=hy_short_w, hy_short_b=hy_short_b, hy_w1=hy_w1, hy_b1=hy_b1,
             hy_w2=hy_w2, hy_b2=hy_b2, hy_w3=hy_w3, hy_freq=hy_freq, hy_decay=hy_decay, hy_bias=hy_bias,
             w_br_attn=w_br_attn, w_br_hy=w_br_hy, w_out=w_out)
    batch, seq, d = x.shape
    ctx_len = ctx.shape[1]
    depth = w_mod.shape[0]
    ctx_group = batch
    cond8 = jnp.zeros((SUBLANES, d), F32).at[:batch].set(c).at[ctx_group].set(c_ctx)
    rope_tabs = _rope_tables(seq)
    lat_tiles = seq // ROW_TILE
    xs = x.reshape(batch * seq, d)
    cs = ctx.reshape(batch * ctx_len, d)
    bf = lambda v: v.astype(BF16)
    for layer in range(depth):
        last = layer == depth - 1
        lw = _layer_weights(p, layer)
        mod = _adaln(cond8, w_mod[layer], b_mod[layer])
        khat_lat = _long_conv_spectrum(seq, lw)
        if last:
            pr = _inproj(cs, mod, lw, tiles_per_group=None, fixed_group=ctx_group, rope_tabs=None, want_q=False,
                         want_hg=False)
            k_ctx = pr["k"].reshape(batch, ctx_len, -1)
            v_ctx = pr["v"].reshape(batch, ctx_len // VT_BLOCK, N_HEADS * HEAD_PAD, VT_BLOCK)
        else:
            khat_ctx = _hyena_filter(ctx_len, ctx_len, lw)
            cs_mid, k_ctx, v_ctx = _token_mixer(cs, mod, lw, khat_ctx, True, batch=batch, n=ctx_len,
                                                tiles_per_group=None, fixed_group=ctx_group, rope_tabs=None)
        xs, _, _ = _token_mixer(xs, mod, lw, khat_lat, False, batch=batch, n=seq, tiles_per_group=lat_tiles,
                                fixed_group=None, rope_tabs=rope_tabs, k_ctx=k_ctx, v_ctx=v_ctx)
        i = layer // 2
        n2 = lw["norm2_g"]
        if layer % 2 == 0:
            assert not last
            wg, wu, wd = bf(ffn_w_gate[i])[None], bf(ffn_w_up[i])[None], bf(ffn_w_down[i])[None]
            xs = _ffn(xs, mod, n2, wg, wu, wd, tiles_per_group=lat_tiles, fixed_group=None)
            cs = _ffn(cs_mid, mod, n2, wg, wu, wd, tiles_per_group=None, fixed_group=ctx_group)
        else:
            assert last
            router = jnp.pad(moe_router[i], ((0, 0), (0, LANES - N_EXPERTS)))
            xs = _moe(xs, mod, n2, router, bf(moe_w_gate[i]), bf(moe_w_up[i]), bf(moe_w_down[i]),
                      final_g.reshape(1, d), lat_tiles)
    return xs.reshape(batch, seq, d)
```

```python
import functools
import math

import numpy as np
import jax
import jax.numpy as jnp
from jax import lax
from jax.experimental import pallas as pl
from jax.experimental.pallas import tpu as pltpu

F32 = jnp.float32
BF16 = jnp.bfloat16
HIGHEST = lax.Precision.HIGHEST

D_MODEL = 1024
GRID_W = 64
EPS = 1e-6
N_HEADS = 8
Q_LORA = 384
KV_LORA = 256
QK_NOPE = 64
QK_ROPE = 32
V_HEAD = 64
ROPE_BASE = 10000.0
ATTN_SCALE = (QK_NOPE + QK_ROPE) ** -0.5
HY_WIDTH = 512
HY_EMB = 17
HY_HIDDEN = 64
D_FF = 2816
N_EXPERTS = 8
KV_START = Q_LORA
HY_START = KV_START + KV_LORA + QK_ROPE
GATE_START = HY_START + 3 * HY_WIDTH

LANES = 128
SUBLANES = 8
HALO_ROWS = 16
HEAD_PAD = LANES
VMEM_LIMIT = 56 * 2**20

ROW_TILE = 256
INPROJ_ROW_TILE = 512
FFN_ROW_TILE = 512
MXU_TILE = 256
FFN_SPANS = ((0, 4 * MXU_TILE), (4 * MXU_TILE, 8 * MXU_TILE), (8 * MXU_TILE, D_FF))
MOE_ROW_TILE = 256
VT_BLOCK = ROW_TILE
ATTN_KV = 512
ATTN_Q_TILE = 1024
Q_SCALE = ATTN_SCALE * math.log2(math.e)
DFT_P = 64
DFT_Q = 128
COLSLOT_BQ = 16
DFT_CB = 4


def _cparams(*sem):
    return pltpu.CompilerParams(dimension_semantics=sem, vmem_limit_bytes=VMEM_LIMIT)


def _dot(a, b):
    return jnp.dot(a, b, preferred_element_type=F32)


def _dot_hi(a, b):
    return jnp.dot(a, b, precision=HIGHEST, preferred_element_type=F32)


def _dot_dft(table, x):
    return _dot(table, x.astype(BF16))


def _rms(xf, g):
    return xf * lax.rsqrt(jnp.mean(xf * xf, axis=-1, keepdims=True) + EPS) * g


def _full(shape):
    nd = len(shape)
    return pl.BlockSpec(shape, lambda *_: (0,) * nd, pipeline_mode=pl.Buffered(1))


def _adaln_body(c_ref, w_ref, b_ref, o_ref):
    c = c_ref[...]
    o_ref[...] = _dot_hi(c * jax.nn.sigmoid(c), w_ref[...]) + b_ref[...]


def _adaln(cond8, w, b):
    d, n = w.shape
    return pl.pallas_call(
        _adaln_body,
        grid=(n // d,),
        in_specs=[_full((SUBLANES, d)), pl.BlockSpec((d, d), lambda j: (0, j)),
                  pl.BlockSpec((1, d), lambda j: (0, j))],
        out_specs=pl.BlockSpec((SUBLANES, d), lambda j: (0, j)),
        out_shape=jax.ShapeDtypeStruct((SUBLANES, n), F32),
        compiler_params=_cparams("arbitrary"),
        name="adaln",
    )(cond8, w, b.reshape(1, n))


def _mod_spec(chunk):
    return pl.BlockSpec((SUBLANES, D_MODEL), lambda t, *_: (0, chunk))


def _mod_row(ref, group):
    return ref[pl.ds(group, 1), :]


def _inproj_body(tiles_per_group, fixed_group, use_rope, want_q, want_hg, *refs):
    it = iter(refs)
    x_ref, sh_ref, sc_ref, g1_ref = next(it), next(it), next(it), next(it)
    wkv_ref, kvg_ref, wuk_ref, wuv_ref, wkr_ref = next(it), next(it), next(it), next(it), next(it)
    if want_q:
        wq_ref, qg_ref, wuqa_ref, wuqb_ref = next(it), next(it), next(it), next(it)
    if want_hg:
        why_ref, wgate_ref = next(it), next(it)
    if use_rope:
        cos_ref, sin_ref = next(it), next(it)
    k_out, v_out = next(it), next(it)
    if want_q:
        q_out = next(it)
    if want_hg:
        phy_out, gate_out = next(it), next(it)

    group = fixed_group if fixed_group is not None else pl.program_id(0) // tiles_per_group
    xf = x_ref[...]
    h = _rms(xf, g1_ref[...]) * (1.0 + _mod_row(sc_ref, group)) + _mod_row(sh_ref, group)
    h = h.astype(BF16)
    if use_rope:
        cos, sin = cos_ref[...], sin_ref[...]

    def rope(a, b):
        return a * cos + b * sin if use_rope else a

    ckv = _rms(_dot(h, wkv_ref[...]), kvg_ref[...]).astype(BF16)
    vt = lax.dot_general(wuv_ref[...], ckv, (((1,), (1,)), ((), ())), preferred_element_type=F32)
    vrow = lax.broadcasted_iota(jnp.int32, vt.shape, 0) & (HEAD_PAD - 1)
    vt = jnp.where(vrow == V_HEAD, 1.0, vt).astype(BF16)
    for j in range(v_out.shape[0]):
        v_out[j] = vt[:, j * VT_BLOCK:(j + 1) * VT_BLOCK]
    k_nope = _dot(ckv, wuk_ref[...])
    kr = _dot(h, wkr_ref[...])
    k_rope = rope(kr[:, :HEAD_PAD], kr[:, HEAD_PAD:])
    for hd in range(N_HEADS):
        sl = slice(hd * HEAD_PAD, (hd + 1) * HEAD_PAD)
        k_out[:, sl] = (k_nope[:, sl] + k_rope).astype(BF16)
    if want_q:
        qn = _rms(_dot(h, wq_ref[...]), qg_ref[...]).astype(BF16)
        qa = _dot(qn, wuqa_ref[...])
        qb = _dot(qn, wuqb_ref[...]) if use_rope else None
        for hd in range(N_HEADS):
            sl = slice(hd * HEAD_PAD, (hd + 1) * HEAD_PAD)
            q_out[:, sl] = (rope(qa[:, sl], None if qb is None else qb[:, sl]) * Q_SCALE).astype(BF16)
    if want_hg:
        n_hy = why_ref.shape[1]
        for c0 in range(0, n_hy, 512):
            phy_out[:, c0:c0 + 512] = _dot(h, why_ref[:, c0:c0 + 512]).astype(BF16)
        n_g = wgate_ref.shape[1]
        for c0 in range(0, n_g, 512):
            gate_out[:, c0:c0 + 512] = jax.nn.sigmoid(_dot(h, wgate_ref[:, c0:c0 + 512])).astype(BF16)


def _inproj(x, mod, lw, *, tiles_per_group, fixed_group, rope_tabs, want_q, want_hg):
    rows = x.shape[0]
    tm = INPROJ_ROW_TILE
    nt = rows // tm
    if tiles_per_group is not None:
        tiles_per_group = tiles_per_group * ROW_TILE // tm
    use_rope = rope_tabs is not None
    row_spec = lambda n: pl.BlockSpec((tm, n), lambda t: (t, 0))
    ins = [x, mod, mod, lw["norm1_g"], lw["w_kv"], lw["kv_norm_g"], lw["w_uk"], lw["w_uv"], lw["w_kr"]]
    specs = [row_spec(D_MODEL), _mod_spec(0), _mod_spec(1), _full((1, D_MODEL)),
             _full(lw["w_kv"].shape), _full((1, KV_LORA)), _full(lw["w_uk"].shape), _full(lw["w_uv"].shape),
             _full(lw["w_kr"].shape)]
    if want_q:
        ins += [lw["w_q"], lw["q_norm_g"], lw["w_uq_a"], lw["w_uq_b"]]
        specs += [_full(lw["w_q"].shape), _full((1, Q_LORA)), _full(lw["w_uq_a"].shape), _full(lw["w_uq_b"].shape)]
    if want_hg:
        ins += [lw["w_hy"], lw["w_gate"]]
        specs += [_full(lw["w_hy"].shape), _full(lw["w_gate"].shape)]
    if use_rope:
        seq_tiles = rope_tabs[0].shape[0] // tm
        ins += list(rope_tabs)
        specs += [pl.BlockSpec((tm, HEAD_PAD), lambda t: (t % seq_tiles, 0))] * 2
    hp = N_HEADS * HEAD_PAD
    vt_blocks = tm // VT_BLOCK
    out_shape = [jax.ShapeDtypeStruct((rows, hp), BF16), jax.ShapeDtypeStruct((rows // VT_BLOCK, hp, VT_BLOCK), BF16)]
    out_specs = [row_spec(hp), pl.BlockSpec((vt_blocks, hp, VT_BLOCK), lambda t: (t, 0, 0))]
    if want_q:
        out_shape.append(jax.ShapeDtypeStruct((rows, hp), BF16))
        out_specs.append(row_spec(hp))
    if want_hg:
        out_shape += [jax.ShapeDtypeStruct((rows, 3 * HY_WIDTH), BF16), jax.ShapeDtypeStruct((rows, 2 * D_MODEL), BF16)]
        out_specs += [row_spec(3 * HY_WIDTH), row_spec(2 * D_MODEL)]
    outs = pl.pallas_call(
        functools.partial(_inproj_body, tiles_per_group, fixed_group, use_rope, want_q, want_hg),
        grid=(nt,), in_specs=specs, out_specs=out_specs, out_shape=out_shape,
        compiler_params=_cparams("arbitrary"), name="inproj",
    )(*ins)
    res = {"k": outs[0], "v": outs[1]}
    i = 2
    if want_q:
        res["q"] = outs[i]
        i += 1
    if want_hg:
        res["p_hy"], res["gates"] = outs[i], outs[i + 1]
    return res


def _attn_body(n_lat_blocks, *refs):
    if n_lat_blocks:
        q_ref, kc_ref, vc_ref, kl_ref, vl_ref, o_ref = refs
    else:
        q_ref, kc_ref, vc_ref, o_ref = refs
    tq = q_ref.shape[0]
    q = q_ref[...]
    heads = tuple(slice(hd * HEAD_PAD, (hd + 1) * HEAD_PAD) for hd in range(2))
    sub = ATTN_KV // VT_BLOCK

    def scores(kblk):
        return tuple(lax.dot_general(kblk[:, sl], q[:, sl], (((1,), (1,)), ((), ())), preferred_element_type=F32)
                     for sl in heads)

    def update(st, vt_blocks, carry):
        m_new = [jnp.maximum(carry[hd][0], jnp.max(st[hd], axis=0, keepdims=True)) for hd in range(2)]
        p = [jnp.exp2(st[hd] - m_new[hd]).astype(BF16) for hd in range(2)]
        out = []
        for hd in range(2):
            m, acc = carry[hd]
            pv = None
            for j, vt in enumerate(vt_blocks):
                r = _dot(vt[heads[hd], :], p[hd][j * VT_BLOCK:(j + 1) * VT_BLOCK])
                pv = r if pv is None else pv + r
            out.append((m_new[hd], jnp.exp2(m - m_new[hd]) * acc + pv))
        return tuple(out)

    init = tuple((jnp.full((1, tq), -jnp.inf, F32), jnp.zeros((HEAD_PAD, tq), F32)) for _ in range(2))
    carry = update(scores(kc_ref[0]), [vc_ref[0, 0]], init)
    if n_lat_blocks:
        def k_block(i):
            return kl_ref[0, pl.ds(pl.multiple_of(i * ATTN_KV, ATTN_KV), ATTN_KV), :]

        def v_blocks(i):
            return [vl_ref[0, i * sub + j] for j in range(sub)]

        def body(i, c):
            st, carry = c
            st_next = scores(k_block(i + 1))
            return st_next, update(st, v_blocks(i), carry)
        st, carry = lax.fori_loop(0, n_lat_blocks - 1, body, (scores(k_block(0)), carry), unroll=True)
        carry = update(st, v_blocks(n_lat_blocks - 1), carry)
    o_t = jnp.concatenate([acc[:V_HEAD] / acc[V_HEAD:V_HEAD + 1] for _, acc in carry], axis=0)
    o_ref[...] = o_t.T.astype(BF16)


def _attention(q, k_ctx, vt_ctx, k_lat, vt_lat, batch):
    rows = q.shape[0]
    tq = min(ATTN_Q_TILE, rows // batch)
    qt_per_b = rows // batch // tq
    ctx_len = k_ctx.shape[1]
    assert ctx_len == VT_BLOCK
    pair_w = 2 * HEAD_PAD
    ins = [q, k_ctx, vt_ctx]
    specs = [pl.BlockSpec((tq, pair_w), lambda b, hp, t: (b * qt_per_b + t, hp)),
             pl.BlockSpec((1, ctx_len, pair_w), lambda b, hp, t: (b, 0, hp)),
             pl.BlockSpec((1, 1, pair_w, VT_BLOCK), lambda b, hp, t: (b, 0, hp, 0))]
    n_lat_blocks = 0
    if k_lat is not None:
        lat_len = k_lat.shape[1]
        n_lat_blocks = lat_len // ATTN_KV
        ins += [k_lat, vt_lat]
        specs += [pl.BlockSpec((1, lat_len, pair_w), lambda b, hp, t: (b, 0, hp)),
                  pl.BlockSpec((1, lat_len // VT_BLOCK, pair_w, VT_BLOCK), lambda b, hp, t: (b, 0, hp, 0))]
    return pl.pallas_call(
        functools.partial(_attn_body, n_lat_blocks),
        grid=(batch, N_HEADS // 2, qt_per_b), in_specs=specs,
        out_specs=pl.BlockSpec((tq, 2 * V_HEAD), lambda b, hp, t: (b * qt_per_b + t, hp)),
        out_shape=jax.ShapeDtypeStruct((rows, N_HEADS * V_HEAD), BF16),
        compiler_params=_cparams("arbitrary", "arbitrary", "arbitrary"), name="attention",
    )(*ins)


def _hypre_body(seq_tiles, n_chunks, p_ref, prev_ref, next_ref, w_ref, b_ref, z_ref, x0_ref, *zt_ref):
    j = pl.program_id(0) % seq_tiles
    p = p_ref[...].astype(F32)
    tm = p.shape[0]
    row = lax.broadcasted_iota(jnp.int32, (tm, 1), 0)
    prev_row = jnp.where(j != 0, prev_ref[...].astype(F32)[HALO_ROWS - 1:HALO_ROWS, :], 0.0)
    next_row = jnp.where(j != seq_tiles - 1, next_ref[...].astype(F32)[0:1, :], 0.0)
    up = jnp.where(row == 0, prev_row, pltpu.roll(p, 1, 0))
    dn = jnp.where(row == tm - 1, next_row, pltpu.roll(p, tm - 1, 0))
    u = up * w_ref[0:1, :] + p * w_ref[1:2, :] + dn * w_ref[2:3, :] + b_ref[...]
    c = HY_WIDTH
    z = u[:, :c] * u[:, c:2 * c]
    z_ref[...] = z.astype(BF16)
    x0_ref[...] = u[:, 2 * c:].astype(BF16)
    if zt_ref:
        q = tm // n_chunks
        for a in range(n_chunks):
            zt_ref[0][0, :, a * c:(a + 1) * c] = z[a * q:(a + 1) * q, :].astype(BF16)


def _hyena_pre(p_hy, short_w, short_b, batch, dft_q):
    rows = p_hy.shape[0]
    n = rows // batch
    seq_tiles = n // ROW_TILE
    nt = rows // ROW_TILE
    c3 = 3 * HY_WIDTH
    halo = ROW_TILE // HALO_ROWS
    last_halo = rows // HALO_ROWS - 1
    specs = [pl.BlockSpec((ROW_TILE, c3), lambda t: (t, 0)),
             pl.BlockSpec((HALO_ROWS, c3), lambda t: (jnp.maximum(t * halo - 1, 0), 0)),
             pl.BlockSpec((HALO_ROWS, c3), lambda t: (jnp.minimum((t + 1) * halo, last_halo), 0)),
             _full((3, c3)), _full((1, c3))]
    out_shape = [jax.ShapeDtypeStruct((rows, HY_WIDTH), BF16)] * 2
    out_specs = [pl.BlockSpec((ROW_TILE, HY_WIDTH), lambda t: (t, 0))] * 2
    n_chunks = 1
    if dft_q is not None:
        n_chunks = ROW_TILE // dft_q
        half_p = n // dft_q
        out_shape.append(jax.ShapeDtypeStruct((batch // 2, dft_q, 2 * half_p * HY_WIDTH), BF16))

        def zt_map(t):
            b, jt = t // seq_tiles, t % seq_tiles
            return (b // 2, 0, (b % 2) * seq_tiles + jt)
        out_specs.append(pl.BlockSpec((1, dft_q, n_chunks * HY_WIDTH), zt_map))
    return pl.pallas_call(
        functools.partial(_hypre_body, seq_tiles, n_chunks),
        grid=(nt,), in_specs=specs, out_specs=out_specs, out_shape=out_shape,
        compiler_params=_cparams("arbitrary"), name="hyena_pre",
    )(p_hy, p_hy, p_hy, short_w, short_b.reshape(1, c3))


def _filter_body(emb_ref, w1_ref, b1_ref, w2_ref, b2_ref, w3_ref, freq_ref, decay_ref, h0_ref, h1_ref, s_ref):
    a = pl.program_id(0)
    emb = emb_ref[...]
    freq = freq_ref[...]
    h = jnp.sin(freq * (_dot_hi(emb, w1_ref[...]) + b1_ref[...]))
    h = jnp.sin(freq * (_dot_hi(h, w2_ref[...]) + b2_ref[...]))
    h = _dot_hi(h, w3_ref[...]) * jnp.exp(-emb[:, 0:1] * jnp.abs(decay_ref[...]))
    c = HY_WIDTH
    row = lax.broadcasted_iota(jnp.int32, (emb.shape[0], 1), 0)
    h0 = h[:, :c]
    h1 = jnp.where(jnp.logical_and(a == 0, row == 0), 0.0, h[:, c:])
    h0_ref[0] = h0.astype(BF16)
    h1_ref[0] = h1.astype(BF16)
    part = jnp.sum(jnp.abs(h0) + jnp.abs(h1), axis=0, keepdims=True)

    @pl.when(a == 0)
    def _():
        s_ref[...] = jnp.zeros_like(s_ref)
    s_ref[...] += part


def _hyena_filter(n, q, lw):
    f32 = F32
    bands = (HY_EMB - 1) // 2
    t = jnp.linspace(0.0, 1.0, n, dtype=f32)[:, None]
    phase = (2.0 * math.pi / n) * jnp.arange(n, dtype=f32)[:, None] * jnp.linspace(1e-4, bands - 1, bands, dtype=f32)
    emb = jnp.concatenate([t, jnp.cos(phase), -jnp.sin(phase), jnp.zeros((n, 32 - HY_EMB), f32)], axis=-1)
    w1 = jnp.concatenate([lw["hy_w1"], jnp.zeros((32 - HY_EMB, HY_HIDDEN), f32)], axis=0)
    c = HY_WIDTH
    slots = n // q
    row = lambda v: v.reshape(1, -1)
    return pl.pallas_call(
        _filter_body,
        grid=(slots,),
        in_specs=[pl.BlockSpec((q, 32), lambda a: (a, 0)), _full((32, HY_HIDDEN)), _full((1, HY_HIDDEN)),
                  _full((HY_HIDDEN, HY_HIDDEN)), _full((1, HY_HIDDEN)), _full((HY_HIDDEN, 2 * c)),
                  _full((1, HY_HIDDEN)), _full((1, 2 * c))],
        out_specs=[pl.BlockSpec((1, q, c), lambda a: (0, 0, a)), pl.BlockSpec((1, q, c), lambda a: (0, 0, a)),
                   pl.BlockSpec((1, c), lambda a: (0, 0))],
        out_shape=[jax.ShapeDtypeStruct((1, q, slots * c), BF16)] * 2 + [jax.ShapeDtypeStruct((1, c), f32)],
        compiler_params=_cparams("arbitrary"), name="hyena_filter",
    )(emb, w1, row(lw["hy_b1"]), lw["hy_w2"], row(lw["hy_b2"]), lw["hy_w3"], row(lw["hy_freq"]),
      row(lw["hy_decay"]))


def _colslot_body(nparts, bq, c, *refs):
    x_refs, t_refs, o_ref = refs[:nparts], refs[nparts:2 * nparts], refs[-1]
    for j in range(bq):
        acc = None
        for x_ref, t_ref in zip(x_refs, t_refs):
            r = _dot_dft(t_ref[j], x_ref[0, j])
            acc = r if acc is None else acc + r
        o_ref[0, :, j * c:(j + 1) * c] = acc.astype(o_ref.dtype)


def _colslot_matmul(xs, tabs, out_dtype, c=HY_WIDTH):
    tabs = [jnp.asarray(t, BF16) for t in tabs]
    g, q = xs[0].shape[0], xs[0].shape[1]
    m = tabs[0].shape[1]
    bq = COLSLOT_BQ
    xs4 = [x.reshape(g, q, -1, c) for x in xs]
    specs = [pl.BlockSpec((1, bq, x.shape[2], c), lambda gi, qi: (gi, qi, 0, 0)) for x in xs4]
    specs += [pl.BlockSpec((bq, m, t.shape[2]), lambda gi, qi: (qi, 0, 0)) for t in tabs]
    return pl.pallas_call(
        functools.partial(_colslot_body, len(xs), bq, c),
        grid=(g, q // bq), in_specs=specs,
        out_specs=pl.BlockSpec((1, m, bq * c), lambda gi, qi: (gi, 0, qi)),
        out_shape=jax.ShapeDtypeStruct((g, m, q * c), out_dtype),
        compiler_params=_cparams("arbitrary", "arbitrary"), name="dft_stride_stage",
    )(*xs4, *tabs)


def _spec_mid_body(g_ref, m3_ref, m3i_ref, k_ref, hr_ref, hi_ref):
    bp, q = hr_ref.shape[0], hr_ref.shape[1]
    c = hr_ref.shape[2] // DFT_CB
    for j in range(DFT_CB):
        kh = k_ref[j]
        kr, ki = kh[:q], kh[q:]
        for b in range(bp):
            x = _dot_dft(m3_ref[...], g_ref[b, :, j].reshape(2 * q, -1))
            xr, xi = x[:q], x[q:]
            y = jnp.concatenate([xr * kr - xi * ki, xr * ki + xi * kr], axis=0)
            h = _dot_dft(m3i_ref[...], y)
            hr_ref[b, :, j * c:(j + 1) * c] = h[:q].astype(BF16)
            hi_ref[b, :, j * c:(j + 1) * c] = h[q:].astype(BF16)


def _spec_mid(gf, m3, m3i, khat, p, q, c=HY_WIDTH):
    bp = gf.shape[0]
    g5 = gf.reshape(bp, 2, p, q, c)
    cb = DFT_CB
    return pl.pallas_call(
        _spec_mid_body,
        grid=(p // cb,),
        in_specs=[pl.BlockSpec((bp, 2, cb, q, c), lambda ci: (0, 0, ci, 0, 0)), _full((2 * q, 2 * q)),
                  _full((2 * q, 2 * q)), pl.BlockSpec((cb, 2 * q, c), lambda ci: (ci, 0, 0))],
        out_specs=[pl.BlockSpec((bp, q, cb * c), lambda ci: (0, 0, ci))] * 2,
        out_shape=[jax.ShapeDtypeStruct((bp, q, p * c), BF16)] * 2,
        compiler_params=_cparams("arbitrary"), name="dft_mid",
    )(g5, m3, m3i, khat)


def _filter_spec_body(g_ref, m3k_ref, s_ref, k_ref):
    q4 = m3k_ref.shape[1]
    inv = 1.0 / s_ref[...]
    for j in range(DFT_CB):
        k_ref[j] = _dot_dft(m3k_ref[...], g_ref[0, :, j].reshape(q4, -1)) * inv


def _filter_spec(gk, m3k, s, p, q, c=HY_WIDTH):
    g5 = gk.reshape(1, 4, p, q, c)
    cb = DFT_CB
    return pl.pallas_call(
        _filter_spec_body,
        grid=(p // cb,),
        in_specs=[pl.BlockSpec((1, 4, cb, q, c), lambda ci: (0, 0, ci, 0, 0)), _full((2 * q, 4 * q)), _full((1, c))],
        out_specs=pl.BlockSpec((cb, 2 * q, c), lambda ci: (ci, 0, 0)),
        out_shape=jax.ShapeDtypeStruct((p, 2 * q, c), F32),
        compiler_params=_cparams("arbitrary"), name="filter_spectrum",
    )(g5, m3k, s)


def _direct_conv_body(z_ref, h0_ref, h1_ref, s_ref, tf_ref, tk_ref, ti_ref, o_ref):
    n2 = tf_ref.shape[0] // 2
    kh = _dot_dft(tk_ref[...], jnp.concatenate([h0_ref[0], h1_ref[0]], axis=0)) / s_ref[...]
    x = _dot_dft(tf_ref[...], z_ref[0])
    xr, xi, kr, ki = x[:n2], x[n2:], kh[:n2], kh[n2:]
    y = jnp.concatenate([xr * kr - xi * ki, xr * ki + xi * kr], axis=0)
    o_ref[0] = _dot_dft(ti_ref[...], y)


def _direct_conv(z, h0, h1, s, n, c=HY_WIDTH):
    bp = z.shape[0] // (2 * n)
    tf, tk, ti = (jnp.asarray(t, BF16) for t in _direct_tables(n))
    out = pl.pallas_call(
        _direct_conv_body,
        grid=(bp,),
        in_specs=[pl.BlockSpec((1, 2 * n, c), lambda b: (b, 0, 0)), _full((1, n, c)), _full((1, n, c)), _full((1, c)),
                  _full(tf.shape), _full(tk.shape), _full(ti.shape)],
        out_specs=pl.BlockSpec((1, 2 * n, c), lambda b: (b, 0, 0)),
        out_shape=jax.ShapeDtypeStruct((bp, 2 * n, c), F32),
        compiler_params=_cparams("arbitrary"), name="direct_conv",
    )(z.reshape(bp, 2 * n, c), h0, h1, s, tf, tk, ti)
    return out.reshape(bp * 2 * n, c)


def _cis(m, n_total):
    ang = (2.0 * np.pi / n_total) * (m % n_total).astype(np.float64)
    return np.cos(ang), -np.sin(ang)


@functools.lru_cache(maxsize=None)
def _direct_tables(n):
    nn = 2 * n
    f = np.arange(nn)[:, None]
    t = np.arange(n)[None, :]
    cr, ci = _cis(f * t, nn)
    tf = np.block([[cr, -ci], [ci, cr]])
    tk = np.block([[cr, cr], [ci, -ci]])
    ti = np.block([[cr.T, ci.T], [-ci.T, cr.T]]) / nn
    return tuple(a.astype(np.float32) for a in (tf, tk, ti))


@functools.lru_cache(maxsize=None)
def _twostage_tables(p, q):
    nn = p * q
    hp = p // 2
    s = np.arange(q)[:, None, None]
    c = np.arange(p)[None, :, None]
    a = np.arange(hp)[None, None, :]
    tr, ti = _cis(c * (q * a + s), nn)
    z = np.zeros_like(tr)
    t_data = np.concatenate([np.concatenate([tr, -ti], 2), np.concatenate([ti, tr], 2)], 1)
    t_k0 = np.concatenate([tr, ti, z, z], 1)
    t_k1 = np.concatenate([z, z, tr, -ti], 1)
    trt, tit = np.swapaxes(tr, 1, 2) / nn, np.swapaxes(ti, 1, 2) / nn
    t_inv_r = np.concatenate([trt, -tit], 1)
    t_inv_i = np.concatenate([tit, trt], 1)
    d = np.arange(q)[:, None]
    b = np.arange(q)[None, :]
    fr, fi = _cis(d * b, q)
    m3 = np.block([[fr, -fi], [fi, fr]])
    m3i = np.block([[fr, fi], [-fi, fr]])
    m3k = np.block([[fr, -fi, fr, fi], [fi, fr, -fi, fr]])
    return tuple(x.astype(np.float32) for x in (t_data, t_k0, t_k1, t_inv_r, t_inv_i, m3, m3i, m3k))


def _long_conv_spectrum(n, lw):
    p, q = DFT_P, DFT_Q
    h0, h1, s = _hyena_filter(n, q, lw)
    _, t_k0, t_k1, _, _, _, _, m3k = _twostage_tables(p, q)
    gk = _colslot_matmul([h0, h1], [t_k0, t_k1], BF16)
    return _filter_spec(gk, jnp.asarray(m3k, BF16), s, p, q)


def _long_conv(zt, khat):
    p, q = DFT_P, DFT_Q
    t_data, _, _, t_inv_r, t_inv_i, m3, m3i, _ = _twostage_tables(p, q)
    gf = _colslot_matmul([zt], [t_data], BF16)
    hr, hi = _spec_mid(gf, jnp.asarray(m3, BF16), jnp.asarray(m3i, BF16), khat, p, q)
    y = _colslot_matmul([hr, hi], [t_inv_r, t_inv_i], F32)
    return y.reshape(-1, HY_WIDTH)


def _merge_body(tiles_per_group, fixed_group, x_ref, g1_ref, oa_ref, conv_ref, z_ref, x0_ref, gate_ref, bias_ref,
                wba_ref, wbh_ref, wout_ref, o_ref):
    group = fixed_group if fixed_group is not None else pl.program_id(0) // tiles_per_group
    o_hy = ((conv_ref[...] + z_ref[...] * bias_ref[...]) * x0_ref[...]).astype(BF16)
    d = D_MODEL
    merged = gate_ref[:, :d] * _dot(oa_ref[...], wba_ref[...]) + gate_ref[:, d:] * _dot(o_hy, wbh_ref[...])
    mix = _dot(merged.astype(BF16), wout_ref[...])
    o_ref[...] = x_ref[...] + _mod_row(g1_ref, group) * mix


def _merge(x, mod, o_attn, conv, z, x0, gates, lw, *, tiles_per_group, fixed_group):
    rows = x.shape[0]
    row_spec = lambda n: pl.BlockSpec((ROW_TILE, n), lambda t: (t, 0))
    c = HY_WIDTH
    return pl.pallas_call(
        functools.partial(_merge_body, tiles_per_group, fixed_group),
        grid=(rows // ROW_TILE,),
        in_specs=[row_spec(D_MODEL), _mod_spec(2), row_spec(N_HEADS * V_HEAD), row_spec(c), row_spec(c), row_spec(c),
                  row_spec(2 * D_MODEL), _full((1, c)), _full(lw["w_br_attn"].shape), _full(lw["w_br_hy"].shape),
                  _full(lw["w_out"].shape)],
        out_specs=row_spec(D_MODEL),
        out_shape=jax.ShapeDtypeStruct((rows, D_MODEL), F32),
        compiler_params=_cparams("arbitrary"), name="merge_out",
    )(x, mod, o_attn, conv, z, x0, gates, lw["hy_bias"], lw["w_br_attn"], lw["w_br_hy"], lw["w_out"])


def _swiglu(h, wg_ref, wu_ref, wd_ref):
    acc = None
    for c0, c1 in FFN_SPANS:
        gate = _dot(h, wg_ref[0, :, c0:c1])
        up = _dot(h, wu_ref[0, :, c0:c1])
        y = _dot((gate * jax.nn.sigmoid(gate) * up).astype(BF16), wd_ref[0, c0:c1, :])
        acc = y if acc is None else acc + y
    return acc


def _ffn_body(tiles_per_group, fixed_group, x_ref, sh_ref, sc_ref, g2_ref, n2_ref, wg_ref, wu_ref, wd_ref, o_ref):
    group = fixed_group if fixed_group is not None else pl.program_id(0) // tiles_per_group
    x = x_ref[...]
    h = (_rms(x, n2_ref[...]) * (1.0 + _mod_row(sc_ref, group)) + _mod_row(sh_ref, group)).astype(BF16)
    o_ref[...] = x + _mod_row(g2_ref, group) * _swiglu(h, wg_ref, wu_ref, wd_ref)


def _ffn(x, mod, norm_g, w_gate, w_up, w_down, *, tiles_per_group, fixed_group):
    rows = x.shape[0]
    tm = min(FFN_ROW_TILE, rows)
    row_spec = pl.BlockSpec((tm, D_MODEL), lambda t: (t, 0))
    return pl.pallas_call(
        functools.partial(_ffn_body, tiles_per_group * ROW_TILE // tm if tiles_per_group else None, fixed_group),
        grid=(rows // tm,),
        in_specs=[row_spec, _mod_spec(3), _mod_spec(4), _mod_spec(5), _full((1, D_MODEL)),
                  _full(w_gate.shape), _full(w_up.shape), _full(w_down.shape)],
        out_specs=row_spec,
        out_shape=jax.ShapeDtypeStruct((rows, D_MODEL), F32),
        compiler_params=_cparams("arbitrary"), name="ffn",
    )(x, mod, mod, mod, norm_g, w_gate, w_up, w_down)


def _route_body(tiles_per_group, x_ref, sh_ref, sc_ref, n2_ref, router_ref, h_ref, route_ref):
    group = pl.program_id(0) // tiles_per_group
    h = _rms(x_ref[...], n2_ref[...]) * (1.0 + _mod_row(sc_ref, group)) + _mod_row(sh_ref, group)
    h_ref[...] = h
    logits = _dot_hi(h, router_ref[...])
    lane = lax.broadcasted_iota(jnp.int32, logits.shape, 1)
    logits = jnp.where(lane < N_EXPERTS, logits, -jnp.inf)
    m1 = jnp.max(logits, axis=-1, keepdims=True)
    i1 = jnp.min(jnp.where(logits == m1, lane, LANES), axis=-1, keepdims=True)
    rest = jnp.where(lane == i1, -jnp.inf, logits)
    m2 = jnp.max(rest, axis=-1, keepdims=True)
    i2 = jnp.min(jnp.where(rest == m2, lane, LANES), axis=-1, keepdims=True)
    e2 = jnp.exp(m2 - m1)
    w1 = 1.0 / (1.0 + e2)
    route_ref[...] = (jnp.where(lane == 0, i1.astype(F32), 0.0) + jnp.where(lane == 1, i2.astype(F32), 0.0)
                      + jnp.where(lane == 2, w1, 0.0) + jnp.where(lane == 3, e2 * w1, 0.0))


def _route(x, mod, norm_g, router, tiles_per_group):
    rows = x.shape[0]
    row_spec = lambda n: pl.BlockSpec((ROW_TILE, n), lambda t: (t, 0))
    return pl.pallas_call(
        functools.partial(_route_body, tiles_per_group),
        grid=(rows // ROW_TILE,),
        in_specs=[row_spec(D_MODEL), _mod_spec(3), _mod_spec(4), _full((1, D_MODEL)), _full(router.shape)],
        out_specs=[row_spec(D_MODEL), row_spec(LANES)],
        out_shape=[jax.ShapeDtypeStruct((rows, D_MODEL), F32), jax.ShapeDtypeStruct((rows, LANES), F32)],
        compiler_params=_cparams("arbitrary"), name="moe_route",
    )(x, mod, mod, norm_g, router)


def _gather_rows(idx_ref, n_rows, src_hbm, dst, sem):
    def issue(i, carry):
        for prio in range(2):
            r = 2 * i + prio
            pltpu.make_async_copy(src_hbm.at[pl.ds(idx_ref[0, 0, r], 1), :], dst.at[pl.ds(r, 1), :],
                                  sem).start(priority=prio)
        return carry
    lax.fori_loop(0, n_rows // 2, issue, 0, unroll=4)


def _wait_rows(n_rows, src_hbm, dst, sem):
    pltpu.make_async_copy(src_hbm.at[pl.ds(0, n_rows), :], dst, sem).wait()


def _moe_dispatch_body(idx_ref, h_ref, xs_in_hbm, xs_hbm, stage, sem):
    del xs_in_hbm
    t = pl.program_id(0)
    n_tiles = pl.num_programs(0)
    tm = h_ref.shape[0]
    slot = t % 2

    def wait(s):
        for _ in range(2):
            pltpu.make_async_copy(stage.at[s], xs_hbm.at[pl.ds(0, tm), :], sem.at[s]).wait()

    @pl.when(t >= 2)
    def _():
        wait(slot)

    stage[slot] = h_ref[...]

    def issue(r, carry):
        row = stage.at[slot, pl.ds(r, 1), :]
        pltpu.make_async_copy(row, xs_hbm.at[pl.ds(idx_ref[0, 0, r], 1), :], sem.at[slot]).start(priority=0)
        pltpu.make_async_copy(row, xs_hbm.at[pl.ds(idx_ref[0, 0, tm + r], 1), :], sem.at[slot]).start(priority=1)
        return carry
    lax.fori_loop(0, tm, issue, 0, unroll=4)

    @pl.when(t == n_tiles - 1)
    def _():
        wait(slot)
        wait(1 - slot)


def _tile_index_blocks(dest, tm):
    n_tiles = dest.shape[0] // tm
    return dest.reshape(n_tiles, tm, 2).transpose(0, 2, 1).reshape(n_tiles, 1, 2 * tm)


def _moe_dispatch(h, dest, n_rows):
    tm = ROW_TILE
    idx = _tile_index_blocks(dest, tm)
    n_tiles = idx.shape[0]
    assert n_tiles >= 2
    return pl.pallas_call(
        _moe_dispatch_body,
        grid=(n_tiles,),
        in_specs=[pl.BlockSpec((1, 1, 2 * tm), lambda t: (t, 0, 0), memory_space=pltpu.SMEM),
                  pl.BlockSpec((tm, D_MODEL), lambda t: (t, 0)), pl.BlockSpec(memory_space=pl.ANY)],
        out_specs=pl.BlockSpec(memory_space=pl.ANY),
        out_shape=jax.ShapeDtypeStruct((n_rows, D_MODEL), F32),
        scratch_shapes=[pltpu.VMEM((2, tm, D_MODEL), F32), pltpu.SemaphoreType.DMA((2,))],
        input_output_aliases={2: 0},
        compiler_params=_cparams("arbitrary"), name="moe_dispatch",
    )(idx, h, jnp.zeros((n_rows, D_MODEL), F32))


def _moe_group_body(te_ref, nu_ref, x_ref, wg_ref, wu_ref, wd_ref, y_ref):
    t = pl.program_id(0)

    @pl.when(t < nu_ref[0])
    def _():
        y_ref[...] = _swiglu(x_ref[...].astype(BF16), wg_ref, wu_ref, wd_ref)

    @pl.when(t >= nu_ref[0])
    def _():
        y_ref[...] = jnp.zeros_like(y_ref)


def _moe_group(xs, tile_expert, n_used, w_gate, w_up, w_down):
    tm = MOE_ROW_TILE
    n_rows = xs.shape[0]
    w_spec = lambda w: pl.BlockSpec((1,) + w.shape[1:], lambda t, te, nu: (te[t], 0, 0))
    grid_spec = pltpu.PrefetchScalarGridSpec(
        num_scalar_prefetch=2,
        grid=(n_rows // tm,),
        in_specs=[pl.BlockSpec((tm, D_MODEL), lambda t, te, nu: (jnp.minimum(t, nu[0] - 1), 0)),
                  w_spec(w_gate), w_spec(w_up), w_spec(w_down)],
        out_specs=pl.BlockSpec((tm, D_MODEL), lambda t, te, nu: (t, 0)),
    )
    return pl.pallas_call(
        _moe_group_body, grid_spec=grid_spec,
        out_shape=jax.ShapeDtypeStruct((n_rows, D_MODEL), F32),
        compiler_params=_cparams("arbitrary"), name="moe_group",
    )(tile_expert, n_used, xs, w_gate, w_up, w_down)


def _moe_combine_body(tiles_per_group, idx_ref, idx_next_ref, x_ref, g2_ref, fg_ref, route_ref, y_hbm, o_ref,
                      yg_scr, sem):
    t = pl.program_id(0)
    n_tiles = pl.num_programs(0)
    group = t // tiles_per_group
    tm = x_ref.shape[0]
    slot = t % 2

    @pl.when(t == 0)
    def _():
        _gather_rows(idx_ref, 2 * tm, y_hbm, yg_scr.at[0], sem.at[0])

    _wait_rows(2 * tm, y_hbm, yg_scr.at[slot], sem.at[slot])

    @pl.when(t + 1 < n_tiles)
    def _():
        _gather_rows(idx_next_ref, 2 * tm, y_hbm, yg_scr.at[1 - slot], sem.at[1 - slot])

    route = route_ref[...]
    y = route[:, 2:3] * yg_scr[slot, :tm] + route[:, 3:4] * yg_scr[slot, tm:]
    o_ref[...] = _rms(x_ref[...] + _mod_row(g2_ref, group) * y, fg_ref[...])


def _moe_combine(x, mod, final_g, route, y_sorted, dest, tiles_per_group):
    rows = x.shape[0]
    tm = ROW_TILE
    idx = _tile_index_blocks(dest, tm)
    n_tiles = idx.shape[0]
    return pl.pallas_call(
        functools.partial(_moe_combine_body, tiles_per_group),
        grid=(n_tiles,),
        in_specs=[pl.BlockSpec((1, 1, 2 * tm), lambda t: (t, 0, 0), memory_space=pltpu.SMEM),
                  pl.BlockSpec((1, 1, 2 * tm), lambda t: (jnp.minimum(t + 1, n_tiles - 1), 0, 0),
                               memory_space=pltpu.SMEM),
                  pl.BlockSpec((tm, D_MODEL), lambda t: (t, 0)), _mod_spec(5), _full((1, D_MODEL)),
                  pl.BlockSpec((tm, LANES), lambda t: (t, 0)), pl.BlockSpec(memory_space=pl.ANY)],
        out_specs=pl.BlockSpec((tm, D_MODEL), lambda t: (t, 0)),
        out_shape=jax.ShapeDtypeStruct((rows, D_MODEL), F32),
        scratch_shapes=[pltpu.VMEM((2, 2 * tm, D_MODEL), F32), pltpu.SemaphoreType.DMA((2,))],
        compiler_params=_cparams("arbitrary"), name="moe_combine",
    )(idx, idx, x, mod, final_g, route, y_sorted)


def _moe_plan(route, tm):
    t = route.shape[0]
    experts = jnp.concatenate([route[:, 0], route[:, 1]]).astype(jnp.int32)
    onehot = (experts[:, None] == jnp.arange(N_EXPERTS, dtype=jnp.int32)[None, :]).astype(jnp.int32)
    csum = jnp.cumsum(onehot, axis=0)
    rank = jnp.sum(csum * onehot, axis=1) - 1
    padded = (csum[-1] + tm - 1) // tm * tm
    ends = jnp.cumsum(padded)
    dest = jnp.sum((ends - padded)[None, :] * onehot, axis=1) + rank
    n_rows = 2 * t + N_EXPERTS * tm
    tile_start = jnp.arange(n_rows // tm, dtype=jnp.int32) * tm
    tile_expert = jnp.minimum(jnp.sum((tile_start[:, None] >= ends[None, :]).astype(jnp.int32), axis=1), N_EXPERTS - 1)
    n_used = (ends[-1:] // tm).astype(jnp.int32)
    return dest.reshape(2, t).T, tile_expert, n_used, n_rows


def _moe(x, mod, norm_g, router, w_gate, w_up, w_down, final_g, tiles_per_group):
    h, route = _route(x, mod, norm_g, router, tiles_per_group)
    dest, tile_expert, n_used, n_rows = _moe_plan(route, MOE_ROW_TILE)
    xs = _moe_dispatch(h, dest, n_rows)
    y_sorted = _moe_group(xs, tile_expert, n_used, w_gate, w_up, w_down)
    return _moe_combine(x, mod, final_g, route, y_sorted, dest, tiles_per_group)


def _pad_heads(w, width):
    k = w.shape[0]
    w = w.reshape(k, N_HEADS, width)
    return jnp.pad(w, ((0, 0), (0, 0), (0, HEAD_PAD - width))).reshape(k, N_HEADS * HEAD_PAD)


def _rot_cols(w):
    half = QK_ROPE // 2
    return jnp.concatenate([-w[..., half:], w[..., :half]], axis=-1)


def _layer_weights(p, layer):
    w_in = p["w_in"][layer]
    w_uq = p["w_uq"][layer].reshape(Q_LORA, N_HEADS, QK_NOPE + QK_ROPE)
    zeros_nope = jnp.zeros((Q_LORA, N_HEADS, QK_NOPE), F32)
    w_uq_b = jnp.concatenate([zeros_nope, _rot_cols(w_uq[..., QK_NOPE:])], axis=-1)
    w_kr = w_in[:, KV_START + KV_LORA:HY_START]
    zk = jnp.zeros((D_MODEL, QK_NOPE), F32)
    zp = jnp.zeros((D_MODEL, HEAD_PAD - QK_NOPE - QK_ROPE), F32)
    w_kr2 = jnp.concatenate([zk, w_kr, zp, zk, _rot_cols(w_kr), zp], axis=-1)
    row = lambda v: v.reshape(1, -1)
    bf = lambda v: v.astype(BF16)
    return {
        "norm1_g": row(p["norm1_g"][layer]), "norm2_g": row(p["norm2_g"][layer]),
        "w_q": bf(w_in[:, :Q_LORA]), "q_norm_g": row(p["q_norm_g"][layer]),
        "w_uq_a": bf(_pad_heads(w_uq.reshape(Q_LORA, -1), QK_NOPE + QK_ROPE)),
        "w_uq_b": bf(_pad_heads(w_uq_b.reshape(Q_LORA, -1), QK_NOPE + QK_ROPE)),
        "w_kv": bf(w_in[:, KV_START:KV_START + KV_LORA]), "kv_norm_g": row(p["kv_norm_g"][layer]),
        "w_uk": bf(_pad_heads(p["w_uk"][layer], QK_NOPE)), "w_uv": bf(_pad_heads(p["w_uv"][layer], V_HEAD).T),
        "w_kr": bf(w_kr2),
        "w_hy": bf(w_in[:, HY_START:GATE_START]), "w_gate": bf(w_in[:, GATE_START:]),
        "hy_short_w": p["hy_short_w"][layer], "hy_short_b": p["hy_short_b"][layer],
        "hy_w1": p["hy_w1"][layer], "hy_b1": p["hy_b1"][layer], "hy_w2": p["hy_w2"][layer],
        "hy_b2": p["hy_b2"][layer], "hy_w3": p["hy_w3"][layer], "hy_freq": p["hy_freq"][layer],
        "hy_decay": p["hy_decay"][layer], "hy_bias": row(p["hy_bias"][layer]),
        "w_br_attn": bf(p["w_br_attn"][layer]), "w_br_hy": bf(p["w_br_hy"][layer]), "w_out": bf(p["w_out"][layer]),
    }


def _rope_tables(n):
    rows = n // GRID_W
    n_freq = QK_ROPE // 4
    inv_freq = ROPE_BASE ** (-jnp.arange(n_freq, dtype=F32) / n_freq)
    r = jnp.repeat(jnp.arange(rows, dtype=F32), GRID_W)
    col = jnp.tile(jnp.arange(GRID_W, dtype=F32), rows)
    ang = jnp.concatenate([r[:, None] * inv_freq, col[:, None] * inv_freq], axis=-1)
    cos, sin = jnp.cos(ang), jnp.sin(ang)
    ones = jnp.ones((n, QK_NOPE), F32)
    zeros = jnp.zeros((n, QK_NOPE), F32)
    pad = jnp.zeros((n, HEAD_PAD - QK_NOPE - QK_ROPE), F32)
    return (jnp.concatenate([ones, cos, cos, pad], axis=-1), jnp.concatenate([zeros, sin, sin, pad], axis=-1))


def _token_mixer(xs, mod, lw, khat, conv_short, *, batch, n, tiles_per_group, fixed_group, rope_tabs,
                 k_ctx=None, v_ctx=None):
    pr = _inproj(xs, mod, lw, tiles_per_group=tiles_per_group, fixed_group=fixed_group, rope_tabs=rope_tabs,
                 want_q=True, want_hg=True)
    k3 = pr["k"].reshape(batch, n, -1)
    v3 = pr["v"].reshape(batch, n // VT_BLOCK, N_HEADS * HEAD_PAD, VT_BLOCK)
    if k_ctx is None:
        o_attn = _attention(pr["q"], k3, v3, None, None, batch)
    else:
        o_attn = _attention(pr["q"], k_ctx, v_ctx, k3, v3, batch)
    if conv_short:
        z, x0 = _hyena_pre(pr["p_hy"], lw["hy_short_w"], lw["hy_short_b"], batch, None)
        conv = _direct_conv(z, *khat, n)
    else:
        z, x0, zt = _hyena_pre(pr["p_hy"], lw["hy_short_w"], lw["hy_short_b"], batch, DFT_Q)
        conv = _long_conv(zt, khat)
    x_new = _merge(xs, mod, o_attn, conv, z, x0, pr["gates"], lw, tiles_per_group=tiles_per_group,
                   fixed_group=fixed_group)
    return x_new, k3, v3


def kernel(x, c, ctx, c_ctx, w_mod, b_mod, norm1_g, norm2_g, w_in, q_norm_g, kv_norm_g, w_uq, w_uk, w_uv, hy_short_w, hy_short_b, hy_w1, hy_b1, hy_w2, hy_b2, hy_w3, hy_freq, hy_decay, hy_bias, w_br_attn, w_br_hy, w_out, ffn_w_gate, ffn_w_up, ffn_w_down, moe_router, moe_w_gate, moe_w_up, moe_w_down, final_g):
    p = dict(w_in=w_in, norm1_g=norm1_g, norm2_g=norm2_g, q_norm_g=q_norm_g, kv_norm_g=kv_norm_g, w_uq=w_uq,
             w_uk=w_uk, w_uv=w_uv, hy_short_w=hy_short_w, hy_short_b=hy_short_b, hy_w1=hy_w1, hy_b1=hy_b1,
             hy_w2=hy_w2, hy_b2=hy_b2, hy_w3=hy_w3, hy_freq=hy_freq, hy_decay=hy_decay, hy_bias=hy_bias,
             w_br_attn=w_br_attn, w_br_hy=w_br_hy, w_out=w_out)
    batch, seq, d = x.shape
    ctx_len = ctx.shape[1]
    depth = w_mod.shape[0]
    ctx_group = batch
    cond8 = jnp.zeros((SUBLANES, d), F32).at[:batch].set(c).at[ctx_group].set(c_ctx)
    rope_tabs = _rope_tables(seq)
    lat_tiles = seq // ROW_TILE
    xs = x.reshape(batch * seq, d)
    cs = ctx.reshape(batch * ctx_len, d)
    bf = lambda v: v.astype(BF16)
    for layer in range(depth):
        last = layer == depth - 1
        lw = _layer_weights(p, layer)
        mod = _adaln(cond8, w_mod[layer], b_mod[layer])
        khat_lat = _long_conv_spectrum(seq, lw)
        if last:
            pr = _inproj(cs, mod, lw, tiles_per_group=None, fixed_group=ctx_group, rope_tabs=None, want_q=False,
                         want_hg=False)
            k_ctx = pr["k"].reshape(batch, ctx_len, -1)
            v_ctx = pr["v"].reshape(batch, ctx_len // VT_BLOCK, N_HEADS * HEAD_PAD, VT_BLOCK)
        else:
            khat_ctx = _hyena_filter(ctx_len, ctx_len, lw)
            cs_mid, k_ctx, v_ctx = _token_mixer(cs, mod, lw, khat_ctx, True, batch=batch, n=ctx_len,
                                                tiles_per_group=None, fixed_group=ctx_group, rope_tabs=None)
        xs, _, _ = _token_mixer(xs, mod, lw, khat_lat, False, batch=batch, n=seq, tiles_per_group=lat_tiles,
                                fixed_group=None, rope_tabs=rope_tabs, k_ctx=k_ctx, v_ctx=v_ctx)
        i = layer // 2
        n2 = lw["norm2_g"]
        if layer % 2 == 0:
            assert not last
            wg, wu, wd = bf(ffn_w_gate[i])[None], bf(ffn_w_up[i])[None], bf(ffn_w_down[i])[None]
            xs = _ffn(xs, mod, n2, wg, wu, wd, tiles_per_group=lat_tiles, fixed_group=None)
            cs = _ffn(cs_mid, mod, n2, wg, wu, wd, tiles_per_group=None, fixed_group=ctx_group)
        else:
            assert last
            router = jnp.pad(moe_router[i], ((0, 0), (0, LANES - N_EXPERTS)))
            xs = _moe(xs, mod, n2, router, bf(moe_w_gate[i]), bf(moe_w_up[i]), bf(moe_w_down[i]),
                      final_g.reshape(1, d), lat_tiles)
    return xs.reshape(batch, seq, d)
```

```python
import functools
import math

import numpy as np
import jax
import jax.numpy as jnp
from jax import lax
from jax.experimental import pallas as pl
from jax.experimental.pallas import tpu as pltpu

F32 = jnp.float32
BF16 = jnp.bfloat16
HIGHEST = lax.Precision.HIGHEST

D_MODEL = 1024
GRID_W = 64
EPS = 1e-6
N_HEADS = 8
Q_LORA = 384
KV_LORA = 256
QK_NOPE = 64
QK_ROPE = 32
V_HEAD = 64
ROPE_BASE = 10000.0
ATTN_SCALE = (QK_NOPE + QK_ROPE) ** -0.5
HY_WIDTH = 512
HY_EMB = 17
HY_HIDDEN = 64
D_FF = 2816
N_EXPERTS = 8
KV_START = Q_LORA
HY_START = KV_START + KV_LORA + QK_ROPE
GATE_START = HY_START + 3 * HY_WIDTH

LANES = 128
SUBLANES = 8
HALO_ROWS = 16
HEAD_PAD = LANES
VMEM_LIMIT = 56 * 2**20

ROW_TILE = 256
INPROJ_ROW_TILE = 512
FFN_ROW_TILE = 512
MXU_TILE = 256
FFN_SPANS = ((0, 4 * MXU_TILE), (4 * MXU_TILE, 8 * MXU_TILE), (8 * MXU_TILE, D_FF))
MOE_ROW_TILE = 256
VT_BLOCK = ROW_TILE
ATTN_KV = 512
ATTN_Q_TILE = 1024
ATTN_HEAD_GROUPS = ((0, 1),)
Q_SCALE = ATTN_SCALE * math.log2(math.e)
DFT_P = 64
DFT_Q = 128
COLSLOT_BQ = 16
DFT_CB = 4
FILTER_CHUNKS = 4


def _cparams(*sem):
    return pltpu.CompilerParams(dimension_semantics=sem, vmem_limit_bytes=VMEM_LIMIT)


def _dot(a, b):
    return jnp.dot(a, b, preferred_element_type=F32)


def _dot_hi(a, b):
    return jnp.dot(a, b, precision=HIGHEST, preferred_element_type=F32)


def _dot_dft(table, x):
    return _dot(table, x.astype(BF16))


def _rms(xf, g):
    return xf * lax.rsqrt(jnp.mean(xf * xf, axis=-1, keepdims=True) + EPS) * g


def _full(shape):
    nd = len(shape)
    return pl.BlockSpec(shape, lambda *_: (0,) * nd, pipeline_mode=pl.Buffered(1))


def _adaln_body(c_ref, w_ref, b_ref, o_ref):
    c = c_ref[...]
    o_ref[...] = _dot_hi(c * jax.nn.sigmoid(c), w_ref[...]) + b_ref[...]


def _adaln(cond8, w, b):
    d, n = w.shape
    return pl.pallas_call(
        _adaln_body,
        grid=(n // d,),
        in_specs=[_full((SUBLANES, d)), pl.BlockSpec((d, d), lambda j: (0, j)),
                  pl.BlockSpec((1, d), lambda j: (0, j))],
        out_specs=pl.BlockSpec((SUBLANES, d), lambda j: (0, j)),
        out_shape=jax.ShapeDtypeStruct((SUBLANES, n), F32),
        compiler_params=_cparams("arbitrary"),
        name="adaln",
    )(cond8, w, b.reshape(1, n))


def _mod_spec(chunk):
    return pl.BlockSpec((SUBLANES, D_MODEL), lambda t, *_: (0, chunk))


def _mod_row(ref, group):
    return ref[pl.ds(group, 1), :]


def _inproj_body(tiles_per_group, fixed_group, use_rope, want_q, want_hg, *refs):
    it = iter(refs)
    x_ref, sh_ref, sc_ref, g1_ref = next(it), next(it), next(it), next(it)
    wkv_ref, kvg_ref, wuk_ref, wuv_ref, wkr_ref = next(it), next(it), next(it), next(it), next(it)
    if want_q:
        wq_ref, qg_ref, wuqa_ref, wuqb_ref = next(it), next(it), next(it), next(it)
    if want_hg:
        why_ref, wgate_ref = next(it), next(it)
    if use_rope:
        cos_ref, sin_ref = next(it), next(it)
    k_out, v_out = next(it), next(it)
    if want_q:
        q_out = next(it)
    if want_hg:
        phy_out, gate_out = next(it), next(it)

    group = fixed_group if fixed_group is not None else pl.program_id(0) // tiles_per_group
    xf = x_ref[...]
    h = _rms(xf, g1_ref[...]) * (1.0 + _mod_row(sc_ref, group)) + _mod_row(sh_ref, group)
    h = h.astype(BF16)
    if use_rope:
        cos, sin = cos_ref[...], sin_ref[...]

    def rope(a, b):
        return a * cos + b * sin if use_rope else a

    ckv = _rms(_dot(h, wkv_ref[...]), kvg_ref[...]).astype(BF16)
    vt = lax.dot_general(wuv_ref[...], ckv, (((1,), (1,)), ((), ())), preferred_element_type=F32)
    vrow = lax.broadcasted_iota(jnp.int32, vt.shape, 0) & (HEAD_PAD - 1)
    vt = jnp.where(vrow == V_HEAD, 1.0, vt).astype(BF16)
    for j in range(v_out.shape[0]):
        v_out[j] = vt[:, j * VT_BLOCK:(j + 1) * VT_BLOCK]
    k_nope = _dot(ckv, wuk_ref[...])
    kr = _dot(h, wkr_ref[...])
    k_rope = rope(kr[:, :HEAD_PAD], kr[:, HEAD_PAD:])
    for hd in range(N_HEADS):
        sl = slice(hd * HEAD_PAD, (hd + 1) * HEAD_PAD)
        k_out[:, sl] = (k_nope[:, sl] + k_rope).astype(BF16)
    if want_q:
        qn = _rms(_dot(h, wq_ref[...]), qg_ref[...]).astype(BF16)
        qa = _dot(qn, wuqa_ref[...])
        qb = _dot(qn, wuqb_ref[...]) if use_rope else None
        for hd in range(N_HEADS):
            sl = slice(hd * HEAD_PAD, (hd + 1) * HEAD_PAD)
            q_out[:, sl] = (rope(qa[:, sl], None if qb is None else qb[:, sl]) * Q_SCALE).astype(BF16)
    if want_hg:
        n_hy = why_ref.shape[1]
        for c0 in range(0, n_hy, 512):
            phy_out[:, c0:c0 + 512] = _dot(h, why_ref[:, c0:c0 + 512]).astype(BF16)
        n_g = wgate_ref.shape[1]
        for c0 in range(0, n_g, 512):
            gate_out[:, c0:c0 + 512] = jax.nn.sigmoid(_dot(h, wgate_ref[:, c0:c0 + 512])).astype(BF16)


def _inproj(x, mod, lw, *, tiles_per_group, fixed_group, rope_tabs, want_q, want_hg):
    rows = x.shape[0]
    tm = INPROJ_ROW_TILE
    nt = rows // tm
    if tiles_per_group is not None:
        tiles_per_group = tiles_per_group * ROW_TILE // tm
    use_rope = rope_tabs is not None
    row_spec = lambda n: pl.BlockSpec((tm, n), lambda t: (t, 0))
    ins = [x, mod, mod, lw["norm1_g"], lw["w_kv"], lw["kv_norm_g"], lw["w_uk"], lw["w_uv"], lw["w_kr"]]
    specs = [row_spec(D_MODEL), _mod_spec(0), _mod_spec(1), _full((1, D_MODEL)),
             _full(lw["w_kv"].shape), _full((1, KV_LORA)), _full(lw["w_uk"].shape), _full(lw["w_uv"].shape),
             _full(lw["w_kr"].shape)]
    if want_q:
        ins += [lw["w_q"], lw["q_norm_g"], lw["w_uq_a"], lw["w_uq_b"]]
        specs += [_full(lw["w_q"].shape), _full((1, Q_LORA)), _full(lw["w_uq_a"].shape), _full(lw["w_uq_b"].shape)]
    if want_hg:
        ins += [lw["w_hy"], lw["w_gate"]]
        specs += [_full(lw["w_hy"].shape), _full(lw["w_gate"].shape)]
    if use_rope:
        seq_tiles = rope_tabs[0].shape[0] // tm
        ins += list(rope_tabs)
        specs += [pl.BlockSpec((tm, HEAD_PAD), lambda t: (t % seq_tiles, 0))] * 2
    hp = N_HEADS * HEAD_PAD
    vt_blocks = tm // VT_BLOCK
    out_shape = [jax.ShapeDtypeStruct((rows, hp), BF16), jax.ShapeDtypeStruct((rows // VT_BLOCK, hp, VT_BLOCK), BF16)]
    out_specs = [row_spec(hp), pl.BlockSpec((vt_blocks, hp, VT_BLOCK), lambda t: (t, 0, 0))]
    if want_q:
        out_shape.append(jax.ShapeDtypeStruct((rows, hp), BF16))
        out_specs.append(row_spec(hp))
    if want_hg:
        out_shape += [jax.ShapeDtypeStruct((rows, 3 * HY_WIDTH), BF16), jax.ShapeDtypeStruct((rows, 2 * D_MODEL), BF16)]
        out_specs += [row_spec(3 * HY_WIDTH), row_spec(2 * D_MODEL)]
    outs = pl.pallas_call(
        functools.partial(_inproj_body, tiles_per_group, fixed_group, use_rope, want_q, want_hg),
        grid=(nt,), in_specs=specs, out_specs=out_specs, out_shape=out_shape,
        compiler_params=_cparams("arbitrary"), name="inproj",
    )(*ins)
    res = {"k": outs[0], "v": outs[1]}
    i = 2
    if want_q:
        res["q"] = outs[i]
        i += 1
    if want_hg:
        res["p_hy"], res["gates"] = outs[i], outs[i + 1]
    return res


def _attn_body(n_lat_blocks, *refs):
    if n_lat_blocks:
        q_ref, kc_ref, vc_ref, kl_ref, vl_ref, o_ref = refs
    else:
        q_ref, kc_ref, vc_ref, o_ref = refs
    tq = q_ref.shape[0]
    q = q_ref[...]
    sub = ATTN_KV // VT_BLOCK

    def run(heads):
        nh = len(heads)

        def scores(kblk):
            return tuple(lax.dot_general(kblk[:, sl], q[:, sl], (((1,), (1,)), ((), ())),
                                         preferred_element_type=F32) for sl in heads)

        def update(st, vt_blocks, carry):
            m_new = [jnp.maximum(carry[hd][0], jnp.max(st[hd], axis=0, keepdims=True)) for hd in range(nh)]
            p = [jnp.exp2(st[hd] - m_new[hd]).astype(BF16) for hd in range(nh)]
            out = []
            for hd in range(nh):
                m, acc = carry[hd]
                pv = None
                for j, vt in enumerate(vt_blocks):
                    r = _dot(vt[heads[hd], :], p[hd][j * VT_BLOCK:(j + 1) * VT_BLOCK])
                    pv = r if pv is None else pv + r
                out.append((m_new[hd], jnp.exp2(m - m_new[hd]) * acc + pv))
            return tuple(out)

        init = tuple((jnp.full((1, tq), -jnp.inf, F32), jnp.zeros((HEAD_PAD, tq), F32)) for _ in range(nh))
        carry = update(scores(kc_ref[0]), [vc_ref[0, 0]], init)
        if n_lat_blocks:
            def k_block(i):
                return kl_ref[0, pl.ds(i * ATTN_KV, ATTN_KV), :]

            def v_blocks(i):
                return [vl_ref[0, i * sub + j] for j in range(sub)]

            st = scores(k_block(0))
            for i in range(n_lat_blocks - 1):
                st_next = scores(k_block(i + 1))
                carry = update(st, v_blocks(i), carry)
                st = st_next
            carry = update(st, v_blocks(n_lat_blocks - 1), carry)
        return [acc[:V_HEAD] / acc[V_HEAD:V_HEAD + 1] for _, acc in carry]

    slices = [slice(hd * HEAD_PAD, (hd + 1) * HEAD_PAD) for hd in range(2)]
    outs = []
    for group in ATTN_HEAD_GROUPS:
        outs += run([slices[hd] for hd in group])
    o_ref[...] = jnp.concatenate(outs, axis=0).T.astype(BF16)


def _attention(q, k_ctx, vt_ctx, k_lat, vt_lat, batch):
    rows = q.shape[0]
    tq = min(ATTN_Q_TILE, rows // batch)
    qt_per_b = rows // batch // tq
    ctx_len = k_ctx.shape[1]
    assert ctx_len == VT_BLOCK
    pair_w = 2 * HEAD_PAD
    ins = [q, k_ctx, vt_ctx]
    specs = [pl.BlockSpec((tq, pair_w), lambda b, hp, t: (b * qt_per_b + t, hp)),
             pl.BlockSpec((1, ctx_len, pair_w), lambda b, hp, t: (b, 0, hp)),
             pl.BlockSpec((1, 1, pair_w, VT_BLOCK), lambda b, hp, t: (b, 0, hp, 0))]
    n_lat_blocks = 0
    if k_lat is not None:
        lat_len = k_lat.shape[1]
        n_lat_blocks = lat_len // ATTN_KV
        ins += [k_lat, vt_lat]
        specs += [pl.BlockSpec((1, lat_len, pair_w), lambda b, hp, t: (b, 0, hp)),
                  pl.BlockSpec((1, lat_len // VT_BLOCK, pair_w, VT_BLOCK), lambda b, hp, t: (b, 0, hp, 0))]
    return pl.pallas_call(
        functools.partial(_attn_body, n_lat_blocks),
        grid=(batch, N_HEADS // 2, qt_per_b), in_specs=specs,
        out_specs=pl.BlockSpec((tq, 2 * V_HEAD), lambda b, hp, t: (b * qt_per_b + t, hp)),
        out_shape=jax.ShapeDtypeStruct((rows, N_HEADS * V_HEAD), BF16),
        compiler_params=_cparams("arbitrary", "arbitrary", "arbitrary"), name="attention",
    )(*ins)


def _hypre_body(seq_tiles, n_chunks, p_ref, prev_ref, next_ref, w_ref, b_ref, z_ref, x0_ref, *zt_ref):
    j = pl.program_id(0) % seq_tiles
    p = p_ref[...].astype(F32)
    tm = p.shape[0]
    row = lax.broadcasted_iota(jnp.int32, (tm, 1), 0)
    prev_row = jnp.where(j != 0, prev_ref[...].astype(F32)[HALO_ROWS - 1:HALO_ROWS, :], 0.0)
    next_row = jnp.where(j != seq_tiles - 1, next_ref[...].astype(F32)[0:1, :], 0.0)
    up = jnp.where(row == 0, prev_row, pltpu.roll(p, 1, 0))
    dn = jnp.where(row == tm - 1, next_row, pltpu.roll(p, tm - 1, 0))
    u = up * w_ref[0:1, :] + p * w_ref[1:2, :] + dn * w_ref[2:3, :] + b_ref[...]
    c = HY_WIDTH
    z = u[:, :c] * u[:, c:2 * c]
    z_ref[...] = z.astype(BF16)
    x0_ref[...] = u[:, 2 * c:].astype(BF16)
    if zt_ref:
        q = tm // n_chunks
        for a in range(n_chunks):
            zt_ref[0][0, :, a * c:(a + 1) * c] = z[a * q:(a + 1) * q, :].astype(BF16)


def _hyena_pre(p_hy, short_w, short_b, batch, dft_q):
    rows = p_hy.shape[0]
    n = rows // batch
    seq_tiles = n // ROW_TILE
    nt = rows // ROW_TILE
    c3 = 3 * HY_WIDTH
    halo = ROW_TILE // HALO_ROWS
    last_halo = rows // HALO_ROWS - 1
    specs = [pl.BlockSpec((ROW_TILE, c3), lambda t: (t, 0)),
             pl.BlockSpec((HALO_ROWS, c3), lambda t: (jnp.maximum(t * halo - 1, 0), 0)),
             pl.BlockSpec((HALO_ROWS, c3), lambda t: (jnp.minimum((t + 1) * halo, last_halo), 0)),
             _full((3, c3)), _full((1, c3))]
    out_shape = [jax.ShapeDtypeStruct((rows, HY_WIDTH), BF16)] * 2
    out_specs = [pl.BlockSpec((ROW_TILE, HY_WIDTH), lambda t: (t, 0))] * 2
    n_chunks = 1
    if dft_q is not None:
        n_chunks = ROW_TILE // dft_q
        half_p = n // dft_q
        out_shape.append(jax.ShapeDtypeStruct((batch // 2, dft_q, 2 * half_p * HY_WIDTH), BF16))

        def zt_map(t):
            b, jt = t // seq_tiles, t % seq_tiles
            return (b // 2, 0, (b % 2) * seq_tiles + jt)
        out_specs.append(pl.BlockSpec((1, dft_q, n_chunks * HY_WIDTH), zt_map))
    return pl.pallas_call(
        functools.partial(_hypre_body, seq_tiles, n_chunks),
        grid=(nt,), in_specs=specs, out_specs=out_specs, out_shape=out_shape,
        compiler_params=_cparams("arbitrary"), name="hyena_pre",
    )(p_hy, p_hy, p_hy, short_w, short_b.reshape(1, c3))


def _filter_body(emb_ref, w1_ref, b1_ref, w2_ref, b2_ref, w3_ref, freq_ref, decay_ref, h0_ref, h1_ref, s_ref):
    a = pl.program_id(0)
    emb = emb_ref[...]
    freq = freq_ref[...]
    h = jnp.sin(freq * (_dot_hi(emb, w1_ref[...]) + b1_ref[...]))
    h = jnp.sin(freq * (_dot_hi(h, w2_ref[...]) + b2_ref[...]))
    h = _dot_hi(h, w3_ref[...]) * jnp.exp(-emb[:, 0:1] * jnp.abs(decay_ref[...]))
    c = HY_WIDTH
    row = lax.broadcasted_iota(jnp.int32, (emb.shape[0], 1), 0)
    h0 = h[:, :c]
    h1 = jnp.where(jnp.logical_and(a == 0, row == 0), 0.0, h[:, c:])
    q = h0_ref.shape[1]
    for j in range(h0_ref.shape[2] // c):
        h0_ref[0, :, j * c:(j + 1) * c] = h0[j * q:(j + 1) * q].astype(BF16)
        h1_ref[0, :, j * c:(j + 1) * c] = h1[j * q:(j + 1) * q].astype(BF16)
    part = jnp.sum(jnp.abs(h0) + jnp.abs(h1), axis=0, keepdims=True)

    @pl.when(a == 0)
    def _():
        s_ref[...] = jnp.zeros_like(s_ref)
    s_ref[...] += part


def _hyena_filter(n, q, lw):
    f32 = F32
    bands = (HY_EMB - 1) // 2
    t = jnp.linspace(0.0, 1.0, n, dtype=f32)[:, None]
    phase = (2.0 * math.pi / n) * jnp.arange(n, dtype=f32)[:, None] * jnp.linspace(1e-4, bands - 1, bands, dtype=f32)
    emb = jnp.concatenate([t, jnp.cos(phase), -jnp.sin(phase), jnp.zeros((n, 32 - HY_EMB), f32)], axis=-1)
    w1 = jnp.concatenate([lw["hy_w1"], jnp.zeros((32 - HY_EMB, HY_HIDDEN), f32)], axis=0)
    c = HY_WIDTH
    slots = n // q
    per_step = min(FILTER_CHUNKS, slots)
    row = lambda v: v.reshape(1, -1)
    return pl.pallas_call(
        _filter_body,
        grid=(slots // per_step,),
        in_specs=[pl.BlockSpec((per_step * q, 32), lambda a: (a, 0)), _full((32, HY_HIDDEN)), _full((1, HY_HIDDEN)),
                  _full((HY_HIDDEN, HY_HIDDEN)), _full((1, HY_HIDDEN)), _full((HY_HIDDEN, 2 * c)),
                  _full((1, HY_HIDDEN)), _full((1, 2 * c))],
        out_specs=[pl.BlockSpec((1, q, per_step * c), lambda a: (0, 0, a)),
                   pl.BlockSpec((1, q, per_step * c), lambda a: (0, 0, a)),
                   pl.BlockSpec((1, c), lambda a: (0, 0))],
        out_shape=[jax.ShapeDtypeStruct((1, q, slots * c), BF16)] * 2 + [jax.ShapeDtypeStruct((1, c), f32)],
        compiler_params=_cparams("arbitrary"), name="hyena_filter",
    )(emb, w1, row(lw["hy_b1"]), lw["hy_w2"], row(lw["hy_b2"]), lw["hy_w3"], row(lw["hy_freq"]),
      row(lw["hy_decay"]))


def _colslot_body(nparts, bq, c, *refs):
    x_refs, t_refs, o_ref = refs[:nparts], refs[nparts:2 * nparts], refs[-1]
    for j in range(bq):
        acc = None
        for x_ref, t_ref in zip(x_refs, t_refs):
            r = _dot_dft(t_ref[j], x_ref[0, j])
            acc = r if acc is None else acc + r
        o_ref[0, :, j * c:(j + 1) * c] = acc.astype(o_ref.dtype)


def _colslot_matmul(xs, tabs, out_dtype, c=HY_WIDTH):
    tabs = [jnp.asarray(t, BF16) for t in tabs]
    g, q = xs[0].shape[0], xs[0].shape[1]
    m = tabs[0].shape[1]
    bq = COLSLOT_BQ
    xs4 = [x.reshape(g, q, -1, c) for x in xs]
    specs = [pl.BlockSpec((1, bq, x.shape[2], c), lambda gi, qi: (gi, qi, 0, 0)) for x in xs4]
    specs += [pl.BlockSpec((bq, m, t.shape[2]), lambda gi, qi: (qi, 0, 0)) for t in tabs]
    return pl.pallas_call(
        functools.partial(_colslot_body, len(xs), bq, c),
        grid=(g, q // bq), in_specs=specs,
        out_specs=pl.BlockSpec((1, m, bq * c), lambda gi, qi: (gi, 0, qi)),
        out_shape=jax.ShapeDtypeStruct((g, m, q * c), out_dtype),
        compiler_params=_cparams("arbitrary", "arbitrary"), name="dft_stride_stage",
    )(*xs4, *tabs)


def _spec_mid_body(g_ref, m3_ref, m3i_ref, k_ref, hr_ref, hi_ref):
    bp, q = hr_ref.shape[0], hr_ref.shape[1]
    c = hr_ref.shape[2] // DFT_CB
    for j in range(DFT_CB):
        kh = k_ref[j]
        kr, ki = kh[:q], kh[q:]
        for b in range(bp):
            x = _dot_dft(m3_ref[...], g_ref[b, :, j].reshape(2 * q, -1))
            xr, xi = x[:q], x[q:]
            y = jnp.concatenate([xr * kr - xi * ki, xr * ki + xi * kr], axis=0)
            h = _dot_dft(m3i_ref[...], y)
            hr_ref[b, :, j * c:(j + 1) * c] = h[:q].astype(BF16)
            hi_ref[b, :, j * c:(j + 1) * c] = h[q:].astype(BF16)


def _spec_mid(gf, m3, m3i, khat, p, q, c=HY_WIDTH):
    bp = gf.shape[0]
    g5 = gf.reshape(bp, 2, p, q, c)
    cb = DFT_CB
    return pl.pallas_call(
        _spec_mid_body,
        grid=(p // cb,),
        in_specs=[pl.BlockSpec((bp, 2, cb, q, c), lambda ci: (0, 0, ci, 0, 0)), _full((2 * q, 2 * q)),
                  _full((2 * q, 2 * q)), pl.BlockSpec((cb, 2 * q, c), lambda ci: (ci, 0, 0))],
        out_specs=[pl.BlockSpec((bp, q, cb * c), lambda ci: (0, 0, ci))] * 2,
        out_shape=[jax.ShapeDtypeStruct((bp, q, p * c), BF16)] * 2,
        compiler_params=_cparams("arbitrary"), name="dft_mid",
    )(g5, m3, m3i, khat)


def _filter_spec_body(g_ref, m3k_ref, s_ref, k_ref):
    q4 = m3k_ref.shape[1]
    inv = 1.0 / s_ref[...]
    for j in range(DFT_CB):
        k_ref[j] = _dot_dft(m3k_ref[...], g_ref[0, :, j].reshape(q4, -1)) * inv


def _filter_spec(gk, m3k, s, p, q, c=HY_WIDTH):
    g5 = gk.reshape(1, 4, p, q, c)
    cb = DFT_CB
    return pl.pallas_call(
        _filter_spec_body,
        grid=(p // cb,),
        in_specs=[pl.BlockSpec((1, 4, cb, q, c), lambda ci: (0, 0, ci, 0, 0)), _full((2 * q, 4 * q)), _full((1, c))],
        out_specs=pl.BlockSpec((cb, 2 * q, c), lambda ci: (ci, 0, 0)),
        out_shape=jax.ShapeDtypeStruct((p, 2 * q, c), F32),
        compiler_params=_cparams("arbitrary"), name="filter_spectrum",
    )(g5, m3k, s)


def _direct_conv_body(z_ref, h0_ref, h1_ref, s_ref, tf_ref, tk_ref, ti_ref, o_ref):
    n2 = tf_ref.shape[0] // 2
    kh = _dot_dft(tk_ref[...], jnp.concatenate([h0_ref[0], h1_ref[0]], axis=0)) / s_ref[...]
    x = _dot_dft(tf_ref[...], z_ref[0])
    xr, xi, kr, ki = x[:n2], x[n2:], kh[:n2], kh[n2:]
    y = jnp.concatenate([xr * kr - xi * ki, xr * ki + xi * kr], axis=0)
    o_ref[0] = _dot_dft(ti_ref[...], y)


def _direct_conv(z, h0, h1, s, n, c=HY_WIDTH):
    bp = z.shape[0] // (2 * n)
    tf, tk, ti = (jnp.asarray(t, BF16) for t in _direct_tables(n))
    out = pl.pallas_call(
        _direct_conv_body,
        grid=(bp,),
        in_specs=[pl.BlockSpec((1, 2 * n, c), lambda b: (b, 0, 0)), _full((1, n, c)), _full((1, n, c)), _full((1, c)),
                  _full(tf.shape), _full(tk.shape), _full(ti.shape)],
        out_specs=pl.BlockSpec((1, 2 * n, c), lambda b: (b, 0, 0)),
        out_shape=jax.ShapeDtypeStruct((bp, 2 * n, c), F32),
        compiler_params=_cparams("arbitrary"), name="direct_conv",
    )(z.reshape(bp, 2 * n, c), h0, h1, s, tf, tk, ti)
    return out.reshape(bp * 2 * n, c)


def _cis(m, n_total):
    ang = (2.0 * np.pi / n_total) * (m % n_total).astype(np.float64)
    return np.cos(ang), -np.sin(ang)


@functools.lru_cache(maxsize=None)
def _direct_tables(n):
    nn = 2 * n
    f = np.arange(nn)[:, None]
    t = np.arange(n)[None, :]
    cr, ci = _cis(f * t, nn)
    tf = np.block([[cr, -ci], [ci, cr]])
    tk = np.block([[cr, cr], [ci, -ci]])
    ti = np.block([[cr.T, ci.T], [-ci.T, cr.T]]) / nn
    return tuple(a.astype(np.float32) for a in (tf, tk, ti))


@functools.lru_cache(maxsize=None)
def _twostage_tables(p, q):
    nn = p * q
    hp = p // 2
    s = np.arange(q)[:, None, None]
    c = np.arange(p)[None, :, None]
    a = np.arange(hp)[None, None, :]
    tr, ti = _cis(c * (q * a + s), nn)
    z = np.zeros_like(tr)
    t_data = np.concatenate([np.concatenate([tr, -ti], 2), np.concatenate([ti, tr], 2)], 1)
    t_k0 = np.concatenate([tr, ti, z, z], 1)
    t_k1 = np.concatenate([z, z, tr, -ti], 1)
    trt, tit = np.swapaxes(tr, 1, 2) / nn, np.swapaxes(ti, 1, 2) / nn
    t_inv_r = np.concatenate([trt, -tit], 1)
    t_inv_i = np.concatenate([tit, trt], 1)
    d = np.arange(q)[:, None]
    b = np.arange(q)[None, :]
    fr, fi = _cis(d * b, q)
    m3 = np.block([[fr, -fi], [fi, fr]])
    m3i = np.block([[fr, fi], [-fi, fr]])
    m3k = np.block([[fr, -fi, fr, fi], [fi, fr, -fi, fr]])
    return tuple(x.astype(np.float32) for x in (t_data, t_k0, t_k1, t_inv_r, t_inv_i, m3, m3i, m3k))


def _long_conv_spectrum(n, lw):
    p, q = DFT_P, DFT_Q
    h0, h1, s = _hyena_filter(n, q, lw)
    _, t_k0, t_k1, _, _, _, _, m3k = _twostage_tables(p, q)
    gk = _colslot_matmul([h0, h1], [t_k0, t_k1], BF16)
    return _filter_spec(gk, jnp.asarray(m3k, BF16), s, p, q)


def _long_conv(zt, khat):
    p, q = DFT_P, DFT_Q
    t_data, _, _, t_inv_r, t_inv_i, m3, m3i, _ = _twostage_tables(p, q)
    gf = _colslot_matmul([zt], [t_data], BF16)
    hr, hi = _spec_mid(gf, jnp.asarray(m3, BF16), jnp.asarray(m3i, BF16), khat, p, q)
    y = _colslot_matmul([hr, hi], [t_inv_r, t_inv_i], BF16)
    return y.reshape(-1, HY_WIDTH)


def _merge_body(tiles_per_group, fixed_group, x_ref, g1_ref, oa_ref, conv_ref, z_ref, x0_ref, gate_ref, bias_ref,
                wba_ref, wbh_ref, wout_ref, o_ref):
    group = fixed_group if fixed_group is not None else pl.program_id(0) // tiles_per_group
    o_hy = ((conv_ref[...] + z_ref[...] * bias_ref[...]) * x0_ref[...]).astype(BF16)
    d = D_MODEL
    merged = gate_ref[:, :d] * _dot(oa_ref[...], wba_ref[...]) + gate_ref[:, d:] * _dot(o_hy, wbh_ref[...])
    mix = _dot(merged.astype(BF16), wout_ref[...])
    o_ref[...] = x_ref[...] + _mod_row(g1_ref, group) * mix


def _merge(x, mod, o_attn, conv, z, x0, gates, lw, *, tiles_per_group, fixed_group):
    rows = x.shape[0]
    row_spec = lambda n: pl.BlockSpec((ROW_TILE, n), lambda t: (t, 0))
    c = HY_WIDTH
    return pl.pallas_call(
        functools.partial(_merge_body, tiles_per_group, fixed_group),
        grid=(rows // ROW_TILE,),
        in_specs=[row_spec(D_MODEL), _mod_spec(2), row_spec(N_HEADS * V_HEAD), row_spec(c), row_spec(c), row_spec(c),
                  row_spec(2 * D_MODEL), _full((1, c)), _full(lw["w_br_attn"].shape), _full(lw["w_br_hy"].shape),
                  _full(lw["w_out"].shape)],
        out_specs=row_spec(D_MODEL),
        out_shape=jax.ShapeDtypeStruct((rows, D_MODEL), F32),
        compiler_params=_cparams("arbitrary"), name="merge_out",
    )(x, mod, o_attn, conv, z, x0, gates, lw["hy_bias"], lw["w_br_attn"], lw["w_br_hy"], lw["w_out"])


def _swiglu(h, wg_ref, wu_ref, wd_ref):
    acc = None
    for c0, c1 in FFN_SPANS:
        gate = _dot(h, wg_ref[0, :, c0:c1])
        up = _dot(h, wu_ref[0, :, c0:c1])
        y = _dot((gate * jax.nn.sigmoid(gate) * up).astype(BF16), wd_ref[0, c0:c1, :])
        acc = y if acc is None else acc + y
    return acc


def _ffn_body(tiles_per_group, fixed_group, x_ref, sh_ref, sc_ref, g2_ref, n2_ref, wg_ref, wu_ref, wd_ref, o_ref):
    group = fixed_group if fixed_group is not None else pl.program_id(0) // tiles_per_group
    x = x_ref[...]
    h = (_rms(x, n2_ref[...]) * (1.0 + _mod_row(sc_ref, group)) + _mod_row(sh_ref, group)).astype(BF16)
    o_ref[...] = x + _mod_row(g2_ref, group) * _swiglu(h, wg_ref, wu_ref, wd_ref)


def _ffn(x, mod, norm_g, w_gate, w_up, w_down, *, tiles_per_group, fixed_group):
    rows = x.shape[0]
    tm = min(FFN_ROW_TILE, rows)
    row_spec = pl.BlockSpec((tm, D_MODEL), lambda t: (t, 0))
    return pl.pallas_call(
        functools.partial(_ffn_body, tiles_per_group * ROW_TILE // tm if tiles_per_group else None, fixed_group),
        grid=(rows // tm,),
        in_specs=[row_spec, _mod_spec(3), _mod_spec(4), _mod_spec(5), _full((1, D_MODEL)),
                  _full(w_gate.shape), _full(w_up.shape), _full(w_down.shape)],
        out_specs=row_spec,
        out_shape=jax.ShapeDtypeStruct((rows, D_MODEL), F32),
        compiler_params=_cparams("arbitrary"), name="ffn",
    )(x, mod, mod, mod, norm_g, w_gate, w_up, w_down)


def _route_body(tiles_per_group, x_ref, sh_ref, sc_ref, n2_ref, router_ref, h_ref, route_ref):
    group = pl.program_id(0) // tiles_per_group
    h = _rms(x_ref[...], n2_ref[...]) * (1.0 + _mod_row(sc_ref, group)) + _mod_row(sh_ref, group)
    h_ref[...] = h
    logits = _dot_hi(h, router_ref[...])
    lane = lax.broadcasted_iota(jnp.int32, logits.shape, 1)
    logits = jnp.where(lane < N_EXPERTS, logits, -jnp.inf)
    m1 = jnp.max(logits, axis=-1, keepdims=True)
    i1 = jnp.min(jnp.where(logits == m1, lane, LANES), axis=-1, keepdims=True)
    rest = jnp.where(lane == i1, -jnp.inf, logits)
    m2 = jnp.max(rest, axis=-1, keepdims=True)
    i2 = jnp.min(jnp.where(rest == m2, lane, LANES), axis=-1, keepdims=True)
    e2 = jnp.exp(m2 - m1)
    w1 = 1.0 / (1.0 + e2)
    route_ref[...] = (jnp.where(lane == 0, i1.astype(F32), 0.0) + jnp.where(lane == 1, i2.astype(F32), 0.0)
                      + jnp.where(lane == 2, w1, 0.0) + jnp.where(lane == 3, e2 * w1, 0.0))


def _route(x, mod, norm_g, router, tiles_per_group):
    rows = x.shape[0]
    row_spec = lambda n: pl.BlockSpec((ROW_TILE, n), lambda t: (t, 0))
    return pl.pallas_call(
        functools.partial(_route_body, tiles_per_group),
        grid=(rows // ROW_TILE,),
        in_specs=[row_spec(D_MODEL), _mod_spec(3), _mod_spec(4), _full((1, D_MODEL)), _full(router.shape)],
        out_specs=[row_spec(D_MODEL), row_spec(LANES)],
        out_shape=[jax.ShapeDtypeStruct((rows, D_MODEL), F32), jax.ShapeDtypeStruct((rows, LANES), F32)],
        compiler_params=_cparams("arbitrary"), name="moe_route",
    )(x, mod, mod, norm_g, router)


def _gather_rows(idx_ref, n_rows, src_hbm, dst, sem):
    def issue(i, carry):
        for prio in range(2):
            r = 2 * i + prio
            pltpu.make_async_copy(src_hbm.at[pl.ds(idx_ref[0, 0, r], 1), :], dst.at[pl.ds(r, 1), :],
                                  sem).start(priority=prio)
        return carry
    lax.fori_loop(0, n_rows // 2, issue, 0, unroll=8)


def _wait_rows(n_rows, src_hbm, dst, sem):
    pltpu.make_async_copy(src_hbm.at[pl.ds(0, n_rows), :], dst, sem).wait()


def _moe_dispatch_body(idx_ref, h_ref, xs_in_hbm, xs_hbm, stage, sem):
    del xs_in_hbm
    t = pl.program_id(0)
    n_tiles = pl.num_programs(0)
    tm = h_ref.shape[0]
    slot = t % 2

    def wait(s):
        for _ in range(2):
            pltpu.make_async_copy(stage.at[s], xs_hbm.at[pl.ds(0, tm), :], sem.at[s]).wait()

    @pl.when(t >= 2)
    def _():
        wait(slot)

    stage[slot] = h_ref[...]

    def issue(r, carry):
        row = stage.at[slot, pl.ds(r, 1), :]
        pltpu.make_async_copy(row, xs_hbm.at[pl.ds(idx_ref[0, 0, r], 1), :], sem.at[slot]).start(priority=0)
        pltpu.make_async_copy(row, xs_hbm.at[pl.ds(idx_ref[0, 0, tm + r], 1), :], sem.at[slot]).start(priority=1)
        return carry
    lax.fori_loop(0, tm, issue, 0, unroll=8)

    @pl.when(t == n_tiles - 1)
    def _():
        wait(slot)
        wait(1 - slot)


def _tile_index_blocks(dest, tm):
    n_tiles = dest.shape[0] // tm
    return dest.reshape(n_tiles, tm, 2).transpose(0, 2, 1).reshape(n_tiles, 1, 2 * tm)


def _moe_dispatch(h, dest, n_rows):
    tm = ROW_TILE
    idx = _tile_index_blocks(dest, tm)
    n_tiles = idx.shape[0]
    assert n_tiles >= 2
    return pl.pallas_call(
        _moe_dispatch_body,
        grid=(n_tiles,),
        in_specs=[pl.BlockSpec((1, 1, 2 * tm), lambda t: (t, 0, 0), memory_space=pltpu.SMEM),
                  pl.BlockSpec((tm, D_MODEL), lambda t: (t, 0)), pl.BlockSpec(memory_space=pl.ANY)],
        out_specs=pl.BlockSpec(memory_space=pl.ANY),
        out_shape=jax.ShapeDtypeStruct((n_rows, D_MODEL), F32),
        scratch_shapes=[pltpu.VMEM((2, tm, D_MODEL), F32), pltpu.SemaphoreType.DMA((2,))],
        input_output_aliases={2: 0},
        compiler_params=_cparams("arbitrary"), name="moe_dispatch",
    )(idx, h, jnp.zeros((n_rows, D_MODEL), F32))


def _moe_group_body(te_ref, nu_ref, x_ref, wg_ref, wu_ref, wd_ref, y_ref):
    t = pl.program_id(0)

    @pl.when(t < nu_ref[0])
    def _():
        y_ref[...] = _swiglu(x_ref[...].astype(BF16), wg_ref, wu_ref, wd_ref)

    @pl.when(t >= nu_ref[0])
    def _():
        y_ref[...] = jnp.zeros_like(y_ref)


def _moe_group(xs, tile_expert, n_used, w_gate, w_up, w_down):
    tm = MOE_ROW_TILE
    n_rows = xs.shape[0]
    w_spec = lambda w: pl.BlockSpec((1,) + w.shape[1:], lambda t, te, nu: (te[t], 0, 0))
    grid_spec = pltpu.PrefetchScalarGridSpec(
        num_scalar_prefetch=2,
        grid=(n_rows // tm,),
        in_specs=[pl.BlockSpec((tm, D_MODEL), lambda t, te, nu: (jnp.minimum(t, nu[0] - 1), 0)),
                  w_spec(w_gate), w_spec(w_up), w_spec(w_down)],
        out_specs=pl.BlockSpec((tm, D_MODEL), lambda t, te, nu: (t, 0)),
    )
    return pl.pallas_call(
        _moe_group_body, grid_spec=grid_spec,
        out_shape=jax.ShapeDtypeStruct((n_rows, D_MODEL), F32),
        compiler_params=_cparams("arbitrary"), name="moe_group",
    )(tile_expert, n_used, xs, w_gate, w_up, w_down)


def _moe_combine_body(tiles_per_group, idx_ref, idx_next_ref, x_ref, g2_ref, fg_ref, route_ref, y_hbm, o_ref,
                      yg_scr, sem):
    t = pl.program_id(0)
    n_tiles = pl.num_programs(0)
    group = t // tiles_per_group
    tm = x_ref.shape[0]
    slot = t % 2

    @pl.when(t == 0)
    def _():
        _gather_rows(idx_ref, 2 * tm, y_hbm, yg_scr.at[0], sem.at[0])

    _wait_rows(2 * tm, y_hbm, yg_scr.at[slot], sem.at[slot])

    @pl.when(t + 1 < n_tiles)
    def _():
        _gather_rows(idx_next_ref, 2 * tm, y_hbm, yg_scr.at[1 - slot], sem.at[1 - slot])

    route = route_ref[...]
    y = route[:, 2:3] * yg_scr[slot, :tm] + route[:, 3:4] * yg_scr[slot, tm:]
    o_ref[...] = _rms(x_ref[...] + _mod_row(g2_ref, group) * y, fg_ref[...])


def _moe_combine(x, mod, final_g, route, y_sorted, dest, tiles_per_group):
    rows = x.shape[0]
    tm = ROW_TILE
    idx = _tile_index_blocks(dest, tm)
    n_tiles = idx.shape[0]
    return pl.pallas_call(
        functools.partial(_moe_combine_body, tiles_per_group),
        grid=(n_tiles,),
        in_specs=[pl.BlockSpec((1, 1, 2 * tm), lambda t: (t, 0, 0), memory_space=pltpu.SMEM),
                  pl.BlockSpec((1, 1, 2 * tm), lambda t: (jnp.minimum(t + 1, n_tiles - 1), 0, 0),
                               memory_space=pltpu.SMEM),
                  pl.BlockSpec((tm, D_MODEL), lambda t: (t, 0)), _mod_spec(5), _full((1, D_MODEL)),
                  pl.BlockSpec((tm, LANES), lambda t: (t, 0)), pl.BlockSpec(memory_space=pl.ANY)],
        out_specs=pl.BlockSpec((tm, D_MODEL), lambda t: (t, 0)),
        out_shape=jax.ShapeDtypeStruct((rows, D_MODEL), F32),
        scratch_shapes=[pltpu.VMEM((2, 2 * tm, D_MODEL), F32), pltpu.SemaphoreType.DMA((2,))],
        compiler_params=_cparams("arbitrary"), name="moe_combine",
    )(idx, idx, x, mod, final_g, route, y_sorted)


def _moe_plan(route, tm):
    t = route.shape[0]
    experts = jnp.concatenate([route[:, 0], route[:, 1]]).astype(jnp.int32)
    onehot = (experts[:, None] == jnp.arange(N_EXPERTS, dtype=jnp.int32)[None, :]).astype(jnp.int32)
    csum = jnp.cumsum(onehot, axis=0)
    rank = jnp.sum(csum * onehot, axis=1) - 1
    padded = (csum[-1] + tm - 1) // tm * tm
    ends = jnp.cumsum(padded)
    dest = jnp.sum((ends - padded)[None, :] * onehot, axis=1) + rank
    n_rows = 2 * t + N_EXPERTS * tm
    tile_start = jnp.arange(n_rows // tm, dtype=jnp.int32) * tm
    tile_expert = jnp.minimum(jnp.sum((tile_start[:, None] >= ends[None, :]).astype(jnp.int32), axis=1), N_EXPERTS - 1)
    n_used = (ends[-1:] // tm).astype(jnp.int32)
    return dest.reshape(2, t).T, tile_expert, n_used, n_rows


def _moe(x, mod, norm_g, router, w_gate, w_up, w_down, final_g, tiles_per_group):
    h, route = _route(x, mod, norm_g, router, tiles_per_group)
    dest, tile_expert, n_used, n_rows = _moe_plan(route, MOE_ROW_TILE)
    xs = _moe_dispatch(h, dest, n_rows)
    y_sorted = _moe_group(xs, tile_expert, n_used, w_gate, w_up, w_down)
    return _moe_combine(x, mod, final_g, route, y_sorted, dest, tiles_per_group)


def _pad_heads(w, width):
    k = w.shape[0]
    w = w.reshape(k, N_HEADS, width)
    return jnp.pad(w, ((0, 0), (0, 0), (0, HEAD_PAD - width))).reshape(k, N_HEADS * HEAD_PAD)


def _rot_cols(w):
    half = QK_ROPE // 2
    return jnp.concatenate([-w[..., half:], w[..., :half]], axis=-1)


def _layer_weights(p, layer):
    w_in = p["w_in"][layer]
    w_uq = p["w_uq"][layer].reshape(Q_LORA, N_HEADS, QK_NOPE + QK_ROPE)
    zeros_nope = jnp.zeros((Q_LORA, N_HEADS, QK_NOPE), F32)
    w_uq_b = jnp.concatenate([zeros_nope, _rot_cols(w_uq[..., QK_NOPE:])], axis=-1)
    w_kr = w_in[:, KV_START + KV_LORA:HY_START]
    zk = jnp.zeros((D_MODEL, QK_NOPE), F32)
    zp = jnp.zeros((D_MODEL, HEAD_PAD - QK_NOPE - QK_ROPE), F32)
    w_kr2 = jnp.concatenate([zk, w_kr, zp, zk, _rot_cols(w_kr), zp], axis=-1)
    row = lambda v: v.reshape(1, -1)
    bf = lambda v: v.astype(BF16)
    return {
        "norm1_g": row(p["norm1_g"][layer]), "norm2_g": row(p["norm2_g"][layer]),
        "w_q": bf(w_in[:, :Q_LORA]), "q_norm_g": row(p["q_norm_g"][layer]),
        "w_uq_a": bf(_pad_heads(w_uq.reshape(Q_LORA, -1), QK_NOPE + QK_ROPE)),
        "w_uq_b": bf(_pad_heads(w_uq_b.reshape(Q_LORA, -1), QK_NOPE + QK_ROPE)),
        "w_kv": bf(w_in[:, KV_START:KV_START + KV_LORA]), "kv_norm_g": row(p["kv_norm_g"][layer]),
        "w_uk": bf(_pad_heads(p["w_uk"][layer], QK_NOPE)), "w_uv": bf(_pad_heads(p["w_uv"][layer], V_HEAD).T),
        "w_kr": bf(w_kr2),
        "w_hy": bf(w_in[:, HY_START:GATE_START]), "w_gate": bf(w_in[:, GATE_START:]),
        "hy_short_w": p["hy_short_w"][layer], "hy_short_b": p["hy_short_b"][layer],
        "hy_w1": p["hy_w1"][layer], "hy_b1": p["hy_b1"][layer], "hy_w2": p["hy_w2"][layer],
        "hy_b2": p["hy_b2"][layer], "hy_w3": p["hy_w3"][layer], "hy_freq": p["hy_freq"][layer],
        "hy_decay": p["hy_decay"][layer], "hy_bias": row(p["hy_bias"][layer]),
        "w_br_attn": bf(p["w_br_attn"][layer]), "w_br_hy": bf(p["w_br_hy"][layer]), "w_out": bf(p["w_out"][layer]),
    }


def _rope_tables(n):
    rows = n // GRID_W
    n_freq = QK_ROPE // 4
    inv_freq = ROPE_BASE ** (-jnp.arange(n_freq, dtype=F32) / n_freq)
    r = jnp.repeat(jnp.arange(rows, dtype=F32), GRID_W)
    col = jnp.tile(jnp.arange(GRID_W, dtype=F32), rows)
    ang = jnp.concatenate([r[:, None] * inv_freq, col[:, None] * inv_freq], axis=-1)
    cos, sin = jnp.cos(ang), jnp.sin(ang)
    ones = jnp.ones((n, QK_NOPE), F32)
    zeros = jnp.zeros((n, QK_NOPE), F32)
    pad = jnp.zeros((n, HEAD_PAD - QK_NOPE - QK_ROPE), F32)
    return (jnp.concatenate([ones, cos, cos, pad], axis=-1), jnp.concatenate([zeros, sin, sin, pad], axis=-1))


def _token_mixer(xs, mod, lw, khat, conv_short, *, batch, n, tiles_per_group, fixed_group, rope_tabs,
                 k_ctx=None, v_ctx=None):
    pr = _inproj(xs, mod, lw, tiles_per_group=tiles_per_group, fixed_group=fixed_group, rope_tabs=rope_tabs,
                 want_q=True, want_hg=True)
    k3 = pr["k"].reshape(batch, n, -1)
    v3 = pr["v"].reshape(batch, n // VT_BLOCK, N_HEADS * HEAD_PAD, VT_BLOCK)
    if k_ctx is None:
        o_attn = _attention(pr["q"], k3, v3, None, None, batch)
    else:
        o_attn = _attention(pr["q"], k_ctx, v_ctx, k3, v3, batch)
    if conv_short:
        z, x0 = _hyena_pre(pr["p_hy"], lw["hy_short_w"], lw["hy_short_b"], batch, None)
        conv = _direct_conv(z, *khat, n)
    else:
        z, x0, zt = _hyena_pre(pr["p_hy"], lw["hy_short_w"], lw["hy_short_b"], batch, DFT_Q)
        conv = _long_conv(zt, khat)
    x_new = _merge(xs, mod, o_attn, conv, z, x0, pr["gates"], lw, tiles_per_group=tiles_per_group,
                   fixed_group=fixed_group)
    return x_new, k3, v3


def kernel(x, c, ctx, c_ctx, w_mod, b_mod, norm1_g, norm2_g, w_in, q_norm_g, kv_norm_g, w_uq, w_uk, w_uv, hy_short_w, hy_short_b, hy_w1, hy_b1, hy_w2, hy_b2, hy_w3, hy_freq, hy_decay, hy_bias, w_br_attn, w_br_hy, w_out, ffn_w_gate, ffn_w_up, ffn_w_down, moe_router, moe_w_gate, moe_w_up, moe_w_down, final_g):
    p = dict(w_in=w_in, norm1_g=norm1_g, norm2_g=norm2_g, q_norm_g=q_norm_g, kv_norm_g=kv_norm_g, w_uq=w_uq,
             w_uk=w_uk, w_uv=w_uv, hy_short_w=hy_short_w, hy_short_b=hy_short_b, hy_w1=hy_w1, hy_b1=hy_b1,
             hy_w2=hy_w2, hy_b2=hy_b2, hy_w3=hy_w3, hy_freq=hy_freq, hy_decay=hy_decay, hy_bias=hy_bias,
             w_br_attn=w_br_attn, w_br_hy=w_br_hy, w_out=w_out)
    batch, seq, d = x.shape
    ctx_len = ctx.shape[1]
    depth = w_mod.shape[0]
    ctx_group = batch
    cond8 = jnp.zeros((SUBLANES, d), F32).at[:batch].set(c).at[ctx_group].set(c_ctx)
    rope_tabs = _rope_tables(seq)
    lat_tiles = seq // ROW_TILE
    xs = x.reshape(batch * seq, d)
    cs = ctx.reshape(batch * ctx_len, d)
    bf = lambda v: v.astype(BF16)
    for layer in range(depth):
        last = layer == depth - 1
        lw = _layer_weights(p, layer)
        mod = _adaln(cond8, w_mod[layer], b_mod[layer])
        khat_lat = _long_conv_spectrum(seq, lw)
        if last:
            pr = _inproj(cs, mod, lw, tiles_per_group=None, fixed_group=ctx_group, rope_tabs=None, want_q=False,
                         want_hg=False)
            k_ctx = pr["k"].reshape(batch, ctx_len, -1)
            v_ctx = pr["v"].reshape(batch, ctx_len // VT_BLOCK, N_HEADS * HEAD_PAD, VT_BLOCK)
        else:
            khat_ctx = _hyena_filter(ctx_len, ctx_len, lw)
            cs_mid, k_ctx, v_ctx = _token_mixer(cs, mod, lw, khat_ctx, True, batch=batch, n=ctx_len,
                                                tiles_per_group=None, fixed_group=ctx_group, rope_tabs=None)
        xs, _, _ = _token_mixer(xs, mod, lw, khat_lat, False, batch=batch, n=seq, tiles_per_group=lat_tiles,
                                fixed_group=None, rope_tabs=rope_tabs, k_ctx=k_ctx, v_ctx=v_ctx)
        i = layer // 2
        n2 = lw["norm2_g"]
        if layer % 2 == 0:
            assert not last
            wg, wu, wd = bf(ffn_w_gate[i])[None], bf(ffn_w_up[i])[None], bf(ffn_w_down[i])[None]
            xs = _ffn(xs, mod, n2, wg, wu, wd, tiles_per_group=lat_tiles, fixed_group=None)
            cs = _ffn(cs_mid, mod, n2, wg, wu, wd, tiles_per_group=None, fixed_group=ctx_group)
        else:
            assert last
            router = jnp.pad(moe_router[i], ((0, 0), (0, LANES - N_EXPERTS)))
            xs = _moe(xs, mod, n2, router, bf(moe_w_gate[i]), bf(moe_w_up[i]), bf(moe_w_down[i]),
                      final_g.reshape(1, d), lat_tiles)
    return xs.reshape(batch, seq, d)
```

```python
import functools
import math

import numpy as np
import jax
import jax.numpy as jnp
from jax import lax
from jax.experimental import pallas as pl
from jax.experimental.pallas import tpu as pltpu

F32 = jnp.float32
BF16 = jnp.bfloat16
HIGHEST = lax.Precision.HIGHEST

D_MODEL = 1024
GRID_W = 64
EPS = 1e-6
N_HEADS = 8
Q_LORA = 384
KV_LORA = 256
QK_NOPE = 64
QK_ROPE = 32
V_HEAD = 64
ROPE_BASE = 10000.0
ATTN_SCALE = (QK_NOPE + QK_ROPE) ** -0.5
HY_WIDTH = 512
HY_EMB = 17
HY_HIDDEN = 64
D_FF = 2816
N_EXPERTS = 8
KV_START = Q_LORA
HY_START = KV_START + KV_LORA + QK_ROPE
GATE_START = HY_START + 3 * HY_WIDTH

LANES = 128
SUBLANES = 8
HALO_ROWS = 16
HEAD_PAD = LANES
VMEM_LIMIT = 56 * 2**20

ROW_TILE = 256
INPROJ_ROW_TILE = 512
STREAM_ROW_TILE = 512
FFN_ROW_TILE = 512
MXU_TILE = 256
FFN_SPANS = ((0, 4 * MXU_TILE), (4 * MXU_TILE, 8 * MXU_TILE), (8 * MXU_TILE, D_FF))
MOE_ROW_TILE = 256
VT_BLOCK = ROW_TILE
ATTN_KV = 512
ATTN_Q_TILE = 1024
ATTN_HEAD_GROUPS = ((0, 1),)
Q_SCALE = ATTN_SCALE * math.log2(math.e)
DFT_P = 64
DFT_Q = 128
COLSLOT_BQ = 16
DFT_CB = 4
FILTER_CHUNKS = 4


def _cparams(*sem):
    return pltpu.CompilerParams(dimension_semantics=sem, vmem_limit_bytes=VMEM_LIMIT)


def _dot(a, b):
    return jnp.dot(a, b, preferred_element_type=F32)


def _dot_hi(a, b):
    return jnp.dot(a, b, precision=HIGHEST, preferred_element_type=F32)


def _dot_dft(table, x):
    return _dot(table, x.astype(BF16))


def _rms(xf, g):
    return xf * lax.rsqrt(jnp.mean(xf * xf, axis=-1, keepdims=True) + EPS) * g


def _full(shape):
    nd = len(shape)
    return pl.BlockSpec(shape, lambda *_: (0,) * nd, pipeline_mode=pl.Buffered(1))


def _adaln_body(c_ref, w_ref, b_ref, o_ref):
    c = c_ref[...]
    o_ref[...] = _dot_hi(c * jax.nn.sigmoid(c), w_ref[0]) + b_ref[0]


def _adaln(cond8, w, b, layer):
    n_layers, d, n = w.shape
    return pl.pallas_call(
        _adaln_body,
        grid=(n // d,),
        in_specs=[_full((SUBLANES, d)), pl.BlockSpec((1, d, d), lambda j: (layer, 0, j)),
                  pl.BlockSpec((1, 1, d), lambda j: (layer, 0, j))],
        out_specs=pl.BlockSpec((SUBLANES, d), lambda j: (0, j)),
        out_shape=jax.ShapeDtypeStruct((SUBLANES, n), F32),
        compiler_params=_cparams("arbitrary"),
        name="adaln",
    )(cond8, w, b.reshape(n_layers, 1, n))


def _mod_spec(chunk):
    return pl.BlockSpec((SUBLANES, D_MODEL), lambda t, *_: (0, chunk))


def _scale_tiles(tiles_per_group, tm):
    return None if tiles_per_group is None else tiles_per_group * ROW_TILE // tm


def _mod_row(ref, group):
    return ref[pl.ds(group, 1), :]


def _inproj_body(tiles_per_group, fixed_group, use_rope, want_q, want_hg, *refs):
    it = iter(refs)
    x_ref, sh_ref, sc_ref, g1_ref = next(it), next(it), next(it), next(it)
    wkv_ref, kvg_ref, wuk_ref, wuv_ref, wkr_ref = next(it), next(it), next(it), next(it), next(it)
    if want_q:
        wq_ref, qg_ref, wuqa_ref, wuqb_ref = next(it), next(it), next(it), next(it)
    if want_hg:
        why_ref, wgate_ref = next(it), next(it)
    if use_rope:
        cos_ref, sin_ref = next(it), next(it)
    k_out, v_out = next(it), next(it)
    if want_q:
        q_out = next(it)
    if want_hg:
        phy_out, gate_out = next(it), next(it)

    group = fixed_group if fixed_group is not None else pl.program_id(0) // tiles_per_group
    xf = x_ref[...]
    h = _rms(xf, g1_ref[...]) * (1.0 + _mod_row(sc_ref, group)) + _mod_row(sh_ref, group)
    h = h.astype(BF16)
    if use_rope:
        cos, sin = cos_ref[...], sin_ref[...]

    def rope(a, b):
        return a * cos + b * sin if use_rope else a

    ckv = _rms(_dot(h, wkv_ref[...]), kvg_ref[...]).astype(BF16)
    vt = lax.dot_general(wuv_ref[...], ckv, (((1,), (1,)), ((), ())), preferred_element_type=F32)
    vrow = lax.broadcasted_iota(jnp.int32, vt.shape, 0) & (HEAD_PAD - 1)
    vt = jnp.where(vrow == V_HEAD, 1.0, vt).astype(BF16)
    for j in range(v_out.shape[0]):
        v_out[j] = vt[:, j * VT_BLOCK:(j + 1) * VT_BLOCK]
    k_nope = _dot(ckv, wuk_ref[...])
    kr = _dot(h, wkr_ref[...])
    k_rope = rope(kr[:, :HEAD_PAD], kr[:, HEAD_PAD:])
    for hd in range(N_HEADS):
        sl = slice(hd * HEAD_PAD, (hd + 1) * HEAD_PAD)
        k_out[:, sl] = (k_nope[:, sl] + k_rope).astype(BF16)
    if want_q:
        qn = _rms(_dot(h, wq_ref[...]), qg_ref[...]).astype(BF16)
        qa = _dot(qn, wuqa_ref[...])
        qb = _dot(qn, wuqb_ref[...]) if use_rope else None
        for hd in range(N_HEADS):
            sl = slice(hd * HEAD_PAD, (hd + 1) * HEAD_PAD)
            q_out[:, sl] = (rope(qa[:, sl], None if qb is None else qb[:, sl]) * Q_SCALE).astype(BF16)
    if want_hg:
        n_hy = why_ref.shape[1]
        for c0 in range(0, n_hy, 512):
            phy_out[:, c0:c0 + 512] = _dot(h, why_ref[:, c0:c0 + 512]).astype(BF16)
        n_g = wgate_ref.shape[1]
        for c0 in range(0, n_g, 512):
            gate_out[:, c0:c0 + 512] = jax.nn.sigmoid(_dot(h, wgate_ref[:, c0:c0 + 512])).astype(BF16)


def _inproj(x, mod, lw, *, tiles_per_group, fixed_group, rope_tabs, want_q, want_hg):
    rows = x.shape[0]
    tm = INPROJ_ROW_TILE
    nt = rows // tm
    if tiles_per_group is not None:
        tiles_per_group = tiles_per_group * ROW_TILE // tm
    use_rope = rope_tabs is not None
    row_spec = lambda n: pl.BlockSpec((tm, n), lambda t: (t, 0))
    ins = [x, mod, mod, lw["norm1_g"], lw["w_kv"], lw["kv_norm_g"], lw["w_uk"], lw["w_uv"], lw["w_kr"]]
    specs = [row_spec(D_MODEL), _mod_spec(0), _mod_spec(1), _full((1, D_MODEL)),
             _full(lw["w_kv"].shape), _full((1, KV_LORA)), _full(lw["w_uk"].shape), _full(lw["w_uv"].shape),
             _full(lw["w_kr"].shape)]
    if want_q:
        ins += [lw["w_q"], lw["q_norm_g"], lw["w_uq_a"], lw["w_uq_b"]]
        specs += [_full(lw["w_q"].shape), _full((1, Q_LORA)), _full(lw["w_uq_a"].shape), _full(lw["w_uq_b"].shape)]
    if want_hg:
        ins += [lw["w_hy"], lw["w_gate"]]
        specs += [_full(lw["w_hy"].shape), _full(lw["w_gate"].shape)]
    if use_rope:
        seq_tiles = rope_tabs[0].shape[0] // tm
        ins += list(rope_tabs)
        specs += [pl.BlockSpec((tm, HEAD_PAD), lambda t: (t % seq_tiles, 0))] * 2
    hp = N_HEADS * HEAD_PAD
    vt_blocks = tm // VT_BLOCK
    out_shape = [jax.ShapeDtypeStruct((rows, hp), BF16), jax.ShapeDtypeStruct((rows // VT_BLOCK, hp, VT_BLOCK), BF16)]
    out_specs = [row_spec(hp), pl.BlockSpec((vt_blocks, hp, VT_BLOCK), lambda t: (t, 0, 0))]
    if want_q:
        out_shape.append(jax.ShapeDtypeStruct((rows, hp), BF16))
        out_specs.append(row_spec(hp))
    if want_hg:
        out_shape += [jax.ShapeDtypeStruct((rows, 3 * HY_WIDTH), BF16), jax.ShapeDtypeStruct((rows, 2 * D_MODEL), BF16)]
        out_specs += [row_spec(3 * HY_WIDTH), row_spec(2 * D_MODEL)]
    outs = pl.pallas_call(
        functools.partial(_inproj_body, tiles_per_group, fixed_group, use_rope, want_q, want_hg),
        grid=(nt,), in_specs=specs, out_specs=out_specs, out_shape=out_shape,
        compiler_params=_cparams("arbitrary"), name="inproj",
    )(*ins)
    res = {"k": outs[0], "v": outs[1]}
    i = 2
    if want_q:
        res["q"] = outs[i]
        i += 1
    if want_hg:
        res["p_hy"], res["gates"] = outs[i], outs[i + 1]
    return res


def _attn_body(n_lat_blocks, *refs):
    if n_lat_blocks:
        q_ref, kc_ref, vc_ref, kl_ref, vl_ref, o_ref = refs
    else:
        q_ref, kc_ref, vc_ref, o_ref = refs
    tq = q_ref.shape[0]
    q = q_ref[...]
    sub = ATTN_KV // VT_BLOCK

    def run(heads):
        nh = len(heads)

        def scores(kblk):
            return tuple(lax.dot_general(kblk[:, sl], q[:, sl], (((1,), (1,)), ((), ())),
                                         preferred_element_type=F32) for sl in heads)

        def update(st, vt_blocks, carry):
            m_new = [jnp.maximum(carry[hd][0], jnp.max(st[hd], axis=0, keepdims=True)) for hd in range(nh)]
            p = [jnp.exp2(st[hd] - m_new[hd]).astype(BF16) for hd in range(nh)]
            out = []
            for hd in range(nh):
                m, acc = carry[hd]
                pv = None
                for j, vt in enumerate(vt_blocks):
                    r = _dot(vt[heads[hd], :], p[hd][j * VT_BLOCK:(j + 1) * VT_BLOCK])
                    pv = r if pv is None else pv + r
                out.append((m_new[hd], jnp.exp2(m - m_new[hd]) * acc + pv))
            return tuple(out)

        init = tuple((jnp.full((1, tq), -jnp.inf, F32), jnp.zeros((HEAD_PAD, tq), F32)) for _ in range(nh))
        carry = update(scores(kc_ref[0]), [vc_ref[0, 0]], init)
        if n_lat_blocks:
            def k_block(i):
                return kl_ref[0, pl.ds(i * ATTN_KV, ATTN_KV), :]

            def v_blocks(i):
                return [vl_ref[0, i * sub + j] for j in range(sub)]

            st = scores(k_block(0))
            for i in range(n_lat_blocks - 1):
                st_next = scores(k_block(i + 1))
                carry = update(st, v_blocks(i), carry)
                st = st_next
            carry = update(st, v_blocks(n_lat_blocks - 1), carry)
        return [acc[:V_HEAD] / acc[V_HEAD:V_HEAD + 1] for _, acc in carry]

    slices = [slice(hd * HEAD_PAD, (hd + 1) * HEAD_PAD) for hd in range(2)]
    outs = []
    for group in ATTN_HEAD_GROUPS:
        outs += run([slices[hd] for hd in group])
    o_ref[...] = jnp.concatenate(outs, axis=0).T.astype(BF16)


def _attention(q, k_ctx, vt_ctx, k_lat, vt_lat, batch):
    rows = q.shape[0]
    tq = min(ATTN_Q_TILE, rows // batch)
    qt_per_b = rows // batch // tq
    ctx_len = k_ctx.shape[1]
    assert ctx_len == VT_BLOCK
    pair_w = 2 * HEAD_PAD
    ins = [q, k_ctx, vt_ctx]
    specs = [pl.BlockSpec((tq, pair_w), lambda b, hp, t: (b * qt_per_b + t, hp)),
             pl.BlockSpec((1, ctx_len, pair_w), lambda b, hp, t: (b, 0, hp)),
             pl.BlockSpec((1, 1, pair_w, VT_BLOCK), lambda b, hp, t: (b, 0, hp, 0))]
    n_lat_blocks = 0
    if k_lat is not None:
        lat_len = k_lat.shape[1]
        n_lat_blocks = lat_len // ATTN_KV
        ins += [k_lat, vt_lat]
        specs += [pl.BlockSpec((1, lat_len, pair_w), lambda b, hp, t: (b, 0, hp)),
                  pl.BlockSpec((1, lat_len // VT_BLOCK, pair_w, VT_BLOCK), lambda b, hp, t: (b, 0, hp, 0))]
    return pl.pallas_call(
        functools.partial(_attn_body, n_lat_blocks),
        grid=(batch, N_HEADS // 2, qt_per_b), in_specs=specs,
        out_specs=pl.BlockSpec((tq, 2 * V_HEAD), lambda b, hp, t: (b * qt_per_b + t, hp)),
        out_shape=jax.ShapeDtypeStruct((rows, N_HEADS * V_HEAD), BF16),
        compiler_params=_cparams("arbitrary", "arbitrary", "arbitrary"), name="attention",
    )(*ins)


def _hypre_body(seq_tiles, n_chunks, p_ref, prev_ref, next_ref, w_ref, b_ref, z_ref, x0_ref, *zt_ref):
    j = pl.program_id(0) % seq_tiles
    p = p_ref[...].astype(F32)
    tm = p.shape[0]
    row = lax.broadcasted_iota(jnp.int32, (tm, 1), 0)
    prev_row = jnp.where(j != 0, prev_ref[...].astype(F32)[HALO_ROWS - 1:HALO_ROWS, :], 0.0)
    next_row = jnp.where(j != seq_tiles - 1, next_ref[...].astype(F32)[0:1, :], 0.0)
    up = jnp.where(row == 0, prev_row, pltpu.roll(p, 1, 0))
    dn = jnp.where(row == tm - 1, next_row, pltpu.roll(p, tm - 1, 0))
    u = up * w_ref[0:1, :] + p * w_ref[1:2, :] + dn * w_ref[2:3, :] + b_ref[...]
    c = HY_WIDTH
    z = u[:, :c] * u[:, c:2 * c]
    z_ref[...] = z.astype(BF16)
    x0_ref[...] = u[:, 2 * c:].astype(BF16)
    if zt_ref:
        q = tm // n_chunks
        for a in range(n_chunks):
            zt_ref[0][0, :, a * c:(a + 1) * c] = z[a * q:(a + 1) * q, :].astype(BF16)


def _hyena_pre(p_hy, short_w, short_b, batch, dft_q):
    rows = p_hy.shape[0]
    n = rows // batch
    tm = min(STREAM_ROW_TILE, n)
    seq_tiles = n // tm
    nt = rows // tm
    c3 = 3 * HY_WIDTH
    halo = tm // HALO_ROWS
    last_halo = rows // HALO_ROWS - 1
    specs = [pl.BlockSpec((tm, c3), lambda t: (t, 0)),
             pl.BlockSpec((HALO_ROWS, c3), lambda t: (jnp.maximum(t * halo - 1, 0), 0)),
             pl.BlockSpec((HALO_ROWS, c3), lambda t: (jnp.minimum((t + 1) * halo, last_halo), 0)),
             _full((3, c3)), _full((1, c3))]
    out_shape = [jax.ShapeDtypeStruct((rows, HY_WIDTH), BF16)] * 2
    out_specs = [pl.BlockSpec((tm, HY_WIDTH), lambda t: (t, 0))] * 2
    n_chunks = 1
    if dft_q is not None:
        n_chunks = tm // dft_q
        half_p = n // dft_q
        out_shape.append(jax.ShapeDtypeStruct((batch // 2, dft_q, 2 * half_p * HY_WIDTH), BF16))

        def zt_map(t):
            b, jt = t // seq_tiles, t % seq_tiles
            return (b // 2, 0, (b % 2) * seq_tiles + jt)
        out_specs.append(pl.BlockSpec((1, dft_q, n_chunks * HY_WIDTH), zt_map))
    return pl.pallas_call(
        functools.partial(_hypre_body, seq_tiles, n_chunks),
        grid=(nt,), in_specs=specs, out_specs=out_specs, out_shape=out_shape,
        compiler_params=_cparams("arbitrary"), name="hyena_pre",
    )(p_hy, p_hy, p_hy, short_w, short_b.reshape(1, c3))


def _filter_body(emb_ref, w1_ref, b1_ref, w2_ref, b2_ref, w3_ref, freq_ref, decay_ref, h0_ref, h1_ref, s_ref):
    a = pl.program_id(0)
    emb = emb_ref[...]
    freq = freq_ref[...]
    h = jnp.sin(freq * (_dot_hi(emb, w1_ref[...]) + b1_ref[...]))
    h = jnp.sin(freq * (_dot_hi(h, w2_ref[...]) + b2_ref[...]))
    h = _dot_hi(h, w3_ref[...]) * jnp.exp(-emb[:, 0:1] * jnp.abs(decay_ref[...]))
    c = HY_WIDTH
    row = lax.broadcasted_iota(jnp.int32, (emb.shape[0], 1), 0)
    h0 = h[:, :c]
    h1 = jnp.where(jnp.logical_and(a == 0, row == 0), 0.0, h[:, c:])
    q = h0_ref.shape[1]
    for j in range(h0_ref.shape[2] // c):
        h0_ref[0, :, j * c:(j + 1) * c] = h0[j * q:(j + 1) * q].astype(BF16)
        h1_ref[0, :, j * c:(j + 1) * c] = h1[j * q:(j + 1) * q].astype(BF16)
    part = jnp.sum(jnp.abs(h0) + jnp.abs(h1), axis=0, keepdims=True)

    @pl.when(a == 0)
    def _():
        s_ref[...] = jnp.zeros_like(s_ref)
    s_ref[...] += part


def _hyena_filter(n, q, lw):
    f32 = F32
    bands = (HY_EMB - 1) // 2
    t = jnp.linspace(0.0, 1.0, n, dtype=f32)[:, None]
    phase = (2.0 * math.pi / n) * jnp.arange(n, dtype=f32)[:, None] * jnp.linspace(1e-4, bands - 1, bands, dtype=f32)
    emb = jnp.concatenate([t, jnp.cos(phase), -jnp.sin(phase), jnp.zeros((n, 32 - HY_EMB), f32)], axis=-1)
    w1 = jnp.concatenate([lw["hy_w1"], jnp.zeros((32 - HY_EMB, HY_HIDDEN), f32)], axis=0)
    c = HY_WIDTH
    slots = n // q
    per_step = min(FILTER_CHUNKS, slots)
    row = lambda v: v.reshape(1, -1)
    return pl.pallas_call(
        _filter_body,
        grid=(slots // per_step,),
        in_specs=[pl.BlockSpec((per_step * q, 32), lambda a: (a, 0)), _full((32, HY_HIDDEN)), _full((1, HY_HIDDEN)),
                  _full((HY_HIDDEN, HY_HIDDEN)), _full((1, HY_HIDDEN)), _full((HY_HIDDEN, 2 * c)),
                  _full((1, HY_HIDDEN)), _full((1, 2 * c))],
        out_specs=[pl.BlockSpec((1, q, per_step * c), lambda a: (0, 0, a)),
                   pl.BlockSpec((1, q, per_step * c), lambda a: (0, 0, a)),
                   pl.BlockSpec((1, c), lambda a: (0, 0))],
        out_shape=[jax.ShapeDtypeStruct((1, q, slots * c), BF16)] * 2 + [jax.ShapeDtypeStruct((1, c), f32)],
        compiler_params=_cparams("arbitrary"), name="hyena_filter",
    )(emb, w1, row(lw["hy_b1"]), lw["hy_w2"], row(lw["hy_b2"]), lw["hy_w3"], row(lw["hy_freq"]),
      row(lw["hy_decay"]))


def _colslot_body(nparts, bq, c, *refs):
    x_refs, t_refs, o_ref = refs[:nparts], refs[nparts:2 * nparts], refs[-1]
    for j in range(bq):
        acc = None
        for x_ref, t_ref in zip(x_refs, t_refs):
            r = _dot_dft(t_ref[j], x_ref[0, j])
            acc = r if acc is None else acc + r
        o_ref[0, :, j * c:(j + 1) * c] = acc.astype(o_ref.dtype)


def _colslot_matmul(xs, tabs, out_dtype, c=HY_WIDTH):
    tabs = [jnp.asarray(t, BF16) for t in tabs]
    g, q = xs[0].shape[0], xs[0].shape[1]
    m = tabs[0].shape[1]
    bq = COLSLOT_BQ
    xs4 = [x.reshape(g, q, -1, c) for x in xs]
    specs = [pl.BlockSpec((1, bq, x.shape[2], c), lambda gi, qi: (gi, qi, 0, 0)) for x in xs4]
    specs += [pl.BlockSpec((bq, m, t.shape[2]), lambda gi, qi: (qi, 0, 0)) for t in tabs]
    return pl.pallas_call(
        functools.partial(_colslot_body, len(xs), bq, c),
        grid=(g, q // bq), in_specs=specs,
        out_specs=pl.BlockSpec((1, m, bq * c), lambda gi, qi: (gi, 0, qi)),
        out_shape=jax.ShapeDtypeStruct((g, m, q * c), out_dtype),
        compiler_params=_cparams("arbitrary", "arbitrary"), name="dft_stride_stage",
    )(*xs4, *tabs)


def _spec_mid_body(g_ref, m3_ref, m3i_ref, k_ref, hr_ref, hi_ref):
    bp, q = hr_ref.shape[0], hr_ref.shape[1]
    c = hr_ref.shape[2] // DFT_CB
    for j in range(DFT_CB):
        kh = k_ref[j]
        kr, ki = kh[:q], kh[q:]
        for b in range(bp):
            x = _dot_dft(m3_ref[...], g_ref[b, :, j].reshape(2 * q, -1))
            xr, xi = x[:q], x[q:]
            y = jnp.concatenate([xr * kr - xi * ki, xr * ki + xi * kr], axis=0)
            h = _dot_dft(m3i_ref[...], y)
            hr_ref[b, :, j * c:(j + 1) * c] = h[:q].astype(BF16)
            hi_ref[b, :, j * c:(j + 1) * c] = h[q:].astype(BF16)


def _spec_mid(gf, m3, m3i, khat, p, q, c=HY_WIDTH):
    bp = gf.shape[0]
    g5 = gf.reshape(bp, 2, p, q, c)
    cb = DFT_CB
    return pl.pallas_call(
        _spec_mid_body,
        grid=(p // cb,),
        in_specs=[pl.BlockSpec((bp, 2, cb, q, c), lambda ci: (0, 0, ci, 0, 0)), _full((2 * q, 2 * q)),
                  _full((2 * q, 2 * q)), pl.BlockSpec((cb, 2 * q, c), lambda ci: (ci, 0, 0))],
        out_specs=[pl.BlockSpec((bp, q, cb * c), lambda ci: (0, 0, ci))] * 2,
        out_shape=[jax.ShapeDtypeStruct((bp, q, p * c), BF16)] * 2,
        compiler_params=_cparams("arbitrary"), name="dft_mid",
    )(g5, m3, m3i, khat)


def _filter_spec_body(g_ref, m3k_ref, s_ref, k_ref):
    q4 = m3k_ref.shape[1]
    inv = 1.0 / s_ref[...]
    for j in range(DFT_CB):
        k_ref[j] = _dot_dft(m3k_ref[...], g_ref[0, :, j].reshape(q4, -1)) * inv


def _filter_spec(gk, m3k, s, p, q, c=HY_WIDTH):
    g5 = gk.reshape(1, 4, p, q, c)
    cb = DFT_CB
    return pl.pallas_call(
        _filter_spec_body,
        grid=(p // cb,),
        in_specs=[pl.BlockSpec((1, 4, cb, q, c), lambda ci: (0, 0, ci, 0, 0)), _full((2 * q, 4 * q)), _full((1, c))],
        out_specs=pl.BlockSpec((cb, 2 * q, c), lambda ci: (ci, 0, 0)),
        out_shape=jax.ShapeDtypeStruct((p, 2 * q, c), F32),
        compiler_params=_cparams("arbitrary"), name="filter_spectrum",
    )(g5, m3k, s)


def _direct_conv_body(z_ref, h0_ref, h1_ref, s_ref, tf_ref, tk_ref, ti_ref, o_ref):
    n2 = tf_ref.shape[0] // 2
    kh = _dot_dft(tk_ref[...], jnp.concatenate([h0_ref[0], h1_ref[0]], axis=0)) / s_ref[...]
    x = _dot_dft(tf_ref[...], z_ref[0])
    xr, xi, kr, ki = x[:n2], x[n2:], kh[:n2], kh[n2:]
    y = jnp.concatenate([xr * kr - xi * ki, xr * ki + xi * kr], axis=0)
    o_ref[0] = _dot_dft(ti_ref[...], y)


def _direct_conv(z, h0, h1, s, n, c=HY_WIDTH):
    bp = z.shape[0] // (2 * n)
    tf, tk, ti = (jnp.asarray(t, BF16) for t in _direct_tables(n))
    out = pl.pallas_call(
        _direct_conv_body,
        grid=(bp,),
        in_specs=[pl.BlockSpec((1, 2 * n, c), lambda b: (b, 0, 0)), _full((1, n, c)), _full((1, n, c)), _full((1, c)),
                  _full(tf.shape), _full(tk.shape), _full(ti.shape)],
        out_specs=pl.BlockSpec((1, 2 * n, c), lambda b: (b, 0, 0)),
        out_shape=jax.ShapeDtypeStruct((bp, 2 * n, c), F32),
        compiler_params=_cparams("arbitrary"), name="direct_conv",
    )(z.reshape(bp, 2 * n, c), h0, h1, s, tf, tk, ti)
    return out.reshape(bp * 2 * n, c)


def _cis(m, n_total):
    ang = (2.0 * np.pi / n_total) * (m % n_total).astype(np.float64)
    return np.cos(ang), -np.sin(ang)


@functools.lru_cache(maxsize=None)
def _direct_tables(n):
    nn = 2 * n
    f = np.arange(nn)[:, None]
    t = np.arange(n)[None, :]
    cr, ci = _cis(f * t, nn)
    tf = np.block([[cr, -ci], [ci, cr]])
    tk = np.block([[cr, cr], [ci, -ci]])
    ti = np.block([[cr.T, ci.T], [-ci.T, cr.T]]) / nn
    return tuple(a.astype(np.float32) for a in (tf, tk, ti))


@functools.lru_cache(maxsize=None)
def _twostage_tables(p, q):
    nn = p * q
    hp = p // 2
    s = np.arange(q)[:, None, None]
    c = np.arange(p)[None, :, None]
    a = np.arange(hp)[None, None, :]
    tr, ti = _cis(c * (q * a + s), nn)
    z = np.zeros_like(tr)
    t_data = np.concatenate([np.concatenate([tr, -ti], 2), np.concatenate([ti, tr], 2)], 1)
    t_k0 = np.concatenate([tr, ti, z, z], 1)
    t_k1 = np.concatenate([z, z, tr, -ti], 1)
    trt, tit = np.swapaxes(tr, 1, 2) / nn, np.swapaxes(ti, 1, 2) / nn
    t_inv_r = np.concatenate([trt, -tit], 1)
    t_inv_i = np.concatenate([tit, trt], 1)
    d = np.arange(q)[:, None]
    b = np.arange(q)[None, :]
    fr, fi = _cis(d * b, q)
    m3 = np.block([[fr, -fi], [fi, fr]])
    m3i = np.block([[fr, fi], [-fi, fr]])
    m3k = np.block([[fr, -fi, fr, fi], [fi, fr, -fi, fr]])
    return tuple(x.astype(np.float32) for x in (t_data, t_k0, t_k1, t_inv_r, t_inv_i, m3, m3i, m3k))


def _long_conv_spectrum(n, lw):
    p, q = DFT_P, DFT_Q
    h0, h1, s = _hyena_filter(n, q, lw)
    _, t_k0, t_k1, _, _, _, _, m3k = _twostage_tables(p, q)
    gk = _colslot_matmul([h0, h1], [t_k0, t_k1], BF16)
    return _filter_spec(gk, jnp.asarray(m3k, BF16), s, p, q)


def _long_conv(zt, khat):
    p, q = DFT_P, DFT_Q
    t_data, _, _, t_inv_r, t_inv_i, m3, m3i, _ = _twostage_tables(p, q)
    gf = _colslot_matmul([zt], [t_data], BF16)
    hr, hi = _spec_mid(gf, jnp.asarray(m3, BF16), jnp.asarray(m3i, BF16), khat, p, q)
    y = _colslot_matmul([hr, hi], [t_inv_r, t_inv_i], BF16)
    return y.reshape(-1, HY_WIDTH)


def _merge_body(tiles_per_group, fixed_group, x_ref, g1_ref, oa_ref, conv_ref, z_ref, x0_ref, gate_ref, bias_ref,
                wba_ref, wbh_ref, wout_ref, o_ref):
    group = fixed_group if fixed_group is not None else pl.program_id(0) // tiles_per_group
    o_hy = ((conv_ref[...] + z_ref[...] * bias_ref[...]) * x0_ref[...]).astype(BF16)
    d = D_MODEL
    merged = gate_ref[:, :d] * _dot(oa_ref[...], wba_ref[...]) + gate_ref[:, d:] * _dot(o_hy, wbh_ref[...])
    mix = _dot(merged.astype(BF16), wout_ref[...])
    o_ref[...] = x_ref[...] + _mod_row(g1_ref, group) * mix


def _merge(x, mod, o_attn, conv, z, x0, gates, lw, *, tiles_per_group, fixed_group):
    rows = x.shape[0]
    tm = STREAM_ROW_TILE
    row_spec = lambda n: pl.BlockSpec((tm, n), lambda t: (t, 0))
    c = HY_WIDTH
    return pl.pallas_call(
        functools.partial(_merge_body, _scale_tiles(tiles_per_group, tm), fixed_group),
        grid=(rows // tm,),
        in_specs=[row_spec(D_MODEL), _mod_spec(2), row_spec(N_HEADS * V_HEAD), row_spec(c), row_spec(c), row_spec(c),
                  row_spec(2 * D_MODEL), _full((1, c)), _full(lw["w_br_attn"].shape), _full(lw["w_br_hy"].shape),
                  _full(lw["w_out"].shape)],
        out_specs=row_spec(D_MODEL),
        out_shape=jax.ShapeDtypeStruct((rows, D_MODEL), F32),
        compiler_params=_cparams("arbitrary"), name="merge_out",
    )(x, mod, o_attn, conv, z, x0, gates, lw["hy_bias"], lw["w_br_attn"], lw["w_br_hy"], lw["w_out"])


def _swiglu(h, wg_ref, wu_ref, wd_ref):
    acc = None
    for c0, c1 in FFN_SPANS:
        gate = _dot(h, wg_ref[0, :, c0:c1])
        up = _dot(h, wu_ref[0, :, c0:c1])
        y = _dot((gate * jax.nn.sigmoid(gate) * up).astype(BF16), wd_ref[0, c0:c1, :])
        acc = y if acc is None else acc + y
    return acc


def _ffn_body(tiles_per_group, fixed_group, x_ref, sh_ref, sc_ref, g2_ref, n2_ref, wg_ref, wu_ref, wd_ref, o_ref):
    group = fixed_group if fixed_group is not None else pl.program_id(0) // tiles_per_group
    x = x_ref[...]
    h = (_rms(x, n2_ref[...]) * (1.0 + _mod_row(sc_ref, group)) + _mod_row(sh_ref, group)).astype(BF16)
    o_ref[...] = x + _mod_row(g2_ref, group) * _swiglu(h, wg_ref, wu_ref, wd_ref)


def _ffn(x, mod, norm_g, w_gate, w_up, w_down, *, tiles_per_group, fixed_group):
    rows = x.shape[0]
    tm = min(FFN_ROW_TILE, rows)
    row_spec = pl.BlockSpec((tm, D_MODEL), lambda t: (t, 0))
    return pl.pallas_call(
        functools.partial(_ffn_body, tiles_per_group * ROW_TILE // tm if tiles_per_group else None, fixed_group),
        grid=(rows // tm,),
        in_specs=[row_spec, _mod_spec(3), _mod_spec(4), _mod_spec(5), _full((1, D_MODEL)),
                  _full(w_gate.shape), _full(w_up.shape), _full(w_down.shape)],
        out_specs=row_spec,
        out_shape=jax.ShapeDtypeStruct((rows, D_MODEL), F32),
        compiler_params=_cparams("arbitrary"), name="ffn",
    )(x, mod, mod, mod, norm_g, w_gate, w_up, w_down)


def _route_body(tiles_per_group, x_ref, sh_ref, sc_ref, n2_ref, router_ref, h_ref, route_ref):
    group = pl.program_id(0) // tiles_per_group
    h = _rms(x_ref[...], n2_ref[...]) * (1.0 + _mod_row(sc_ref, group)) + _mod_row(sh_ref, group)
    h_ref[...] = h
    logits = _dot_hi(h, router_ref[...])
    lane = lax.broadcasted_iota(jnp.int32, logits.shape, 1)
    logits = jnp.where(lane < N_EXPERTS, logits, -jnp.inf)
    m1 = jnp.max(logits, axis=-1, keepdims=True)
    i1 = jnp.min(jnp.where(logits == m1, lane, LANES), axis=-1, keepdims=True)
    rest = jnp.where(lane == i1, -jnp.inf, logits)
    m2 = jnp.max(rest, axis=-1, keepdims=True)
    i2 = jnp.min(jnp.where(rest == m2, lane, LANES), axis=-1, keepdims=True)
    e2 = jnp.exp(m2 - m1)
    w1 = 1.0 / (1.0 + e2)
    route_ref[...] = (jnp.where(lane == 0, i1.astype(F32), 0.0) + jnp.where(lane == 1, i2.astype(F32), 0.0)
                      + jnp.where(lane == 2, w1, 0.0) + jnp.where(lane == 3, e2 * w1, 0.0))


def _route(x, mod, norm_g, router, tiles_per_group):
    rows = x.shape[0]
    tm = STREAM_ROW_TILE
    row_spec = lambda n: pl.BlockSpec((tm, n), lambda t: (t, 0))
    return pl.pallas_call(
        functools.partial(_route_body, _scale_tiles(tiles_per_group, tm)),
        grid=(rows // tm,),
        in_specs=[row_spec(D_MODEL), _mod_spec(3), _mod_spec(4), _full((1, D_MODEL)), _full(router.shape)],
        out_specs=[row_spec(D_MODEL), row_spec(LANES)],
        out_shape=[jax.ShapeDtypeStruct((rows, D_MODEL), F32), jax.ShapeDtypeStruct((rows, LANES), F32)],
        compiler_params=_cparams("arbitrary"), name="moe_route",
    )(x, mod, mod, norm_g, router)


def _gather_rows(idx_ref, n_rows, src_hbm, dst, sem):
    def issue(i, carry):
        for prio in range(2):
            r = 2 * i + prio
            pltpu.make_async_copy(src_hbm.at[pl.ds(idx_ref[0, 0, r], 1), :], dst.at[pl.ds(r, 1), :],
                                  sem).start(priority=prio)
        return carry
    lax.fori_loop(0, n_rows // 2, issue, 0, unroll=8)


def _wait_rows(n_rows, src_hbm, dst, sem):
    pltpu.make_async_copy(src_hbm.at[pl.ds(0, n_rows), :], dst, sem).wait()


def _moe_dispatch_body(idx_ref, h_ref, xs_in_hbm, xs_hbm, stage, sem):
    del xs_in_hbm
    t = pl.program_id(0)
    n_tiles = pl.num_programs(0)
    tm = h_ref.shape[0]
    slot = t % 2

    def wait(s):
        for _ in range(2):
            pltpu.make_async_copy(stage.at[s], xs_hbm.at[pl.ds(0, tm), :], sem.at[s]).wait()

    @pl.when(t >= 2)
    def _():
        wait(slot)

    stage[slot] = h_ref[...]

    def issue(r, carry):
        row = stage.at[slot, pl.ds(r, 1), :]
        pltpu.make_async_copy(row, xs_hbm.at[pl.ds(idx_ref[0, 0, r], 1), :], sem.at[slot]).start(priority=0)
        pltpu.make_async_copy(row, xs_hbm.at[pl.ds(idx_ref[0, 0, tm + r], 1), :], sem.at[slot]).start(priority=1)
        return carry
    lax.fori_loop(0, tm, issue, 0, unroll=8)

    @pl.when(t == n_tiles - 1)
    def _():
        wait(slot)
        wait(1 - slot)


def _tile_index_blocks(dest, tm):
    n_tiles = dest.shape[0] // tm
    return dest.reshape(n_tiles, tm, 2).transpose(0, 2, 1).reshape(n_tiles, 1, 2 * tm)


def _moe_dispatch(h, dest, n_rows):
    tm = STREAM_ROW_TILE
    idx = _tile_index_blocks(dest, tm)
    n_tiles = idx.shape[0]
    assert n_tiles >= 2
    return pl.pallas_call(
        _moe_dispatch_body,
        grid=(n_tiles,),
        in_specs=[pl.BlockSpec((1, 1, 2 * tm), lambda t: (t, 0, 0), memory_space=pltpu.SMEM),
                  pl.BlockSpec((tm, D_MODEL), lambda t: (t, 0)), pl.BlockSpec(memory_space=pl.ANY)],
        out_specs=pl.BlockSpec(memory_space=pl.ANY),
        out_shape=jax.ShapeDtypeStruct((n_rows, D_MODEL), F32),
        scratch_shapes=[pltpu.VMEM((2, tm, D_MODEL), F32), pltpu.SemaphoreType.DMA((2,))],
        input_output_aliases={2: 0},
        compiler_params=_cparams("arbitrary"), name="moe_dispatch",
    )(idx, h, jnp.zeros((n_rows, D_MODEL), F32))


def _moe_group_body(te_ref, nu_ref, x_ref, wg_ref, wu_ref, wd_ref, y_ref):
    t = pl.program_id(0)

    @pl.when(t < nu_ref[0])
    def _():
        y_ref[...] = _swiglu(x_ref[...].astype(BF16), wg_ref, wu_ref, wd_ref)

    @pl.when(t >= nu_ref[0])
    def _():
        y_ref[...] = jnp.zeros_like(y_ref)


def _moe_group(xs, tile_expert, n_used, w_gate, w_up, w_down):
    tm = MOE_ROW_TILE
    n_rows = xs.shape[0]
    w_spec = lambda w: pl.BlockSpec((1,) + w.shape[1:], lambda t, te, nu: (te[t], 0, 0))
    grid_spec = pltpu.PrefetchScalarGridSpec(
        num_scalar_prefetch=2,
        grid=(n_rows // tm,),
        in_specs=[pl.BlockSpec((tm, D_MODEL), lambda t, te, nu: (jnp.minimum(t, nu[0] - 1), 0)),
                  w_spec(w_gate), w_spec(w_up), w_spec(w_down)],
        out_specs=pl.BlockSpec((tm, D_MODEL), lambda t, te, nu: (t, 0)),
    )
    return pl.pallas_call(
        _moe_group_body, grid_spec=grid_spec,
        out_shape=jax.ShapeDtypeStruct((n_rows, D_MODEL), F32),
        compiler_params=_cparams("arbitrary"), name="moe_group",
    )(tile_expert, n_used, xs, w_gate, w_up, w_down)


def _moe_combine_body(tiles_per_group, idx_ref, idx_next_ref, x_ref, g2_ref, fg_ref, route_ref, y_hbm, o_ref,
                      yg_scr, sem):
    t = pl.program_id(0)
    n_tiles = pl.num_programs(0)
    group = t // tiles_per_group
    tm = x_ref.shape[0]
    slot = t % 2

    @pl.when(t == 0)
    def _():
        _gather_rows(idx_ref, 2 * tm, y_hbm, yg_scr.at[0], sem.at[0])

    _wait_rows(2 * tm, y_hbm, yg_scr.at[slot], sem.at[slot])

    @pl.when(t + 1 < n_tiles)
    def _():
        _gather_rows(idx_next_ref, 2 * tm, y_hbm, yg_scr.at[1 - slot], sem.at[1 - slot])

    route = route_ref[...]
    y = route[:, 2:3] * yg_scr[slot, :tm] + route[:, 3:4] * yg_scr[slot, tm:]
    o_ref[...] = _rms(x_ref[...] + _mod_row(g2_ref, group) * y, fg_ref[...])


def _moe_combine(x, mod, final_g, route, y_sorted, dest, tiles_per_group):
    rows = x.shape[0]
    tm = STREAM_ROW_TILE
    idx = _tile_index_blocks(dest, tm)
    n_tiles = idx.shape[0]
    return pl.pallas_call(
        functools.partial(_moe_combine_body, _scale_tiles(tiles_per_group, tm)),
        grid=(n_tiles,),
        in_specs=[pl.BlockSpec((1, 1, 2 * tm), lambda t: (t, 0, 0), memory_space=pltpu.SMEM),
                  pl.BlockSpec((1, 1, 2 * tm), lambda t: (jnp.minimum(t + 1, n_tiles - 1), 0, 0),
                               memory_space=pltpu.SMEM),
                  pl.BlockSpec((tm, D_MODEL), lambda t: (t, 0)), _mod_spec(5), _full((1, D_MODEL)),
                  pl.BlockSpec((tm, LANES), lambda t: (t, 0)), pl.BlockSpec(memory_space=pl.ANY)],
        out_specs=pl.BlockSpec((tm, D_MODEL), lambda t: (t, 0)),
        out_shape=jax.ShapeDtypeStruct((rows, D_MODEL), F32),
        scratch_shapes=[pltpu.VMEM((2, 2 * tm, D_MODEL), F32), pltpu.SemaphoreType.DMA((2,))],
        compiler_params=_cparams("arbitrary"), name="moe_combine",
    )(idx, idx, x, mod, final_g, route, y_sorted)


def _moe_plan(route, tm):
    t = route.shape[0]
    experts = jnp.concatenate([route[:, 0], route[:, 1]]).astype(jnp.int32)
    onehot = (experts[:, None] == jnp.arange(N_EXPERTS, dtype=jnp.int32)[None, :]).astype(jnp.int32)
    csum = jnp.cumsum(onehot, axis=0)
    rank = jnp.sum(csum * onehot, axis=1) - 1
    padded = (csum[-1] + tm - 1) // tm * tm
    ends = jnp.cumsum(padded)
    dest = jnp.sum((ends - padded)[None, :] * onehot, axis=1) + rank
    n_rows = 2 * t + N_EXPERTS * tm
    tile_start = jnp.arange(n_rows // tm, dtype=jnp.int32) * tm
    tile_expert = jnp.minimum(jnp.sum((tile_start[:, None] >= ends[None, :]).astype(jnp.int32), axis=1), N_EXPERTS - 1)
    n_used = (ends[-1:] // tm).astype(jnp.int32)
    return dest.reshape(2, t).T, tile_expert, n_used, n_rows


def _moe(x, mod, norm_g, router, w_gate, w_up, w_down, final_g, tiles_per_group):
    h, route = _route(x, mod, norm_g, router, tiles_per_group)
    dest, tile_expert, n_used, n_rows = _moe_plan(route, MOE_ROW_TILE)
    xs = _moe_dispatch(h, dest, n_rows)
    y_sorted = _moe_group(xs, tile_expert, n_used, w_gate, w_up, w_down)
    return _moe_combine(x, mod, final_g, route, y_sorted, dest, tiles_per_group)


def _pad_heads(w, width):
    k = w.shape[0]
    w = w.reshape(k, N_HEADS, width)
    return jnp.pad(w, ((0, 0), (0, 0), (0, HEAD_PAD - width))).reshape(k, N_HEADS * HEAD_PAD)


def _rot_cols(w):
    half = QK_ROPE // 2
    return jnp.concatenate([-w[..., half:], w[..., :half]], axis=-1)


def _layer_weights(p, layer):
    cols = lambda a, b: p["w_in"][layer, :, a:b]
    w_uq = p["w_uq"][layer].reshape(Q_LORA, N_HEADS, QK_NOPE + QK_ROPE)
    zeros_nope = jnp.zeros((Q_LORA, N_HEADS, QK_NOPE), F32)
    w_uq_b = jnp.concatenate([zeros_nope, _rot_cols(w_uq[..., QK_NOPE:])], axis=-1)
    w_kr = cols(KV_START + KV_LORA, HY_START)
    zk = jnp.zeros((D_MODEL, QK_NOPE), F32)
    zp = jnp.zeros((D_MODEL, HEAD_PAD - QK_NOPE - QK_ROPE), F32)
    w_kr2 = jnp.concatenate([zk, w_kr, zp, zk, _rot_cols(w_kr), zp], axis=-1)
    row = lambda v: v.reshape(1, -1)
    bf = lambda v: v.astype(BF16)
    return {
        "norm1_g": row(p["norm1_g"][layer]), "norm2_g": row(p["norm2_g"][layer]),
        "w_q": bf(cols(0, Q_LORA)), "q_norm_g": row(p["q_norm_g"][layer]),
        "w_uq_a": bf(_pad_heads(w_uq.reshape(Q_LORA, -1), QK_NOPE + QK_ROPE)),
        "w_uq_b": bf(_pad_heads(w_uq_b.reshape(Q_LORA, -1), QK_NOPE + QK_ROPE)),
        "w_kv": bf(cols(KV_START, KV_START + KV_LORA)), "kv_norm_g": row(p["kv_norm_g"][layer]),
        "w_uk": bf(_pad_heads(p["w_uk"][layer], QK_NOPE)), "w_uv": bf(_pad_heads(p["w_uv"][layer], V_HEAD).T),
        "w_kr": bf(w_kr2),
        "w_hy": bf(cols(HY_START, GATE_START)), "w_gate": bf(cols(GATE_START, GATE_START + 2 * D_MODEL)),
        "hy_short_w": p["hy_short_w"][layer], "hy_short_b": p["hy_short_b"][layer],
        "hy_w1": p["hy_w1"][layer], "hy_b1": p["hy_b1"][layer], "hy_w2": p["hy_w2"][layer],
        "hy_b2": p["hy_b2"][layer], "hy_w3": p["hy_w3"][layer], "hy_freq": p["hy_freq"][layer],
        "hy_decay": p["hy_decay"][layer], "hy_bias": row(p["hy_bias"][layer]),
        "w_br_attn": bf(p["w_br_attn"][layer]), "w_br_hy": bf(p["w_br_hy"][layer]), "w_out": bf(p["w_out"][layer]),
    }


def _rope_tables(n):
    rows = n // GRID_W
    n_freq = QK_ROPE // 4
    inv_freq = ROPE_BASE ** (-jnp.arange(n_freq, dtype=F32) / n_freq)
    r = jnp.repeat(jnp.arange(rows, dtype=F32), GRID_W)
    col = jnp.tile(jnp.arange(GRID_W, dtype=F32), rows)
    ang = jnp.concatenate([r[:, None] * inv_freq, col[:, None] * inv_freq], axis=-1)
    cos, sin = jnp.cos(ang), jnp.sin(ang)
    ones = jnp.ones((n, QK_NOPE), F32)
    zeros = jnp.zeros((n, QK_NOPE), F32)
    pad = jnp.zeros((n, HEAD_PAD - QK_NOPE - QK_ROPE), F32)
    return (jnp.concatenate([ones, cos, cos, pad], axis=-1), jnp.concatenate([zeros, sin, sin, pad], axis=-1))


def _token_mixer(xs, mod, lw, khat, conv_short, *, batch, n, tiles_per_group, fixed_group, rope_tabs,
                 k_ctx=None, v_ctx=None):
    pr = _inproj(xs, mod, lw, tiles_per_group=tiles_per_group, fixed_group=fixed_group, rope_tabs=rope_tabs,
                 want_q=True, want_hg=True)
    k3 = pr["k"].reshape(batch, n, -1)
    v3 = pr["v"].reshape(batch, n // VT_BLOCK, N_HEADS * HEAD_PAD, VT_BLOCK)
    if k_ctx is None:
        o_attn = _attention(pr["q"], k3, v3, None, None, batch)
    else:
        o_attn = _attention(pr["q"], k_ctx, v_ctx, k3, v3, batch)
    if conv_short:
        z, x0 = _hyena_pre(pr["p_hy"], lw["hy_short_w"], lw["hy_short_b"], batch, None)
        conv = _direct_conv(z, *khat, n)
    else:
        z, x0, zt = _hyena_pre(pr["p_hy"], lw["hy_short_w"], lw["hy_short_b"], batch, DFT_Q)
        conv = _long_conv(zt, khat)
    x_new = _merge(xs, mod, o_attn, conv, z, x0, pr["gates"], lw, tiles_per_group=tiles_per_group,
                   fixed_group=fixed_group)
    return x_new, k3, v3


def kernel(x, c, ctx, c_ctx, w_mod, b_mod, norm1_g, norm2_g, w_in, q_norm_g, kv_norm_g, w_uq, w_uk, w_uv, hy_short_w, hy_short_b, hy_w1, hy_b1, hy_w2, hy_b2, hy_w3, hy_freq, hy_decay, hy_bias, w_br_attn, w_br_hy, w_out, ffn_w_gate, ffn_w_up, ffn_w_down, moe_router, moe_w_gate, moe_w_up, moe_w_down, final_g):
    p = dict(w_in=w_in, norm1_g=norm1_g, norm2_g=norm2_g, q_norm_g=q_norm_g, kv_norm_g=kv_norm_g, w_uq=w_uq,
             w_uk=w_uk, w_uv=w_uv, hy_short_w=hy_short_w, hy_short_b=hy_short_b, hy_w1=hy_w1, hy_b1=hy_b1,
             hy_w2=hy_w2, hy_b2=hy_b2, hy_w3=hy_w3, hy_freq=hy_freq, hy_decay=hy_decay, hy_bias=hy_bias,
             w_br_attn=w_br_attn, w_br_hy=w_br_hy, w_out=w_out)
    batch, seq, d = x.shape
    ctx_len = ctx.shape[1]
    depth = w_mod.shape[0]
    ctx_group = batch
    cond8 = jnp.zeros((SUBLANES, d), F32).at[:batch].set(c).at[ctx_group].set(c_ctx)
    rope_tabs = _rope_tables(seq)
    lat_tiles = seq // ROW_TILE
    xs = x.reshape(batch * seq, d)
    cs = ctx.reshape(batch * ctx_len, d)
    bf = lambda v: v.astype(BF16)
    for layer in range(depth):
        last = layer == depth - 1
        lw = _layer_weights(p, layer)
        mod = _adaln(cond8, w_mod, b_mod, layer)
        khat_lat = _long_conv_spectrum(seq, lw)
        if last:
            pr = _inproj(cs, mod, lw, tiles_per_group=None, fixed_group=ctx_group, rope_tabs=None, want_q=False,
                         want_hg=False)
            k_ctx = pr["k"].reshape(batch, ctx_len, -1)
            v_ctx = pr["v"].reshape(batch, ctx_len // VT_BLOCK, N_HEADS * HEAD_PAD, VT_BLOCK)
        else:
            khat_ctx = _hyena_filter(ctx_len, ctx_len, lw)
            cs_mid, k_ctx, v_ctx = _token_mixer(cs, mod, lw, khat_ctx, True, batch=batch, n=ctx_len,
                                                tiles_per_group=None, fixed_group=ctx_group, rope_tabs=None)
        xs, _, _ = _token_mixer(xs, mod, lw, khat_lat, False, batch=batch, n=seq, tiles_per_group=lat_tiles,
                                fixed_group=None, rope_tabs=rope_tabs, k_ctx=k_ctx, v_ctx=v_ctx)
        i = layer // 2
        n2 = lw["norm2_g"]
        if layer % 2 == 0:
            assert not last
            wg, wu, wd = bf(ffn_w_gate[i])[None], bf(ffn_w_up[i])[None], bf(ffn_w_down[i])[None]
            xs = _ffn(xs, mod, n2, wg, wu, wd, tiles_per_group=lat_tiles, fixed_group=None)
            cs = _ffn(cs_mid, mod, n2, wg, wu, wd, tiles_per_group=None, fixed_group=ctx_group)
        else:
            assert last
            router = jnp.pad(moe_router[i], ((0, 0), (0, LANES - N_EXPERTS)))
            xs = _moe(xs, mod, n2, router, bf(moe_w_gate[i]), bf(moe_w_up[i]), bf(moe_w_down[i]),
                      final_g.reshape(1, d), lat_tiles)
    return xs.reshape(batch, seq, d)
```

```python
import functools
import math

import numpy as np
import jax
import jax.numpy as jnp
from jax import lax
from jax.experimental import pallas as pl
from jax.experimental.pallas import tpu as pltpu

F32 = jnp.float32
BF16 = jnp.bfloat16
HIGHEST = lax.Precision.HIGHEST

D_MODEL = 1024
GRID_W = 64
EPS = 1e-6
N_HEADS = 8
Q_LORA = 384
KV_LORA = 256
QK_NOPE = 64
QK_ROPE = 32
V_HEAD = 64
ROPE_BASE = 10000.0
ATTN_SCALE = (QK_NOPE + QK_ROPE) ** -0.5
HY_WIDTH = 512
HY_EMB = 17
HY_HIDDEN = 64
D_FF = 2816
N_EXPERTS = 8
KV_START = Q_LORA
HY_START = KV_START + KV_LORA + QK_ROPE
GATE_START = HY_START + 3 * HY_WIDTH

LANES = 128
SUBLANES = 8
HALO_ROWS = 16
HEAD_PAD = LANES
VMEM_LIMIT = 56 * 2**20

ROW_TILE = 256
INPROJ_ROW_TILE = 512
STREAM_ROW_TILE = 512
FFN_ROW_TILE = 512
MXU_TILE = 256
FFN_SPANS = ((0, 4 * MXU_TILE), (4 * MXU_TILE, 8 * MXU_TILE), (8 * MXU_TILE, D_FF))
MOE_ROW_TILE = 256
VT_BLOCK = ROW_TILE
ATTN_KV = 256
ATTN_Q_TILE = 1024
ATTN_HEAD_GROUPS = ((0, 1),)
Q_SCALE = ATTN_SCALE * math.log2(math.e)
DFT_P = 64
DFT_Q = 128
COLSLOT_BQ = 16
DFT_CB = 8
FILTER_CHUNKS = 4


def _cparams(*sem):
    return pltpu.CompilerParams(dimension_semantics=sem, vmem_limit_bytes=VMEM_LIMIT)


def _dot(a, b):
    return jnp.dot(a, b, preferred_element_type=F32)


def _dot_hi(a, b):
    return jnp.dot(a, b, precision=HIGHEST, preferred_element_type=F32)


def _dot_split(a, b):
    a_hi, b_hi = a.astype(BF16), b.astype(BF16)
    a_lo = (a - a_hi.astype(F32)).astype(BF16)
    b_lo = (b - b_hi.astype(F32)).astype(BF16)
    return _dot(a_hi, b_hi) + (_dot(a_hi, b_lo) + _dot(a_lo, b_hi))


def _dot_dft(table, x):
    return _dot(table, x.astype(BF16))


def _rms(xf, g):
    return xf * lax.rsqrt(jnp.mean(xf * xf, axis=-1, keepdims=True) + EPS) * g


def _full(shape):
    nd = len(shape)
    return pl.BlockSpec(shape, lambda *_: (0,) * nd, pipeline_mode=pl.Buffered(1))


def _adaln_body(c_ref, w_ref, b_ref, o_ref):
    c = c_ref[...]
    o_ref[...] = _dot_hi(c * jax.nn.sigmoid(c), w_ref[0]) + b_ref[0]


def _adaln(cond8, w, b, layer):
    n_layers, d, n = w.shape
    return pl.pallas_call(
        _adaln_body,
        grid=(n // d,),
        in_specs=[_full((SUBLANES, d)), pl.BlockSpec((1, d, d), lambda j: (layer, 0, j)),
                  pl.BlockSpec((1, 1, d), lambda j: (layer, 0, j))],
        out_specs=pl.BlockSpec((SUBLANES, d), lambda j: (0, j)),
        out_shape=jax.ShapeDtypeStruct((SUBLANES, n), F32),
        compiler_params=_cparams("arbitrary"),
        name="adaln",
    )(cond8, w, b.reshape(n_layers, 1, n))


def _mod_spec(chunk):
    return pl.BlockSpec((SUBLANES, D_MODEL), lambda t, *_: (0, chunk))


def _scale_tiles(tiles_per_group, tm):
    return None if tiles_per_group is None else tiles_per_group * ROW_TILE // tm


def _mod_row(ref, group):
    return ref[pl.ds(group, 1), :]


def _inproj_body(tiles_per_group, fixed_group, use_rope, want_q, want_hg, *refs):
    it = iter(refs)
    x_ref, sh_ref, sc_ref, g1_ref = next(it), next(it), next(it), next(it)
    wkv_ref, kvg_ref, wuk_ref, wuv_ref, wkr_ref = next(it), next(it), next(it), next(it), next(it)
    if want_q:
        wq_ref, qg_ref, wuqa_ref, wuqb_ref = next(it), next(it), next(it), next(it)
    if want_hg:
        why_ref, wgate_ref = next(it), next(it)
    if use_rope:
        cos_ref, sin_ref = next(it), next(it)
    k_out, v_out = next(it), next(it)
    if want_q:
        q_out = next(it)
    if want_hg:
        phy_out, gate_out = next(it), next(it)

    group = fixed_group if fixed_group is not None else pl.program_id(0) // tiles_per_group
    xf = x_ref[...]
    h = _rms(xf, g1_ref[...]) * (1.0 + _mod_row(sc_ref, group)) + _mod_row(sh_ref, group)
    h = h.astype(BF16)
    if use_rope:
        cos, sin = cos_ref[...], sin_ref[...]

    def rope(a, b):
        return a * cos + b * sin if use_rope else a

    ckv = _rms(_dot(h, wkv_ref[...]), kvg_ref[...]).astype(BF16)
    vt = lax.dot_general(wuv_ref[...], ckv, (((1,), (1,)), ((), ())), preferred_element_type=F32)
    vrow = lax.broadcasted_iota(jnp.int32, vt.shape, 0) & (HEAD_PAD - 1)
    vt = jnp.where(vrow == V_HEAD, 1.0, vt).astype(BF16)
    for j in range(v_out.shape[0]):
        v_out[j] = vt[:, j * VT_BLOCK:(j + 1) * VT_BLOCK]
    k_nope = _dot(ckv, wuk_ref[...])
    kr = _dot(h, wkr_ref[...])
    k_rope = rope(kr[:, :HEAD_PAD], kr[:, HEAD_PAD:])
    for hd in range(N_HEADS):
        sl = slice(hd * HEAD_PAD, (hd + 1) * HEAD_PAD)
        k_out[:, sl] = (k_nope[:, sl] + k_rope).astype(BF16)
    if want_q:
        qn = _rms(_dot(h, wq_ref[...]), qg_ref[...]).astype(BF16)
        qa = _dot(qn, wuqa_ref[...])
        qb = _dot(qn, wuqb_ref[...]) if use_rope else None
        for hd in range(N_HEADS):
            sl = slice(hd * HEAD_PAD, (hd + 1) * HEAD_PAD)
            q_out[:, sl] = (rope(qa[:, sl], None if qb is None else qb[:, sl]) * Q_SCALE).astype(BF16)
    if want_hg:
        n_hy = why_ref.shape[1]
        for c0 in range(0, n_hy, 512):
            phy_out[:, c0:c0 + 512] = _dot(h, why_ref[:, c0:c0 + 512]).astype(BF16)
        n_g = wgate_ref.shape[1]
        for c0 in range(0, n_g, 512):
            gate_out[:, c0:c0 + 512] = jax.nn.sigmoid(_dot(h, wgate_ref[:, c0:c0 + 512])).astype(BF16)


def _inproj(x, mod, lw, *, tiles_per_group, fixed_group, rope_tabs, want_q, want_hg):
    rows = x.shape[0]
    tm = INPROJ_ROW_TILE
    nt = rows // tm
    if tiles_per_group is not None:
        tiles_per_group = tiles_per_group * ROW_TILE // tm
    use_rope = rope_tabs is not None
    row_spec = lambda n: pl.BlockSpec((tm, n), lambda t: (t, 0))
    ins = [x, mod, mod, lw["norm1_g"], lw["w_kv"], lw["kv_norm_g"], lw["w_uk"], lw["w_uv"], lw["w_kr"]]
    specs = [row_spec(D_MODEL), _mod_spec(0), _mod_spec(1), _full((1, D_MODEL)),
             _full(lw["w_kv"].shape), _full((1, KV_LORA)), _full(lw["w_uk"].shape), _full(lw["w_uv"].shape),
             _full(lw["w_kr"].shape)]
    if want_q:
        ins += [lw["w_q"], lw["q_norm_g"], lw["w_uq_a"], lw["w_uq_b"]]
        specs += [_full(lw["w_q"].shape), _full((1, Q_LORA)), _full(lw["w_uq_a"].shape), _full(lw["w_uq_b"].shape)]
    if want_hg:
        ins += [lw["w_hy"], lw["w_gate"]]
        specs += [_full(lw["w_hy"].shape), _full(lw["w_gate"].shape)]
    if use_rope:
        seq_tiles = rope_tabs[0].shape[0] // tm
        ins += list(rope_tabs)
        specs += [pl.BlockSpec((tm, HEAD_PAD), lambda t: (t % seq_tiles, 0))] * 2
    hp = N_HEADS * HEAD_PAD
    vt_blocks = tm // VT_BLOCK
    out_shape = [jax.ShapeDtypeStruct((rows, hp), BF16), jax.ShapeDtypeStruct((rows // VT_BLOCK, hp, VT_BLOCK), BF16)]
    out_specs = [row_spec(hp), pl.BlockSpec((vt_blocks, hp, VT_BLOCK), lambda t: (t, 0, 0))]
    if want_q:
        out_shape.append(jax.ShapeDtypeStruct((rows, hp), BF16))
        out_specs.append(row_spec(hp))
    if want_hg:
        out_shape += [jax.ShapeDtypeStruct((rows, 3 * HY_WIDTH), BF16), jax.ShapeDtypeStruct((rows, 2 * D_MODEL), BF16)]
        out_specs += [row_spec(3 * HY_WIDTH), row_spec(2 * D_MODEL)]
    outs = pl.pallas_call(
        functools.partial(_inproj_body, tiles_per_group, fixed_group, use_rope, want_q, want_hg),
        grid=(nt,), in_specs=specs, out_specs=out_specs, out_shape=out_shape,
        compiler_params=_cparams("arbitrary"), name="inproj",
    )(*ins)
    res = {"k": outs[0], "v": outs[1]}
    i = 2
    if want_q:
        res["q"] = outs[i]
        i += 1
    if want_hg:
        res["p_hy"], res["gates"] = outs[i], outs[i + 1]
    return res


def _attn_body(n_lat_blocks, *refs):
    if n_lat_blocks:
        q_ref, kc_ref, vc_ref, kl_ref, vl_ref, o_ref = refs
    else:
        q_ref, kc_ref, vc_ref, o_ref = refs
    tq = q_ref.shape[0]
    q = q_ref[...]
    sub = ATTN_KV // VT_BLOCK

    def run(heads):
        nh = len(heads)

        def scores(kblk):
            return tuple(lax.dot_general(kblk[:, sl], q[:, sl], (((1,), (1,)), ((), ())),
                                         preferred_element_type=F32) for sl in heads)

        def update(st, vt_blocks, carry):
            m_new = [jnp.maximum(carry[hd][0], jnp.max(st[hd], axis=0, keepdims=True)) for hd in range(nh)]
            p = [jnp.exp2(st[hd] - m_new[hd]).astype(BF16) for hd in range(nh)]
            out = []
            for hd in range(nh):
                m, acc = carry[hd]
                pv = None
                for j, vt in enumerate(vt_blocks):
                    r = _dot(vt[heads[hd], :], p[hd][j * VT_BLOCK:(j + 1) * VT_BLOCK])
                    pv = r if pv is None else pv + r
                out.append((m_new[hd], jnp.exp2(m - m_new[hd]) * acc + pv))
            return tuple(out)

        init = tuple((jnp.full((1, tq), -jnp.inf, F32), jnp.zeros((HEAD_PAD, tq), F32)) for _ in range(nh))
        carry = update(scores(kc_ref[0]), [vc_ref[0, 0]], init)
        if n_lat_blocks:
            def k_block(i):
                return kl_ref[0, pl.ds(i * ATTN_KV, ATTN_KV), :]

            def v_blocks(i):
                return [vl_ref[0, i * sub + j] for j in range(sub)]

            st = scores(k_block(0))
            for i in range(n_lat_blocks - 1):
                st_next = scores(k_block(i + 1))
                carry = update(st, v_blocks(i), carry)
                st = st_next
            carry = update(st, v_blocks(n_lat_blocks - 1), carry)
        return [acc[:V_HEAD] / acc[V_HEAD:V_HEAD + 1] for _, acc in carry]

    slices = [slice(hd * HEAD_PAD, (hd + 1) * HEAD_PAD) for hd in range(2)]
    outs = []
    for group in ATTN_HEAD_GROUPS:
        outs += run([slices[hd] for hd in group])
    o_ref[...] = jnp.concatenate(outs, axis=0).T.astype(BF16)


def _attention(q, k_ctx, vt_ctx, k_lat, vt_lat, batch):
    rows = q.shape[0]
    tq = min(ATTN_Q_TILE, rows // batch)
    qt_per_b = rows // batch // tq
    ctx_len = k_ctx.shape[1]
    assert ctx_len == VT_BLOCK
    pair_w = 2 * HEAD_PAD
    ins = [q, k_ctx, vt_ctx]
    specs = [pl.BlockSpec((tq, pair_w), lambda b, hp, t: (b * qt_per_b + t, hp)),
             pl.BlockSpec((1, ctx_len, pair_w), lambda b, hp, t: (b, 0, hp)),
             pl.BlockSpec((1, 1, pair_w, VT_BLOCK), lambda b, hp, t: (b, 0, hp, 0))]
    n_lat_blocks = 0
    if k_lat is not None:
        lat_len = k_lat.shape[1]
        n_lat_blocks = lat_len // ATTN_KV
        ins += [k_lat, vt_lat]
        specs += [pl.BlockSpec((1, lat_len, pair_w), lambda b, hp, t: (b, 0, hp)),
                  pl.BlockSpec((1, lat_len // VT_BLOCK, pair_w, VT_BLOCK), lambda b, hp, t: (b, 0, hp, 0))]
    return pl.pallas_call(
        functools.partial(_attn_body, n_lat_blocks),
        grid=(batch, N_HEADS // 2, qt_per_b), in_specs=specs,
        out_specs=pl.BlockSpec((tq, 2 * V_HEAD), lambda b, hp, t: (b * qt_per_b + t, hp)),
        out_shape=jax.ShapeDtypeStruct((rows, N_HEADS * V_HEAD), BF16),
        compiler_params=_cparams("arbitrary", "arbitrary", "arbitrary"), name="attention",
    )(*ins)


def _hypre_body(seq_tiles, n_chunks, p_ref, prev_ref, next_ref, w_ref, b_ref, z_ref, x0_ref, *zt_ref):
    j = pl.program_id(0) % seq_tiles
    p = p_ref[...].astype(F32)
    tm = p.shape[0]
    row = lax.broadcasted_iota(jnp.int32, (tm, 1), 0)
    prev_row = jnp.where(j != 0, prev_ref[...].astype(F32)[HALO_ROWS - 1:HALO_ROWS, :], 0.0)
    next_row = jnp.where(j != seq_tiles - 1, next_ref[...].astype(F32)[0:1, :], 0.0)
    up = jnp.where(row == 0, prev_row, pltpu.roll(p, 1, 0))
    dn = jnp.where(row == tm - 1, next_row, pltpu.roll(p, tm - 1, 0))
    u = up * w_ref[0:1, :] + p * w_ref[1:2, :] + dn * w_ref[2:3, :] + b_ref[...]
    c = HY_WIDTH
    z = u[:, :c] * u[:, c:2 * c]
    z_ref[...] = z.astype(BF16)
    x0_ref[...] = u[:, 2 * c:].astype(BF16)
    if zt_ref:
        q = tm // n_chunks
        for a in range(n_chunks):
            zt_ref[0][0, :, a * c:(a + 1) * c] = z[a * q:(a + 1) * q, :].astype(BF16)


def _hyena_pre(p_hy, short_w, short_b, batch, dft_q):
    rows = p_hy.shape[0]
    n = rows // batch
    tm = min(STREAM_ROW_TILE, n)
    seq_tiles = n // tm
    nt = rows // tm
    c3 = 3 * HY_WIDTH
    halo = tm // HALO_ROWS
    last_halo = rows // HALO_ROWS - 1
    specs = [pl.BlockSpec((tm, c3), lambda t: (t, 0)),
             pl.BlockSpec((HALO_ROWS, c3), lambda t: (jnp.maximum(t * halo - 1, 0), 0)),
             pl.BlockSpec((HALO_ROWS, c3), lambda t: (jnp.minimum((t + 1) * halo, last_halo), 0)),
             _full((3, c3)), _full((1, c3))]
    out_shape = [jax.ShapeDtypeStruct((rows, HY_WIDTH), BF16)] * 2
    out_specs = [pl.BlockSpec((tm, HY_WIDTH), lambda t: (t, 0))] * 2
    n_chunks = 1
    if dft_q is not None:
        n_chunks = tm // dft_q
        half_p = n // dft_q
        out_shape.append(jax.ShapeDtypeStruct((batch // 2, dft_q, 2 * half_p * HY_WIDTH), BF16))

        def zt_map(t):
            b, jt = t // seq_tiles, t % seq_tiles
            return (b // 2, 0, (b % 2) * seq_tiles + jt)
        out_specs.append(pl.BlockSpec((1, dft_q, n_chunks * HY_WIDTH), zt_map))
    return pl.pallas_call(
        functools.partial(_hypre_body, seq_tiles, n_chunks),
        grid=(nt,), in_specs=specs, out_specs=out_specs, out_shape=out_shape,
        compiler_params=_cparams("arbitrary"), name="hyena_pre",
    )(p_hy, p_hy, p_hy, short_w, short_b.reshape(1, c3))


def _filter_body(emb_ref, w1_ref, b1_ref, w2_ref, b2_ref, w3_ref, freq_ref, decay_ref, h0_ref, h1_ref, s_ref):
    a = pl.program_id(0)
    emb = emb_ref[...]
    freq = freq_ref[...]
    h = jnp.sin(freq * (_dot_hi(emb, w1_ref[...]) + b1_ref[...]))
    h = jnp.sin(freq * (_dot_hi(h, w2_ref[...]) + b2_ref[...]))
    h = _dot_hi(h, w3_ref[...]) * jnp.exp(-emb[:, 0:1] * jnp.abs(decay_ref[...]))
    c = HY_WIDTH
    row = lax.broadcasted_iota(jnp.int32, (emb.shape[0], 1), 0)
    h0 = h[:, :c]
    h1 = jnp.where(jnp.logical_and(a == 0, row == 0), 0.0, h[:, c:])
    q = h0_ref.shape[1]
    for j in range(h0_ref.shape[2] // c):
        h0_ref[0, :, j * c:(j + 1) * c] = h0[j * q:(j + 1) * q].astype(BF16)
        h1_ref[0, :, j * c:(j + 1) * c] = h1[j * q:(j + 1) * q].astype(BF16)
    part = jnp.sum(jnp.abs(h0) + jnp.abs(h1), axis=0, keepdims=True)

    @pl.when(a == 0)
    def _():
        s_ref[...] = jnp.zeros_like(s_ref)
    s_ref[...] += part


def _hyena_filter(n, q, lw):
    f32 = F32
    bands = (HY_EMB - 1) // 2
    t = jnp.linspace(0.0, 1.0, n, dtype=f32)[:, None]
    phase = (2.0 * math.pi / n) * jnp.arange(n, dtype=f32)[:, None] * jnp.linspace(1e-4, bands - 1, bands, dtype=f32)
    emb = jnp.concatenate([t, jnp.cos(phase), -jnp.sin(phase), jnp.zeros((n, 32 - HY_EMB), f32)], axis=-1)
    w1 = jnp.concatenate([lw["hy_w1"], jnp.zeros((32 - HY_EMB, HY_HIDDEN), f32)], axis=0)
    c = HY_WIDTH
    slots = n // q
    per_step = min(FILTER_CHUNKS, slots)
    row = lambda v: v.reshape(1, -1)
    return pl.pallas_call(
        _filter_body,
        grid=(slots // per_step,),
        in_specs=[pl.BlockSpec((per_step * q, 32), lambda a: (a, 0)), _full((32, HY_HIDDEN)), _full((1, HY_HIDDEN)),
                  _full((HY_HIDDEN, HY_HIDDEN)), _full((1, HY_HIDDEN)), _full((HY_HIDDEN, 2 * c)),
                  _full((1, HY_HIDDEN)), _full((1, 2 * c))],
        out_specs=[pl.BlockSpec((1, q, per_step * c), lambda a: (0, 0, a)),
                   pl.BlockSpec((1, q, per_step * c), lambda a: (0, 0, a)),
                   pl.BlockSpec((1, c), lambda a: (0, 0))],
        out_shape=[jax.ShapeDtypeStruct((1, q, slots * c), BF16)] * 2 + [jax.ShapeDtypeStruct((1, c), f32)],
        compiler_params=_cparams("arbitrary"), name="hyena_filter",
    )(emb, w1, row(lw["hy_b1"]), lw["hy_w2"], row(lw["hy_b2"]), lw["hy_w3"], row(lw["hy_freq"]),
      row(lw["hy_decay"]))


def _colslot_body(nparts, bq, c, *refs):
    x_refs, t_refs, o_ref = refs[:nparts], refs[nparts:2 * nparts], refs[-1]
    for j in range(bq):
        acc = None
        for x_ref, t_ref in zip(x_refs, t_refs):
            r = _dot_dft(t_ref[j], x_ref[0, j])
            acc = r if acc is None else acc + r
        o_ref[0, :, j * c:(j + 1) * c] = acc.astype(o_ref.dtype)


def _colslot_matmul(xs, tabs, out_dtype, c=HY_WIDTH):
    tabs = [jnp.asarray(t, BF16) for t in tabs]
    g, q = xs[0].shape[0], xs[0].shape[1]
    m = tabs[0].shape[1]
    bq = COLSLOT_BQ
    xs4 = [x.reshape(g, q, -1, c) for x in xs]
    specs = [pl.BlockSpec((1, bq, x.shape[2], c), lambda gi, qi: (gi, qi, 0, 0)) for x in xs4]
    specs += [pl.BlockSpec((bq, m, t.shape[2]), lambda gi, qi: (qi, 0, 0)) for t in tabs]
    return pl.pallas_call(
        functools.partial(_colslot_body, len(xs), bq, c),
        grid=(g, q // bq), in_specs=specs,
        out_specs=pl.BlockSpec((1, m, bq * c), lambda gi, qi: (gi, 0, qi)),
        out_shape=jax.ShapeDtypeStruct((g, m, q * c), out_dtype),
        compiler_params=_cparams("arbitrary", "arbitrary"), name="dft_stride_stage",
    )(*xs4, *tabs)


def _spec_mid_body(g_ref, m3_ref, m3i_ref, k_ref, hr_ref, hi_ref):
    bp, q = hr_ref.shape[0], hr_ref.shape[1]
    c = hr_ref.shape[2] // DFT_CB
    for j in range(DFT_CB):
        kh = k_ref[j]
        kr, ki = kh[:q], kh[q:]
        for b in range(bp):
            x = _dot_dft(m3_ref[...], g_ref[b, :, j].reshape(2 * q, -1))
            xr, xi = x[:q], x[q:]
            y = jnp.concatenate([xr * kr - xi * ki, xr * ki + xi * kr], axis=0)
            h = _dot_dft(m3i_ref[...], y)
            hr_ref[b, :, j * c:(j + 1) * c] = h[:q].astype(BF16)
            hi_ref[b, :, j * c:(j + 1) * c] = h[q:].astype(BF16)


def _spec_mid(gf, m3, m3i, khat, p, q, c=HY_WIDTH):
    bp = gf.shape[0]
    g5 = gf.reshape(bp, 2, p, q, c)
    cb = DFT_CB
    return pl.pallas_call(
        _spec_mid_body,
        grid=(p // cb,),
        in_specs=[pl.BlockSpec((bp, 2, cb, q, c), lambda ci: (0, 0, ci, 0, 0)), _full((2 * q, 2 * q)),
                  _full((2 * q, 2 * q)), pl.BlockSpec((cb, 2 * q, c), lambda ci: (ci, 0, 0))],
        out_specs=[pl.BlockSpec((bp, q, cb * c), lambda ci: (0, 0, ci))] * 2,
        out_shape=[jax.ShapeDtypeStruct((bp, q, p * c), BF16)] * 2,
        compiler_params=_cparams("arbitrary"), name="dft_mid",
    )(g5, m3, m3i, khat)


def _filter_spec_body(g_ref, m3k_ref, s_ref, k_ref):
    q4 = m3k_ref.shape[1]
    inv = 1.0 / s_ref[...]
    for j in range(DFT_CB):
        k_ref[j] = _dot_dft(m3k_ref[...], g_ref[0, :, j].reshape(q4, -1)) * inv


def _filter_spec(gk, m3k, s, p, q, c=HY_WIDTH):
    g5 = gk.reshape(1, 4, p, q, c)
    cb = DFT_CB
    return pl.pallas_call(
        _filter_spec_body,
        grid=(p // cb,),
        in_specs=[pl.BlockSpec((1, 4, cb, q, c), lambda ci: (0, 0, ci, 0, 0)), _full((2 * q, 4 * q)), _full((1, c))],
        out_specs=pl.BlockSpec((cb, 2 * q, c), lambda ci: (ci, 0, 0)),
        out_shape=jax.ShapeDtypeStruct((p, 2 * q, c), F32),
        compiler_params=_cparams("arbitrary"), name="filter_spectrum",
    )(g5, m3k, s)


def _direct_conv_body(z_ref, h0_ref, h1_ref, s_ref, tf_ref, tk_ref, ti_ref, o_ref):
    n2 = tf_ref.shape[0] // 2
    kh = _dot_dft(tk_ref[...], jnp.concatenate([h0_ref[0], h1_ref[0]], axis=0)) / s_ref[...]
    x = _dot_dft(tf_ref[...], z_ref[0])
    xr, xi, kr, ki = x[:n2], x[n2:], kh[:n2], kh[n2:]
    y = jnp.concatenate([xr * kr - xi * ki, xr * ki + xi * kr], axis=0)
    o_ref[0] = _dot_dft(ti_ref[...], y)


def _direct_conv(z, h0, h1, s, n, c=HY_WIDTH):
    bp = z.shape[0] // (2 * n)
    tf, tk, ti = (jnp.asarray(t, BF16) for t in _direct_tables(n))
    out = pl.pallas_call(
        _direct_conv_body,
        grid=(bp,),
        in_specs=[pl.BlockSpec((1, 2 * n, c), lambda b: (b, 0, 0)), _full((1, n, c)), _full((1, n, c)), _full((1, c)),
                  _full(tf.shape), _full(tk.shape), _full(ti.shape)],
        out_specs=pl.BlockSpec((1, 2 * n, c), lambda b: (b, 0, 0)),
        out_shape=jax.ShapeDtypeStruct((bp, 2 * n, c), F32),
        compiler_params=_cparams("arbitrary"), name="direct_conv",
    )(z.reshape(bp, 2 * n, c), h0, h1, s, tf, tk, ti)
    return out.reshape(bp * 2 * n, c)


def _cis(m, n_total):
    ang = (2.0 * np.pi / n_total) * (m % n_total).astype(np.float64)
    return np.cos(ang), -np.sin(ang)


@functools.lru_cache(maxsize=None)
def _direct_tables(n):
    nn = 2 * n
    f = np.arange(nn)[:, None]
    t = np.arange(n)[None, :]
    cr, ci = _cis(f * t, nn)
    tf = np.block([[cr, -ci], [ci, cr]])
    tk = np.block([[cr, cr], [ci, -ci]])
    ti = np.block([[cr.T, ci.T], [-ci.T, cr.T]]) / nn
    return tuple(a.astype(np.float32) for a in (tf, tk, ti))


@functools.lru_cache(maxsize=None)
def _twostage_tables(p, q):
    nn = p * q
    hp = p // 2
    s = np.arange(q)[:, None, None]
    c = np.arange(p)[None, :, None]
    a = np.arange(hp)[None, None, :]
    tr, ti = _cis(c * (q * a + s), nn)
    z = np.zeros_like(tr)
    t_data = np.concatenate([np.concatenate([tr, -ti], 2), np.concatenate([ti, tr], 2)], 1)
    t_k0 = np.concatenate([tr, ti, z, z], 1)
    t_k1 = np.concatenate([z, z, tr, -ti], 1)
    trt, tit = np.swapaxes(tr, 1, 2) / nn, np.swapaxes(ti, 1, 2) / nn
    t_inv_r = np.concatenate([trt, -tit], 1)
    t_inv_i = np.concatenate([tit, trt], 1)
    d = np.arange(q)[:, None]
    b = np.arange(q)[None, :]
    fr, fi = _cis(d * b, q)
    m3 = np.block([[fr, -fi], [fi, fr]])
    m3i = np.block([[fr, fi], [-fi, fr]])
    m3k = np.block([[fr, -fi, fr, fi], [fi, fr, -fi, fr]])
    return tuple(x.astype(np.float32) for x in (t_data, t_k0, t_k1, t_inv_r, t_inv_i, m3, m3i, m3k))


def _long_conv_spectrum(n, lw):
    p, q = DFT_P, DFT_Q
    h0, h1, s = _hyena_filter(n, q, lw)
    _, t_k0, t_k1, _, _, _, _, m3k = _twostage_tables(p, q)
    gk = _colslot_matmul([h0, h1], [t_k0, t_k1], BF16)
    return _filter_spec(gk, jnp.asarray(m3k, BF16), s, p, q)


def _long_conv(zt, khat):
    p, q = DFT_P, DFT_Q
    t_data, _, _, t_inv_r, t_inv_i, m3, m3i, _ = _twostage_tables(p, q)
    gf = _colslot_matmul([zt], [t_data], BF16)
    hr, hi = _spec_mid(gf, jnp.asarray(m3, BF16), jnp.asarray(m3i, BF16), khat, p, q)
    y = _colslot_matmul([hr, hi], [t_inv_r, t_inv_i], BF16)
    return y.reshape(-1, HY_WIDTH)


def _merge_body(tiles_per_group, fixed_group, x_ref, g1_ref, oa_ref, conv_ref, z_ref, x0_ref, gate_ref, bias_ref,
                wba_ref, wbh_ref, wout_ref, o_ref):
    group = fixed_group if fixed_group is not None else pl.program_id(0) // tiles_per_group
    o_hy = ((conv_ref[...] + z_ref[...] * bias_ref[...]) * x0_ref[...]).astype(BF16)
    d = D_MODEL
    merged = gate_ref[:, :d] * _dot(oa_ref[...], wba_ref[...]) + gate_ref[:, d:] * _dot(o_hy, wbh_ref[...])
    mix = _dot(merged.astype(BF16), wout_ref[...])
    o_ref[...] = x_ref[...] + _mod_row(g1_ref, group) * mix


def _merge(x, mod, o_attn, conv, z, x0, gates, lw, *, tiles_per_group, fixed_group):
    rows = x.shape[0]
    tm = STREAM_ROW_TILE
    row_spec = lambda n: pl.BlockSpec((tm, n), lambda t: (t, 0))
    c = HY_WIDTH
    return pl.pallas_call(
        functools.partial(_merge_body, _scale_tiles(tiles_per_group, tm), fixed_group),
        grid=(rows // tm,),
        in_specs=[row_spec(D_MODEL), _mod_spec(2), row_spec(N_HEADS * V_HEAD), row_spec(c), row_spec(c), row_spec(c),
                  row_spec(2 * D_MODEL), _full((1, c)), _full(lw["w_br_attn"].shape), _full(lw["w_br_hy"].shape),
                  _full(lw["w_out"].shape)],
        out_specs=row_spec(D_MODEL),
        out_shape=jax.ShapeDtypeStruct((rows, D_MODEL), F32),
        compiler_params=_cparams("arbitrary"), name="merge_out",
    )(x, mod, o_attn, conv, z, x0, gates, lw["hy_bias"], lw["w_br_attn"], lw["w_br_hy"], lw["w_out"])


def _swiglu(h, wg_ref, wu_ref, wd_ref):
    acc = None
    for c0, c1 in FFN_SPANS:
        gate = _dot(h, wg_ref[0, :, c0:c1])
        up = _dot(h, wu_ref[0, :, c0:c1])
        y = _dot((gate * jax.nn.sigmoid(gate) * up).astype(BF16), wd_ref[0, c0:c1, :])
        acc = y if acc is None else acc + y
    return acc


def _ffn_body(tiles_per_group, fixed_group, x_ref, sh_ref, sc_ref, g2_ref, n2_ref, wg_ref, wu_ref, wd_ref, o_ref):
    group = fixed_group if fixed_group is not None else pl.program_id(0) // tiles_per_group
    x = x_ref[...]
    h = (_rms(x, n2_ref[...]) * (1.0 + _mod_row(sc_ref, group)) + _mod_row(sh_ref, group)).astype(BF16)
    o_ref[...] = x + _mod_row(g2_ref, group) * _swiglu(h, wg_ref, wu_ref, wd_ref)


def _ffn(x, mod, norm_g, w_gate, w_up, w_down, *, tiles_per_group, fixed_group):
    rows = x.shape[0]
    tm = min(FFN_ROW_TILE, rows)
    row_spec = pl.BlockSpec((tm, D_MODEL), lambda t: (t, 0))
    return pl.pallas_call(
        functools.partial(_ffn_body, tiles_per_group * ROW_TILE // tm if tiles_per_group else None, fixed_group),
        grid=(rows // tm,),
        in_specs=[row_spec, _mod_spec(3), _mod_spec(4), _mod_spec(5), _full((1, D_MODEL)),
                  _full(w_gate.shape), _full(w_up.shape), _full(w_down.shape)],
        out_specs=row_spec,
        out_shape=jax.ShapeDtypeStruct((rows, D_MODEL), F32),
        compiler_params=_cparams("arbitrary"), name="ffn",
    )(x, mod, mod, mod, norm_g, w_gate, w_up, w_down)


def _route_body(tiles_per_group, x_ref, sh_ref, sc_ref, n2_ref, router_ref, h_ref, route_ref):
    group = pl.program_id(0) // tiles_per_group
    h = _rms(x_ref[...], n2_ref[...]) * (1.0 + _mod_row(sc_ref, group)) + _mod_row(sh_ref, group)
    h_ref[...] = h
    logits = _dot_split(h, router_ref[...])
    lane = lax.broadcasted_iota(jnp.int32, logits.shape, 1)
    logits = jnp.where(lane < N_EXPERTS, logits, -jnp.inf)
    m1 = jnp.max(logits, axis=-1, keepdims=True)
    i1 = jnp.min(jnp.where(logits == m1, lane, LANES), axis=-1, keepdims=True)
    rest = jnp.where(lane == i1, -jnp.inf, logits)
    m2 = jnp.max(rest, axis=-1, keepdims=True)
    i2 = jnp.min(jnp.where(rest == m2, lane, LANES), axis=-1, keepdims=True)
    e2 = jnp.exp(m2 - m1)
    w1 = 1.0 / (1.0 + e2)
    route_ref[...] = (jnp.where(lane == 0, i1.astype(F32), 0.0) + jnp.where(lane == 1, i2.astype(F32), 0.0)
                      + jnp.where(lane == 2, w1, 0.0) + jnp.where(lane == 3, e2 * w1, 0.0))


def _route(x, mod, norm_g, router, tiles_per_group):
    rows = x.shape[0]
    tm = STREAM_ROW_TILE
    row_spec = lambda n: pl.BlockSpec((tm, n), lambda t: (t, 0))
    return pl.pallas_call(
        functools.partial(_route_body, _scale_tiles(tiles_per_group, tm)),
        grid=(rows // tm,),
        in_specs=[row_spec(D_MODEL), _mod_spec(3), _mod_spec(4), _full((1, D_MODEL)), _full(router.shape)],
        out_specs=[row_spec(D_MODEL), row_spec(LANES)],
        out_shape=[jax.ShapeDtypeStruct((rows, D_MODEL), F32), jax.ShapeDtypeStruct((rows, LANES), F32)],
        compiler_params=_cparams("arbitrary"), name="moe_route",
    )(x, mod, mod, norm_g, router)


def _gather_rows(idx_ref, n_rows, src_hbm, dst, sem):
    def issue(i, carry):
        for prio in range(2):
            r = 2 * i + prio
            pltpu.make_async_copy(src_hbm.at[pl.ds(idx_ref[0, 0, r], 1), :], dst.at[pl.ds(r, 1), :],
                                  sem).start(priority=prio)
        return carry
    lax.fori_loop(0, n_rows // 2, issue, 0, unroll=8)


def _wait_rows(n_rows, src_hbm, dst, sem):
    pltpu.make_async_copy(src_hbm.at[pl.ds(0, n_rows), :], dst, sem).wait()


def _moe_dispatch_body(idx_ref, h_ref, xs_in_hbm, xs_hbm, stage, sem):
    del xs_in_hbm
    t = pl.program_id(0)
    n_tiles = pl.num_programs(0)
    tm = h_ref.shape[0]
    slot = t % 2

    def wait(s):
        for _ in range(2):
            pltpu.make_async_copy(stage.at[s], xs_hbm.at[pl.ds(0, tm), :], sem.at[s]).wait()

    @pl.when(t >= 2)
    def _():
        wait(slot)

    stage[slot] = h_ref[...]

    def issue(r, carry):
        row = stage.at[slot, pl.ds(r, 1), :]
        pltpu.make_async_copy(row, xs_hbm.at[pl.ds(idx_ref[0, 0, r], 1), :], sem.at[slot]).start(priority=0)
        pltpu.make_async_copy(row, xs_hbm.at[pl.ds(idx_ref[0, 0, tm + r], 1), :], sem.at[slot]).start(priority=1)
        return carry
    lax.fori_loop(0, tm, issue, 0, unroll=8)

    @pl.when(t == n_tiles - 1)
    def _():
        wait(slot)
        wait(1 - slot)


def _tile_index_blocks(dest, tm):
    n_tiles = dest.shape[0] // tm
    return dest.reshape(n_tiles, tm, 2).transpose(0, 2, 1).reshape(n_tiles, 1, 2 * tm)


def _moe_dispatch(h, dest, n_rows):
    tm = STREAM_ROW_TILE
    idx = _tile_index_blocks(dest, tm)
    n_tiles = idx.shape[0]
    assert n_tiles >= 2
    return pl.pallas_call(
        _moe_dispatch_body,
        grid=(n_tiles,),
        in_specs=[pl.BlockSpec((1, 1, 2 * tm), lambda t: (t, 0, 0), memory_space=pltpu.SMEM),
                  pl.BlockSpec((tm, D_MODEL), lambda t: (t, 0)), pl.BlockSpec(memory_space=pl.ANY)],
        out_specs=pl.BlockSpec(memory_space=pl.ANY),
        out_shape=jax.ShapeDtypeStruct((n_rows, D_MODEL), F32),
        scratch_shapes=[pltpu.VMEM((2, tm, D_MODEL), F32), pltpu.SemaphoreType.DMA((2,))],
        input_output_aliases={2: 0},
        compiler_params=_cparams("arbitrary"), name="moe_dispatch",
    )(idx, h, jnp.zeros((n_rows, D_MODEL), F32))


def _moe_group_body(te_ref, nu_ref, x_ref, wg_ref, wu_ref, wd_ref, y_ref):
    t = pl.program_id(0)

    @pl.when(t < nu_ref[0])
    def _():
        y_ref[...] = _swiglu(x_ref[...].astype(BF16), wg_ref, wu_ref, wd_ref)

    @pl.when(t >= nu_ref[0])
    def _():
        y_ref[...] = jnp.zeros_like(y_ref)


def _moe_group(xs, tile_expert, n_used, w_gate, w_up, w_down):
    tm = MOE_ROW_TILE
    n_rows = xs.shape[0]
    w_spec = lambda w: pl.BlockSpec((1,) + w.shape[1:], lambda t, te, nu: (te[t], 0, 0))
    grid_spec = pltpu.PrefetchScalarGridSpec(
        num_scalar_prefetch=2,
        grid=(n_rows // tm,),
        in_specs=[pl.BlockSpec((tm, D_MODEL), lambda t, te, nu: (jnp.minimum(t, nu[0] - 1), 0)),
                  w_spec(w_gate), w_spec(w_up), w_spec(w_down)],
        out_specs=pl.BlockSpec((tm, D_MODEL), lambda t, te, nu: (t, 0)),
    )
    return pl.pallas_call(
        _moe_group_body, grid_spec=grid_spec,
        out_shape=jax.ShapeDtypeStruct((n_rows, D_MODEL), F32),
        compiler_params=_cparams("arbitrary"), name="moe_group",
    )(tile_expert, n_used, xs, w_gate, w_up, w_down)


def _moe_combine_body(tiles_per_group, idx_ref, idx_next_ref, x_ref, g2_ref, fg_ref, route_ref, y_hbm, o_ref,
                      yg_scr, sem):
    t = pl.program_id(0)
    n_tiles = pl.num_programs(0)
    group = t // tiles_per_group
    tm = x_ref.shape[0]
    slot = t % 2

    @pl.when(t == 0)
    def _():
        _gather_rows(idx_ref, 2 * tm, y_hbm, yg_scr.at[0], sem.at[0])

    _wait_rows(2 * tm, y_hbm, yg_scr.at[slot], sem.at[slot])

    @pl.when(t + 1 < n_tiles)
    def _():
        _gather_rows(idx_next_ref, 2 * tm, y_hbm, yg_scr.at[1 - slot], sem.at[1 - slot])

    route = route_ref[...]
    y = route[:, 2:3] * yg_scr[slot, :tm] + route[:, 3:4] * yg_scr[slot, tm:]
    o_ref[...] = _rms(x_ref[...] + _mod_row(g2_ref, group) * y, fg_ref[...])


def _moe_combine(x, mod, final_g, route, y_sorted, dest, tiles_per_group):
    rows = x.shape[0]
    tm = STREAM_ROW_TILE
    idx = _tile_index_blocks(dest, tm)
    n_tiles = idx.shape[0]
    return pl.pallas_call(
        functools.partial(_moe_combine_body, _scale_tiles(tiles_per_group, tm)),
        grid=(n_tiles,),
        in_specs=[pl.BlockSpec((1, 1, 2 * tm), lambda t: (t, 0, 0), memory_space=pltpu.SMEM),
                  pl.BlockSpec((1, 1, 2 * tm), lambda t: (jnp.minimum(t + 1, n_tiles - 1), 0, 0),
                               memory_space=pltpu.SMEM),
                  pl.BlockSpec((tm, D_MODEL), lambda t: (t, 0)), _mod_spec(5), _full((1, D_MODEL)),
                  pl.BlockSpec((tm, LANES), lambda t: (t, 0)), pl.BlockSpec(memory_space=pl.ANY)],
        out_specs=pl.BlockSpec((tm, D_MODEL), lambda t: (t, 0)),
        out_shape=jax.ShapeDtypeStruct((rows, D_MODEL), F32),
        scratch_shapes=[pltpu.VMEM((2, 2 * tm, D_MODEL), F32), pltpu.SemaphoreType.DMA((2,))],
        compiler_params=_cparams("arbitrary"), name="moe_combine",
    )(idx, idx, x, mod, final_g, route, y_sorted)


def _moe_plan(route, tm):
    t = route.shape[0]
    experts = jnp.concatenate([route[:, 0], route[:, 1]]).astype(jnp.int32)
    onehot = (experts[:, None] == jnp.arange(N_EXPERTS, dtype=jnp.int32)[None, :]).astype(jnp.int32)
    csum = jnp.cumsum(onehot, axis=0)
    rank = jnp.sum(csum * onehot, axis=1) - 1
    padded = (csum[-1] + tm - 1) // tm * tm
    ends = jnp.cumsum(padded)
    dest = jnp.sum((ends - padded)[None, :] * onehot, axis=1) + rank
    n_rows = 2 * t + N_EXPERTS * tm
    tile_start = jnp.arange(n_rows // tm, dtype=jnp.int32) * tm
    tile_expert = jnp.minimum(jnp.sum((tile_start[:, None] >= ends[None, :]).astype(jnp.int32), axis=1), N_EXPERTS - 1)
    n_used = (ends[-1:] // tm).astype(jnp.int32)
    return dest.reshape(2, t).T, tile_expert, n_used, n_rows


def _moe(x, mod, norm_g, router, w_gate, w_up, w_down, final_g, tiles_per_group):
    h, route = _route(x, mod, norm_g, router, tiles_per_group)
    dest, tile_expert, n_used, n_rows = _moe_plan(route, MOE_ROW_TILE)
    xs = _moe_dispatch(h, dest, n_rows)
    y_sorted = _moe_group(xs, tile_expert, n_used, w_gate, w_up, w_down)
    return _moe_combine(x, mod, final_g, route, y_sorted, dest, tiles_per_group)


def _pad_heads(w, width):
    k = w.shape[0]
    w = w.reshape(k, N_HEADS, width)
    return jnp.pad(w, ((0, 0), (0, 0), (0, HEAD_PAD - width))).reshape(k, N_HEADS * HEAD_PAD)


def _rot_cols(w):
    half = QK_ROPE // 2
    return jnp.concatenate([-w[..., half:], w[..., :half]], axis=-1)


def _layer_weights(p, layer):
    cols = lambda a, b: p["w_in"][layer, :, a:b]
    w_uq = p["w_uq"][layer].reshape(Q_LORA, N_HEADS, QK_NOPE + QK_ROPE)
    zeros_nope = jnp.zeros((Q_LORA, N_HEADS, QK_NOPE), F32)
    w_uq_b = jnp.concatenate([zeros_nope, _rot_cols(w_uq[..., QK_NOPE:])], axis=-1)
    w_kr = cols(KV_START + KV_LORA, HY_START)
    zk = jnp.zeros((D_MODEL, QK_NOPE), F32)
    zp = jnp.zeros((D_MODEL, HEAD_PAD - QK_NOPE - QK_ROPE), F32)
    w_kr2 = jnp.concatenate([zk, w_kr, zp, zk, _rot_cols(w_kr), zp], axis=-1)
    row = lambda v: v.reshape(1, -1)
    bf = lambda v: v.astype(BF16)
    return {
        "norm1_g": row(p["norm1_g"][layer]), "norm2_g": row(p["norm2_g"][layer]),
        "w_q": bf(cols(0, Q_LORA)), "q_norm_g": row(p["q_norm_g"][layer]),
        "w_uq_a": bf(_pad_heads(w_uq.reshape(Q_LORA, -1), QK_NOPE + QK_ROPE)),
        "w_uq_b": bf(_pad_heads(w_uq_b.reshape(Q_LORA, -1), QK_NOPE + QK_ROPE)),
        "w_kv": bf(cols(KV_START, KV_START + KV_LORA)), "kv_norm_g": row(p["kv_norm_g"][layer]),
        "w_uk": bf(_pad_heads(p["w_uk"][layer], QK_NOPE)), "w_uv": bf(_pad_heads(p["w_uv"][layer], V_HEAD).T),
        "w_kr": bf(w_kr2),
        "w_hy": bf(cols(HY_START, GATE_START)), "w_gate": bf(cols(GATE_START, GATE_START + 2 * D_MODEL)),
        "hy_short_w": p["hy_short_w"][layer], "hy_short_b": p["hy_short_b"][layer],
        "hy_w1": p["hy_w1"][layer], "hy_b1": p["hy_b1"][layer], "hy_w2": p["hy_w2"][layer],
        "hy_b2": p["hy_b2"][layer], "hy_w3": p["hy_w3"][layer], "hy_freq": p["hy_freq"][layer],
        "hy_decay": p["hy_decay"][layer], "hy_bias": row(p["hy_bias"][layer]),
        "w_br_attn": bf(p["w_br_attn"][layer]), "w_br_hy": bf(p["w_br_hy"][layer]), "w_out": bf(p["w_out"][layer]),
    }


def _rope_tables(n):
    rows = n // GRID_W
    n_freq = QK_ROPE // 4
    inv_freq = ROPE_BASE ** (-jnp.arange(n_freq, dtype=F32) / n_freq)
    r = jnp.repeat(jnp.arange(rows, dtype=F32), GRID_W)
    col = jnp.tile(jnp.arange(GRID_W, dtype=F32), rows)
    ang = jnp.concatenate([r[:, None] * inv_freq, col[:, None] * inv_freq], axis=-1)
    cos, sin = jnp.cos(ang), jnp.sin(ang)
    ones = jnp.ones((n, QK_NOPE), F32)
    zeros = jnp.zeros((n, QK_NOPE), F32)
    pad = jnp.zeros((n, HEAD_PAD - QK_NOPE - QK_ROPE), F32)
    return (jnp.concatenate([ones, cos, cos, pad], axis=-1), jnp.concatenate([zeros, sin, sin, pad], axis=-1))


def _token_mixer(xs, mod, lw, khat, conv_short, *, batch, n, tiles_per_group, fixed_group, rope_tabs,
                 k_ctx=None, v_ctx=None):
    pr = _inproj(xs, mod, lw, tiles_per_group=tiles_per_group, fixed_group=fixed_group, rope_tabs=rope_tabs,
                 want_q=True, want_hg=True)
    k3 = pr["k"].reshape(batch, n, -1)
    v3 = pr["v"].reshape(batch, n // VT_BLOCK, N_HEADS * HEAD_PAD, VT_BLOCK)
    if k_ctx is None:
        o_attn = _attention(pr["q"], k3, v3, None, None, batch)
    else:
        o_attn = _attention(pr["q"], k_ctx, v_ctx, k3, v3, batch)
    if conv_short:
        z, x0 = _hyena_pre(pr["p_hy"], lw["hy_short_w"], lw["hy_short_b"], batch, None)
        conv = _direct_conv(z, *khat, n)
    else:
        z, x0, zt = _hyena_pre(pr["p_hy"], lw["hy_short_w"], lw["hy_short_b"], batch, DFT_Q)
        conv = _long_conv(zt, khat)
    x_new = _merge(xs, mod, o_attn, conv, z, x0, pr["gates"], lw, tiles_per_group=tiles_per_group,
                   fixed_group=fixed_group)
    return x_new, k3, v3


def kernel(x, c, ctx, c_ctx, w_mod, b_mod, norm1_g, norm2_g, w_in, q_norm_g, kv_norm_g, w_uq, w_uk, w_uv, hy_short_w, hy_short_b, hy_w1, hy_b1, hy_w2, hy_b2, hy_w3, hy_freq, hy_decay, hy_bias, w_br_attn, w_br_hy, w_out, ffn_w_gate, ffn_w_up, ffn_w_down, moe_router, moe_w_gate, moe_w_up, moe_w_down, final_g):
    p = dict(w_in=w_in, norm1_g=norm1_g, norm2_g=norm2_g, q_norm_g=q_norm_g, kv_norm_g=kv_norm_g, w_uq=w_uq,
             w_uk=w_uk, w_uv=w_uv, hy_short_w=hy_short_w, hy_short_b=hy_short_b, hy_w1=hy_w1, hy_b1=hy_b1,
             hy_w2=hy_w2, hy_b2=hy_b2, hy_w3=hy_w3, hy_freq=hy_freq, hy_decay=hy_decay, hy_bias=hy_bias,
             w_br_attn=w_br_attn, w_br_hy=w_br_hy, w_out=w_out)
    batch, seq, d = x.shape
    ctx_len = ctx.shape[1]
    depth = w_mod.shape[0]
    ctx_group = batch
    cond8 = jnp.zeros((SUBLANES, d), F32).at[:batch].set(c).at[ctx_group].set(c_ctx)
    rope_tabs = _rope_tables(seq)
    lat_tiles = seq // ROW_TILE
    xs = x.reshape(batch * seq, d)
    cs = ctx.reshape(batch * ctx_len, d)
    bf = lambda v: v.astype(BF16)
    for layer in range(depth):
        last = layer == depth - 1
        lw = _layer_weights(p, layer)
        mod = _adaln(cond8, w_mod, b_mod, layer)
        khat_lat = _long_conv_spectrum(seq, lw)
        if last:
            pr = _inproj(cs, mod, lw, tiles_per_group=None, fixed_group=ctx_group, rope_tabs=None, want_q=False,
                         want_hg=False)
            k_ctx = pr["k"].reshape(batch, ctx_len, -1)
            v_ctx = pr["v"].reshape(batch, ctx_len // VT_BLOCK, N_HEADS * HEAD_PAD, VT_BLOCK)
        else:
            khat_ctx = _hyena_filter(ctx_len, ctx_len, lw)
            cs_mid, k_ctx, v_ctx = _token_mixer(cs, mod, lw, khat_ctx, True, batch=batch, n=ctx_len,
                                                tiles_per_group=None, fixed_group=ctx_group, rope_tabs=None)
        xs, _, _ = _token_mixer(xs, mod, lw, khat_lat, False, batch=batch, n=seq, tiles_per_group=lat_tiles,
                                fixed_group=None, rope_tabs=rope_tabs, k_ctx=k_ctx, v_ctx=v_ctx)
        i = layer // 2
        n2 = lw["norm2_g"]
        if layer % 2 == 0:
            assert not last
            wg, wu, wd = bf(ffn_w_gate[i])[None], bf(ffn_w_up[i])[None], bf(ffn_w_down[i])[None]
            xs = _ffn(xs, mod, n2, wg, wu, wd, tiles_per_group=lat_tiles, fixed_group=None)
            cs = _ffn(cs_mid, mod, n2, wg, wu, wd, tiles_per_group=None, fixed_group=ctx_group)
        else:
            assert last
            router = jnp.pad(moe_router[i], ((0, 0), (0, LANES - N_EXPERTS)))
            xs = _moe(xs, mod, n2, router, bf(moe_w_gate[i]), bf(moe_w_up[i]), bf(moe_w_down[i]),
                      final_g.reshape(1, d), lat_tiles)
    return xs.reshape(batch, seq, d)
```

```python
import functools
import math

import numpy as np
import jax
import jax.numpy as jnp
from jax import lax
from jax.experimental import pallas as pl
from jax.experimental.pallas import tpu as pltpu

F32 = jnp.float32
BF16 = jnp.bfloat16
HIGHEST = lax.Precision.HIGHEST

D_MODEL = 1024
GRID_W = 64
EPS = 1e-6
N_HEADS = 8
Q_LORA = 384
KV_LORA = 256
QK_NOPE = 64
QK_ROPE = 32
V_HEAD = 64
ROPE_BASE = 10000.0
ATTN_SCALE = (QK_NOPE + QK_ROPE) ** -0.5
HY_WIDTH = 512
HY_EMB = 17
HY_HIDDEN = 64
D_FF = 2816
N_EXPERTS = 8
KV_START = Q_LORA
HY_START = KV_START + KV_LORA + QK_ROPE
GATE_START = HY_START + 3 * HY_WIDTH

LANES = 128
SUBLANES = 8
HALO_ROWS = 16
HEAD_PAD = LANES
VMEM_LIMIT = 56 * 2**20

ROW_TILE = 256
INPROJ_ROW_TILE = 512
STREAM_ROW_TILE = 512
FFN_ROW_TILE = 512
MXU_TILE = 256
FFN_SPANS = ((0, 4 * MXU_TILE), (4 * MXU_TILE, 8 * MXU_TILE), (8 * MXU_TILE, D_FF))
MOE_ROW_TILE = 256
VT_BLOCK = ROW_TILE
ATTN_KV = 256
ATTN_Q_TILE = 1024
Q_SCALE = ATTN_SCALE * math.log2(math.e)
DFT_P = 64
DFT_Q = 128
COLSLOT_BQ = 16
DFT_CB = 8
FILTER_CHUNKS = 4


def _cparams(*sem):
    return pltpu.CompilerParams(dimension_semantics=sem, vmem_limit_bytes=VMEM_LIMIT)


def _dot(a, b):
    return jnp.dot(a, b, preferred_element_type=F32)


def _dot_hi(a, b):
    return jnp.dot(a, b, precision=HIGHEST, preferred_element_type=F32)


def _dot_split(a, b):
    a_hi, b_hi = a.astype(BF16), b.astype(BF16)
    a_lo = (a - a_hi.astype(F32)).astype(BF16)
    b_lo = (b - b_hi.astype(F32)).astype(BF16)
    return _dot(a_hi, b_hi) + (_dot(a_hi, b_lo) + _dot(a_lo, b_hi))


def _dot_dft(table, x):
    return _dot(table, x.astype(BF16))


def _rms(xf, g):
    return xf * lax.rsqrt(jnp.mean(xf * xf, axis=-1, keepdims=True) + EPS) * g


def _full(shape):
    nd = len(shape)
    return pl.BlockSpec(shape, lambda *_: (0,) * nd, pipeline_mode=pl.Buffered(1))


def _adaln_body(c_ref, w_ref, b_ref, o_ref):
    c = c_ref[...]
    o_ref[...] = _dot_hi(c * jax.nn.sigmoid(c), w_ref[0]) + b_ref[0]


def _adaln(cond8, w, b, layer):
    n_layers, d, n = w.shape
    return pl.pallas_call(
        _adaln_body,
        grid=(n // d,),
        in_specs=[_full((SUBLANES, d)), pl.BlockSpec((1, d, d), lambda j: (layer, 0, j)),
                  pl.BlockSpec((1, 1, d), lambda j: (layer, 0, j))],
        out_specs=pl.BlockSpec((SUBLANES, d), lambda j: (0, j)),
        out_shape=jax.ShapeDtypeStruct((SUBLANES, n), F32),
        compiler_params=_cparams("arbitrary"),
        name="adaln",
    )(cond8, w, b.reshape(n_layers, 1, n))


def _mod_spec(chunk):
    return pl.BlockSpec((SUBLANES, D_MODEL), lambda t, *_: (0, chunk))


def _scale_tiles(tiles_per_group, tm):
    return None if tiles_per_group is None else tiles_per_group * ROW_TILE // tm


def _mod_row(ref, group):
    return ref[pl.ds(group, 1), :]


def _inproj_body(tiles_per_group, fixed_group, use_rope, want_q, want_hg, *refs):
    it = iter(refs)
    x_ref, sh_ref, sc_ref, g1_ref = next(it), next(it), next(it), next(it)
    wkv_ref, kvg_ref, wuk_ref, wuv_ref, wkr_ref = next(it), next(it), next(it), next(it), next(it)
    if want_q:
        wq_ref, qg_ref, wuqa_ref, wuqb_ref = next(it), next(it), next(it), next(it)
    if want_hg:
        why_ref, wgate_ref = next(it), next(it)
    if use_rope:
        cos_ref, sin_ref = next(it), next(it)
    k_out, v_out = next(it), next(it)
    if want_q:
        q_out = next(it)
    if want_hg:
        phy_out, gate_out = next(it), next(it)

    group = fixed_group if fixed_group is not None else pl.program_id(0) // tiles_per_group
    xf = x_ref[...]
    h = _rms(xf, g1_ref[...]) * (1.0 + _mod_row(sc_ref, group)) + _mod_row(sh_ref, group)
    h = h.astype(BF16)
    if use_rope:
        cos, sin = cos_ref[...], sin_ref[...]

    def rope(a, b):
        return a * cos + b * sin if use_rope else a

    ckv = _rms(_dot(h, wkv_ref[...]), kvg_ref[...]).astype(BF16)
    vt = lax.dot_general(wuv_ref[...], ckv, (((1,), (1,)), ((), ())), preferred_element_type=F32)
    vrow = lax.broadcasted_iota(jnp.int32, vt.shape, 0) & (HEAD_PAD - 1)
    vt = jnp.where(vrow == V_HEAD, 1.0, vt).astype(BF16)
    for j in range(v_out.shape[0]):
        v_out[j] = vt[:, j * VT_BLOCK:(j + 1) * VT_BLOCK]
    k_nope = _dot(ckv, wuk_ref[...])
    kr = _dot(h, wkr_ref[...])
    k_rope = rope(kr[:, :HEAD_PAD], kr[:, HEAD_PAD:])
    for hd in range(N_HEADS):
        sl = slice(hd * HEAD_PAD, (hd + 1) * HEAD_PAD)
        k_out[:, sl] = (k_nope[:, sl] + k_rope).astype(BF16)
    if want_q:
        qn = _rms(_dot(h, wq_ref[...]), qg_ref[...]).astype(BF16)
        qa = _dot(qn, wuqa_ref[...])
        qb = _dot(qn, wuqb_ref[...]) if use_rope else None
        for hd in range(N_HEADS):
            sl = slice(hd * HEAD_PAD, (hd + 1) * HEAD_PAD)
            q_out[:, sl] = (rope(qa[:, sl], None if qb is None else qb[:, sl]) * Q_SCALE).astype(BF16)
    if want_hg:
        n_hy = why_ref.shape[1]
        for c0 in range(0, n_hy, 512):
            phy_out[:, c0:c0 + 512] = _dot(h, why_ref[:, c0:c0 + 512]).astype(BF16)
        n_g = wgate_ref.shape[1]
        for c0 in range(0, n_g, 512):
            gate_out[:, c0:c0 + 512] = jax.nn.sigmoid(_dot(h, wgate_ref[:, c0:c0 + 512])).astype(BF16)


def _inproj(x, mod, lw, *, tiles_per_group, fixed_group, rope_tabs, want_q, want_hg):
    rows = x.shape[0]
    tm = INPROJ_ROW_TILE
    nt = rows // tm
    if tiles_per_group is not None:
        tiles_per_group = tiles_per_group * ROW_TILE // tm
    use_rope = rope_tabs is not None
    row_spec = lambda n: pl.BlockSpec((tm, n), lambda t: (t, 0))
    ins = [x, mod, mod, lw["norm1_g"], lw["w_kv"], lw["kv_norm_g"], lw["w_uk"], lw["w_uv"], lw["w_kr"]]
    specs = [row_spec(D_MODEL), _mod_spec(0), _mod_spec(1), _full((1, D_MODEL)),
             _full(lw["w_kv"].shape), _full((1, KV_LORA)), _full(lw["w_uk"].shape), _full(lw["w_uv"].shape),
             _full(lw["w_kr"].shape)]
    if want_q:
        ins += [lw["w_q"], lw["q_norm_g"], lw["w_uq_a"], lw["w_uq_b"]]
        specs += [_full(lw["w_q"].shape), _full((1, Q_LORA)), _full(lw["w_uq_a"].shape), _full(lw["w_uq_b"].shape)]
    if want_hg:
        ins += [lw["w_hy"], lw["w_gate"]]
        specs += [_full(lw["w_hy"].shape), _full(lw["w_gate"].shape)]
    if use_rope:
        seq_tiles = rope_tabs[0].shape[0] // tm
        ins += list(rope_tabs)
        specs += [pl.BlockSpec((tm, HEAD_PAD), lambda t: (t % seq_tiles, 0))] * 2
    hp = N_HEADS * HEAD_PAD
    vt_blocks = tm // VT_BLOCK
    out_shape = [jax.ShapeDtypeStruct((rows, hp), BF16), jax.ShapeDtypeStruct((rows // VT_BLOCK, hp, VT_BLOCK), BF16)]
    out_specs = [row_spec(hp), pl.BlockSpec((vt_blocks, hp, VT_BLOCK), lambda t: (t, 0, 0))]
    if want_q:
        out_shape.append(jax.ShapeDtypeStruct((rows, hp), BF16))
        out_specs.append(row_spec(hp))
    if want_hg:
        out_shape += [jax.ShapeDtypeStruct((rows, 3 * HY_WIDTH), BF16), jax.ShapeDtypeStruct((rows, 2 * D_MODEL), BF16)]
        out_specs += [row_spec(3 * HY_WIDTH), row_spec(2 * D_MODEL)]
    outs = pl.pallas_call(
        functools.partial(_inproj_body, tiles_per_group, fixed_group, use_rope, want_q, want_hg),
        grid=(nt,), in_specs=specs, out_specs=out_specs, out_shape=out_shape,
        compiler_params=_cparams("arbitrary"), name="inproj",
    )(*ins)
    res = {"k": outs[0], "v": outs[1]}
    i = 2
    if want_q:
        res["q"] = outs[i]
        i += 1
    if want_hg:
        res["p_hy"], res["gates"] = outs[i], outs[i + 1]
    return res


def _attn_body(n_lat_blocks, *refs):
    if n_lat_blocks:
        q_ref, kc_ref, vc_ref, kl_ref, vl_ref, o_ref = refs
    else:
        q_ref, kc_ref, vc_ref, o_ref = refs
    tq = q_ref.shape[0]
    q = q_ref[...]
    sub = ATTN_KV // VT_BLOCK

    def run(heads):
        nh = len(heads)

        def scores(kblk):
            return tuple(lax.dot_general(kblk[:, sl], q[:, sl], (((1,), (1,)), ((), ())),
                                         preferred_element_type=F32) for sl in heads)

        def update(st, vt_blocks, carry):
            m_new = [jnp.maximum(carry[hd][0], jnp.max(st[hd], axis=0, keepdims=True)) for hd in range(nh)]
            p = [jnp.exp2(st[hd] - m_new[hd]).astype(BF16) for hd in range(nh)]
            out = []
            for hd in range(nh):
                m, acc = carry[hd]
                pv = None
                for j, vt in enumerate(vt_blocks):
                    r = _dot(vt[heads[hd], :], p[hd][j * VT_BLOCK:(j + 1) * VT_BLOCK])
                    pv = r if pv is None else pv + r
                out.append((m_new[hd], jnp.exp2(m - m_new[hd]) * acc + pv))
            return tuple(out)

        init = tuple((jnp.full((1, tq), -jnp.inf, F32), jnp.zeros((HEAD_PAD, tq), F32)) for _ in range(nh))
        carry = update(scores(kc_ref[0]), [vc_ref[0, 0]], init)
        if n_lat_blocks:
            def k_block(i):
                return kl_ref[0, pl.ds(i * ATTN_KV, ATTN_KV), :]

            def v_blocks(i):
                return [vl_ref[0, i * sub + j] for j in range(sub)]

            st = scores(k_block(0))
            for i in range(n_lat_blocks - 1):
                st_next = scores(k_block(i + 1))
                carry = update(st, v_blocks(i), carry)
                st = st_next
            carry = update(st, v_blocks(n_lat_blocks - 1), carry)
        return [acc[:V_HEAD] / acc[V_HEAD:V_HEAD + 1] for _, acc in carry]

    outs = run([slice(hd * HEAD_PAD, (hd + 1) * HEAD_PAD) for hd in range(2)])
    o_ref[...] = jnp.concatenate(outs, axis=0).T.astype(BF16)


def _attention(q, k_ctx, vt_ctx, k_lat, vt_lat, batch):
    rows = q.shape[0]
    tq = min(ATTN_Q_TILE, rows // batch)
    qt_per_b = rows // batch // tq
    ctx_len = k_ctx.shape[1]
    assert ctx_len == VT_BLOCK
    pair_w = 2 * HEAD_PAD
    ins = [q, k_ctx, vt_ctx]
    specs = [pl.BlockSpec((tq, pair_w), lambda b, hp, t: (b * qt_per_b + t, hp)),
             pl.BlockSpec((1, ctx_len, pair_w), lambda b, hp, t: (b, 0, hp)),
             pl.BlockSpec((1, 1, pair_w, VT_BLOCK), lambda b, hp, t: (b, 0, hp, 0))]
    n_lat_blocks = 0
    if k_lat is not None:
        lat_len = k_lat.shape[1]
        n_lat_blocks = lat_len // ATTN_KV
        ins += [k_lat, vt_lat]
        specs += [pl.BlockSpec((1, lat_len, pair_w), lambda b, hp, t: (b, 0, hp)),
                  pl.BlockSpec((1, lat_len // VT_BLOCK, pair_w, VT_BLOCK), lambda b, hp, t: (b, 0, hp, 0))]
    return pl.pallas_call(
        functools.partial(_attn_body, n_lat_blocks),
        grid=(batch, N_HEADS // 2, qt_per_b), in_specs=specs,
        out_specs=pl.BlockSpec((tq, 2 * V_HEAD), lambda b, hp, t: (b * qt_per_b + t, hp)),
        out_shape=jax.ShapeDtypeStruct((rows, N_HEADS * V_HEAD), BF16),
        compiler_params=_cparams("arbitrary", "arbitrary", "arbitrary"), name="attention",
    )(*ins)


def _hypre_body(seq_tiles, n_chunks, p_ref, prev_ref, next_ref, w_ref, b_ref, z_ref, x0_ref, *zt_ref):
    j = pl.program_id(0) % seq_tiles
    p = p_ref[...].astype(F32)
    tm = p.shape[0]
    row = lax.broadcasted_iota(jnp.int32, (tm, 1), 0)
    prev_row = jnp.where(j != 0, prev_ref[...].astype(F32)[HALO_ROWS - 1:HALO_ROWS, :], 0.0)
    next_row = jnp.where(j != seq_tiles - 1, next_ref[...].astype(F32)[0:1, :], 0.0)
    up = jnp.where(row == 0, prev_row, pltpu.roll(p, 1, 0))
    dn = jnp.where(row == tm - 1, next_row, pltpu.roll(p, tm - 1, 0))
    u = up * w_ref[0:1, :] + p * w_ref[1:2, :] + dn * w_ref[2:3, :] + b_ref[...]
    c = HY_WIDTH
    z = u[:, :c] * u[:, c:2 * c]
    z_ref[...] = z.astype(BF16)
    x0_ref[...] = u[:, 2 * c:].astype(BF16)
    if zt_ref:
        q = tm // n_chunks
        for a in range(n_chunks):
            zt_ref[0][0, :, a * c:(a + 1) * c] = z[a * q:(a + 1) * q, :].astype(BF16)


def _hyena_pre(p_hy, short_w, short_b, batch, dft_q):
    rows = p_hy.shape[0]
    n = rows // batch
    tm = min(STREAM_ROW_TILE, n)
    seq_tiles = n // tm
    nt = rows // tm
    c3 = 3 * HY_WIDTH
    halo = tm // HALO_ROWS
    last_halo = rows // HALO_ROWS - 1
    specs = [pl.BlockSpec((tm, c3), lambda t: (t, 0)),
             pl.BlockSpec((HALO_ROWS, c3), lambda t: (jnp.maximum(t * halo - 1, 0), 0)),
             pl.BlockSpec((HALO_ROWS, c3), lambda t: (jnp.minimum((t + 1) * halo, last_halo), 0)),
             _full((3, c3)), _full((1, c3))]
    out_shape = [jax.ShapeDtypeStruct((rows, HY_WIDTH), BF16)] * 2
    out_specs = [pl.BlockSpec((tm, HY_WIDTH), lambda t: (t, 0))] * 2
    n_chunks = 1
    if dft_q is not None:
        n_chunks = tm // dft_q
        half_p = n // dft_q
        out_shape.append(jax.ShapeDtypeStruct((batch // 2, dft_q, 2 * half_p * HY_WIDTH), BF16))

        def zt_map(t):
            b, jt = t // seq_tiles, t % seq_tiles
            return (b // 2, 0, (b % 2) * seq_tiles + jt)
        out_specs.append(pl.BlockSpec((1, dft_q, n_chunks * HY_WIDTH), zt_map))
    return pl.pallas_call(
        functools.partial(_hypre_body, seq_tiles, n_chunks),
        grid=(nt,), in_specs=specs, out_specs=out_specs, out_shape=out_shape,
        compiler_params=_cparams("arbitrary"), name="hyena_pre",
    )(p_hy, p_hy, p_hy, short_w, short_b.reshape(1, c3))


def _filter_body(emb_ref, w1_ref, b1_ref, w2_ref, b2_ref, w3_ref, freq_ref, decay_ref, h0_ref, h1_ref, s_ref):
    a = pl.program_id(0)
    emb = emb_ref[...]
    freq = freq_ref[...]
    h = jnp.sin(freq * (_dot_hi(emb, w1_ref[...]) + b1_ref[...]))
    h = jnp.sin(freq * (_dot_hi(h, w2_ref[...]) + b2_ref[...]))
    h = _dot_hi(h, w3_ref[...]) * jnp.exp(-emb[:, 0:1] * jnp.abs(decay_ref[...]))
    c = HY_WIDTH
    row = lax.broadcasted_iota(jnp.int32, (emb.shape[0], 1), 0)
    h0 = h[:, :c]
    h1 = jnp.where(jnp.logical_and(a == 0, row == 0), 0.0, h[:, c:])
    q = h0_ref.shape[1]
    for j in range(h0_ref.shape[2] // c):
        h0_ref[0, :, j * c:(j + 1) * c] = h0[j * q:(j + 1) * q].astype(BF16)
        h1_ref[0, :, j * c:(j + 1) * c] = h1[j * q:(j + 1) * q].astype(BF16)
    part = jnp.sum(jnp.abs(h0) + jnp.abs(h1), axis=0, keepdims=True)

    @pl.when(a == 0)
    def _():
        s_ref[...] = jnp.zeros_like(s_ref)
    s_ref[...] += part


def _hyena_filter(n, q, lw):
    f32 = F32
    bands = (HY_EMB - 1) // 2
    t = jnp.linspace(0.0, 1.0, n, dtype=f32)[:, None]
    phase = (2.0 * math.pi / n) * jnp.arange(n, dtype=f32)[:, None] * jnp.linspace(1e-4, bands - 1, bands, dtype=f32)
    emb = jnp.concatenate([t, jnp.cos(phase), -jnp.sin(phase), jnp.zeros((n, 32 - HY_EMB), f32)], axis=-1)
    w1 = jnp.concatenate([lw["hy_w1"], jnp.zeros((32 - HY_EMB, HY_HIDDEN), f32)], axis=0)
    c = HY_WIDTH
    slots = n // q
    per_step = min(FILTER_CHUNKS, slots)
    row = lambda v: v.reshape(1, -1)
    return pl.pallas_call(
        _filter_body,
        grid=(slots // per_step,),
        in_specs=[pl.BlockSpec((per_step * q, 32), lambda a: (a, 0)), _full((32, HY_HIDDEN)), _full((1, HY_HIDDEN)),
                  _full((HY_HIDDEN, HY_HIDDEN)), _full((1, HY_HIDDEN)), _full((HY_HIDDEN, 2 * c)),
                  _full((1, HY_HIDDEN)), _full((1, 2 * c))],
        out_specs=[pl.BlockSpec((1, q, per_step * c), lambda a: (0, 0, a)),
                   pl.BlockSpec((1, q, per_step * c), lambda a: (0, 0, a)),
                   pl.BlockSpec((1, c), lambda a: (0, 0))],
        out_shape=[jax.ShapeDtypeStruct((1, q, slots * c), BF16)] * 2 + [jax.ShapeDtypeStruct((1, c), f32)],
        compiler_params=_cparams("arbitrary"), name="hyena_filter",
    )(emb, w1, row(lw["hy_b1"]), lw["hy_w2"], row(lw["hy_b2"]), lw["hy_w3"], row(lw["hy_freq"]),
      row(lw["hy_decay"]))


def _colslot_body(nparts, bq, c, *refs):
    x_refs, t_refs, o_ref = refs[:nparts], refs[nparts:2 * nparts], refs[-1]
    for j in range(bq):
        acc = None
        for x_ref, t_ref in zip(x_refs, t_refs):
            r = _dot_dft(t_ref[j], x_ref[0, j])
            acc = r if acc is None else acc + r
        o_ref[0, :, j * c:(j + 1) * c] = acc.astype(o_ref.dtype)


def _colslot_matmul(xs, tabs, out_dtype, c=HY_WIDTH):
    tabs = [jnp.asarray(t, BF16) for t in tabs]
    g, q = xs[0].shape[0], xs[0].shape[1]
    m = tabs[0].shape[1]
    bq = COLSLOT_BQ
    xs4 = [x.reshape(g, q, -1, c) for x in xs]
    specs = [pl.BlockSpec((1, bq, x.shape[2], c), lambda gi, qi: (gi, qi, 0, 0)) for x in xs4]
    specs += [pl.BlockSpec((bq, m, t.shape[2]), lambda gi, qi: (qi, 0, 0)) for t in tabs]
    return pl.pallas_call(
        functools.partial(_colslot_body, len(xs), bq, c),
        grid=(g, q // bq), in_specs=specs,
        out_specs=pl.BlockSpec((1, m, bq * c), lambda gi, qi: (gi, 0, qi)),
        out_shape=jax.ShapeDtypeStruct((g, m, q * c), out_dtype),
        compiler_params=_cparams("arbitrary", "arbitrary"), name="dft_stride_stage",
    )(*xs4, *tabs)


def _spec_mid_body(g_ref, m3_ref, m3i_ref, k_ref, hr_ref, hi_ref):
    bp, q = hr_ref.shape[0], hr_ref.shape[1]
    c = hr_ref.shape[2] // DFT_CB
    for j in range(DFT_CB):
        kh = k_ref[j]
        kr, ki = kh[:q], kh[q:]
        for b in range(bp):
            x = _dot_dft(m3_ref[...], g_ref[b, :, j].reshape(2 * q, -1))
            xr, xi = x[:q], x[q:]
            y = jnp.concatenate([xr * kr - xi * ki, xr * ki + xi * kr], axis=0)
            h = _dot_dft(m3i_ref[...], y)
            hr_ref[b, :, j * c:(j + 1) * c] = h[:q].astype(BF16)
            hi_ref[b, :, j * c:(j + 1) * c] = h[q:].astype(BF16)


def _spec_mid(gf, m3, m3i, khat, p, q, c=HY_WIDTH):
    bp = gf.shape[0]
    g5 = gf.reshape(bp, 2, p, q, c)
    cb = DFT_CB
    return pl.pallas_call(
        _spec_mid_body,
        grid=(p // cb,),
        in_specs=[pl.BlockSpec((bp, 2, cb, q, c), lambda ci: (0, 0, ci, 0, 0)), _full((2 * q, 2 * q)),
                  _full((2 * q, 2 * q)), pl.BlockSpec((cb, 2 * q, c), lambda ci: (ci, 0, 0))],
        out_specs=[pl.BlockSpec((bp, q, cb * c), lambda ci: (0, 0, ci))] * 2,
        out_shape=[jax.ShapeDtypeStruct((bp, q, p * c), BF16)] * 2,
        compiler_params=_cparams("arbitrary"), name="dft_mid",
    )(g5, m3, m3i, khat)


def _filter_spec_body(g_ref, m3k_ref, s_ref, k_ref):
    q4 = m3k_ref.shape[1]
    inv = 1.0 / s_ref[...]
    for j in range(DFT_CB):
        k_ref[j] = _dot_dft(m3k_ref[...], g_ref[0, :, j].reshape(q4, -1)) * inv


def _filter_spec(gk, m3k, s, p, q, c=HY_WIDTH):
    g5 = gk.reshape(1, 4, p, q, c)
    cb = DFT_CB
    return pl.pallas_call(
        _filter_spec_body,
        grid=(p // cb,),
        in_specs=[pl.BlockSpec((1, 4, cb, q, c), lambda ci: (0, 0, ci, 0, 0)), _full((2 * q, 4 * q)), _full((1, c))],
        out_specs=pl.BlockSpec((cb, 2 * q, c), lambda ci: (ci, 0, 0)),
        out_shape=jax.ShapeDtypeStruct((p, 2 * q, c), F32),
        compiler_params=_cparams("arbitrary"), name="filter_spectrum",
    )(g5, m3k, s)


def _direct_conv_body(z_ref, h0_ref, h1_ref, s_ref, tf_ref, tk_ref, ti_ref, o_ref):
    n2 = tf_ref.shape[0] // 2
    kh = _dot_dft(tk_ref[...], jnp.concatenate([h0_ref[0], h1_ref[0]], axis=0)) / s_ref[...]
    x = _dot_dft(tf_ref[...], z_ref[0])
    xr, xi, kr, ki = x[:n2], x[n2:], kh[:n2], kh[n2:]
    y = jnp.concatenate([xr * kr - xi * ki, xr * ki + xi * kr], axis=0)
    o_ref[0] = _dot_dft(ti_ref[...], y)


def _direct_conv(z, h0, h1, s, n, c=HY_WIDTH):
    bp = z.shape[0] // (2 * n)
    tf, tk, ti = (jnp.asarray(t, BF16) for t in _direct_tables(n))
    out = pl.pallas_call(
        _direct_conv_body,
        grid=(bp,),
        in_specs=[pl.BlockSpec((1, 2 * n, c), lambda b: (b, 0, 0)), _full((1, n, c)), _full((1, n, c)), _full((1, c)),
                  _full(tf.shape), _full(tk.shape), _full(ti.shape)],
        out_specs=pl.BlockSpec((1, 2 * n, c), lambda b: (b, 0, 0)),
        out_shape=jax.ShapeDtypeStruct((bp, 2 * n, c), F32),
        compiler_params=_cparams("arbitrary"), name="direct_conv",
    )(z.reshape(bp, 2 * n, c), h0, h1, s, tf, tk, ti)
    return out.reshape(bp * 2 * n, c)


def _cis(m, n_total):
    ang = (2.0 * np.pi / n_total) * (m % n_total).astype(np.float64)
    return np.cos(ang), -np.sin(ang)


@functools.lru_cache(maxsize=None)
def _direct_tables(n):
    nn = 2 * n
    f = np.arange(nn)[:, None]
    t = np.arange(n)[None, :]
    cr, ci = _cis(f * t, nn)
    tf = np.block([[cr, -ci], [ci, cr]])
    tk = np.block([[cr, cr], [ci, -ci]])
    ti = np.block([[cr.T, ci.T], [-ci.T, cr.T]]) / nn
    return tuple(a.astype(np.float32) for a in (tf, tk, ti))


@functools.lru_cache(maxsize=None)
def _twostage_tables(p, q):
    nn = p * q
    hp = p // 2
    s = np.arange(q)[:, None, None]
    c = np.arange(p)[None, :, None]
    a = np.arange(hp)[None, None, :]
    tr, ti = _cis(c * (q * a + s), nn)
    z = np.zeros_like(tr)
    t_data = np.concatenate([np.concatenate([tr, -ti], 2), np.concatenate([ti, tr], 2)], 1)
    t_k0 = np.concatenate([tr, ti, z, z], 1)
    t_k1 = np.concatenate([z, z, tr, -ti], 1)
    trt, tit = np.swapaxes(tr, 1, 2) / nn, np.swapaxes(ti, 1, 2) / nn
    t_inv_r = np.concatenate([trt, -tit], 1)
    t_inv_i = np.concatenate([tit, trt], 1)
    d = np.arange(q)[:, None]
    b = np.arange(q)[None, :]
    fr, fi = _cis(d * b, q)
    m3 = np.block([[fr, -fi], [fi, fr]])
    m3i = np.block([[fr, fi], [-fi, fr]])
    m3k = np.block([[fr, -fi, fr, fi], [fi, fr, -fi, fr]])
    return tuple(x.astype(np.float32) for x in (t_data, t_k0, t_k1, t_inv_r, t_inv_i, m3, m3i, m3k))


def _long_conv_spectrum(n, lw):
    p, q = DFT_P, DFT_Q
    h0, h1, s = _hyena_filter(n, q, lw)
    _, t_k0, t_k1, _, _, _, _, m3k = _twostage_tables(p, q)
    gk = _colslot_matmul([h0, h1], [t_k0, t_k1], BF16)
    return _filter_spec(gk, jnp.asarray(m3k, BF16), s, p, q)


def _long_conv(zt, khat):
    p, q = DFT_P, DFT_Q
    t_data, _, _, t_inv_r, t_inv_i, m3, m3i, _ = _twostage_tables(p, q)
    gf = _colslot_matmul([zt], [t_data], BF16)
    hr, hi = _spec_mid(gf, jnp.asarray(m3, BF16), jnp.asarray(m3i, BF16), khat, p, q)
    y = _colslot_matmul([hr, hi], [t_inv_r, t_inv_i], BF16)
    return y.reshape(-1, HY_WIDTH)


def _merge_body(tiles_per_group, fixed_group, x_ref, g1_ref, oa_ref, conv_ref, z_ref, x0_ref, gate_ref, bias_ref,
                wba_ref, wbh_ref, wout_ref, o_ref):
    group = fixed_group if fixed_group is not None else pl.program_id(0) // tiles_per_group
    o_hy = ((conv_ref[...] + z_ref[...] * bias_ref[...]) * x0_ref[...]).astype(BF16)
    d = D_MODEL
    merged = gate_ref[:, :d] * _dot(oa_ref[...], wba_ref[...]) + gate_ref[:, d:] * _dot(o_hy, wbh_ref[...])
    mix = _dot(merged.astype(BF16), wout_ref[...])
    o_ref[...] = x_ref[...] + _mod_row(g1_ref, group) * mix


def _merge(x, mod, o_attn, conv, z, x0, gates, lw, *, tiles_per_group, fixed_group):
    rows = x.shape[0]
    tm = STREAM_ROW_TILE
    row_spec = lambda n: pl.BlockSpec((tm, n), lambda t: (t, 0))
    c = HY_WIDTH
    return pl.pallas_call(
        functools.partial(_merge_body, _scale_tiles(tiles_per_group, tm), fixed_group),
        grid=(rows // tm,),
        in_specs=[row_spec(D_MODEL), _mod_spec(2), row_spec(N_HEADS * V_HEAD), row_spec(c), row_spec(c), row_spec(c),
                  row_spec(2 * D_MODEL), _full((1, c)), _full(lw["w_br_attn"].shape), _full(lw["w_br_hy"].shape),
                  _full(lw["w_out"].shape)],
        out_specs=row_spec(D_MODEL),
        out_shape=jax.ShapeDtypeStruct((rows, D_MODEL), F32),
        compiler_params=_cparams("arbitrary"), name="merge_out",
    )(x, mod, o_attn, conv, z, x0, gates, lw["hy_bias"], lw["w_br_attn"], lw["w_br_hy"], lw["w_out"])


def _swiglu(h, wg_ref, wu_ref, wd_ref):
    acc = None
    for c0, c1 in FFN_SPANS:
        gate = _dot(h, wg_ref[0, :, c0:c1])
        up = _dot(h, wu_ref[0, :, c0:c1])
        y = _dot((gate * jax.nn.sigmoid(gate) * up).astype(BF16), wd_ref[0, c0:c1, :])
        acc = y if acc is None else acc + y
    return acc


def _ffn_body(tiles_per_group, fixed_group, x_ref, sh_ref, sc_ref, g2_ref, n2_ref, wg_ref, wu_ref, wd_ref, o_ref):
    group = fixed_group if fixed_group is not None else pl.program_id(0) // tiles_per_group
    x = x_ref[...]
    h = (_rms(x, n2_ref[...]) * (1.0 + _mod_row(sc_ref, group)) + _mod_row(sh_ref, group)).astype(BF16)
    o_ref[...] = x + _mod_row(g2_ref, group) * _swiglu(h, wg_ref, wu_ref, wd_ref)


def _ffn(x, mod, norm_g, w_gate, w_up, w_down, *, tiles_per_group, fixed_group):
    rows = x.shape[0]
    tm = min(FFN_ROW_TILE, rows)
    row_spec = pl.BlockSpec((tm, D_MODEL), lambda t: (t, 0))
    return pl.pallas_call(
        functools.partial(_ffn_body, tiles_per_group * ROW_TILE // tm if tiles_per_group else None, fixed_group),
        grid=(rows // tm,),
        in_specs=[row_spec, _mod_spec(3), _mod_spec(4), _mod_spec(5), _full((1, D_MODEL)),
                  _full(w_gate.shape), _full(w_up.shape), _full(w_down.shape)],
        out_specs=row_spec,
        out_shape=jax.ShapeDtypeStruct((rows, D_MODEL), F32),
        compiler_params=_cparams("arbitrary"), name="ffn",
    )(x, mod, mod, mod, norm_g, w_gate, w_up, w_down)


def _route_body(tiles_per_group, x_ref, sh_ref, sc_ref, n2_ref, router_ref, h_ref, route_ref):
    group = pl.program_id(0) // tiles_per_group
    h = _rms(x_ref[...], n2_ref[...]) * (1.0 + _mod_row(sc_ref, group)) + _mod_row(sh_ref, group)
    h_ref[...] = h
    logits = _dot_split(h, router_ref[...])
    lane = lax.broadcasted_iota(jnp.int32, logits.shape, 1)
    logits = jnp.where(lane < N_EXPERTS, logits, -jnp.inf)
    m1 = jnp.max(logits, axis=-1, keepdims=True)
    i1 = jnp.min(jnp.where(logits == m1, lane, LANES), axis=-1, keepdims=True)
    rest = jnp.where(lane == i1, -jnp.inf, logits)
    m2 = jnp.max(rest, axis=-1, keepdims=True)
    i2 = jnp.min(jnp.where(rest == m2, lane, LANES), axis=-1, keepdims=True)
    e2 = jnp.exp(m2 - m1)
    w1 = 1.0 / (1.0 + e2)
    route_ref[...] = (jnp.where(lane == 0, i1.astype(F32), 0.0) + jnp.where(lane == 1, i2.astype(F32), 0.0)
                      + jnp.where(lane == 2, w1, 0.0) + jnp.where(lane == 3, e2 * w1, 0.0))


def _route(x, mod, norm_g, router, tiles_per_group):
    rows = x.shape[0]
    tm = STREAM_ROW_TILE
    row_spec = lambda n: pl.BlockSpec((tm, n), lambda t: (t, 0))
    return pl.pallas_call(
        functools.partial(_route_body, _scale_tiles(tiles_per_group, tm)),
        grid=(rows // tm,),
        in_specs=[row_spec(D_MODEL), _mod_spec(3), _mod_spec(4), _full((1, D_MODEL)), _full(router.shape)],
        out_specs=[row_spec(D_MODEL), row_spec(LANES)],
        out_shape=[jax.ShapeDtypeStruct((rows, D_MODEL), F32), jax.ShapeDtypeStruct((rows, LANES), F32)],
        compiler_params=_cparams("arbitrary"), name="moe_route",
    )(x, mod, mod, norm_g, router)


def _gather_rows(idx_ref, n_rows, src_hbm, dst, sem):
    def issue(i, carry):
        for prio in range(2):
            r = 2 * i + prio
            pltpu.make_async_copy(src_hbm.at[pl.ds(idx_ref[0, 0, r], 1), :], dst.at[pl.ds(r, 1), :],
                                  sem).start(priority=prio)
        return carry
    lax.fori_loop(0, n_rows // 2, issue, 0, unroll=8)


def _wait_rows(n_rows, src_hbm, dst, sem):
    pltpu.make_async_copy(src_hbm.at[pl.ds(0, n_rows), :], dst, sem).wait()


def _moe_dispatch_body(idx_ref, h_ref, xs_in_hbm, xs_hbm, stage, sem):
    del xs_in_hbm
    t = pl.program_id(0)
    n_tiles = pl.num_programs(0)
    tm = h_ref.shape[0]
    slot = t % 2

    def wait(s):
        for _ in range(2):
            pltpu.make_async_copy(stage.at[s], xs_hbm.at[pl.ds(0, tm), :], sem.at[s]).wait()

    @pl.when(t >= 2)
    def _():
        wait(slot)

    stage[slot] = h_ref[...]

    def issue(r, carry):
        row = stage.at[slot, pl.ds(r, 1), :]
        pltpu.make_async_copy(row, xs_hbm.at[pl.ds(idx_ref[0, 0, r], 1), :], sem.at[slot]).start(priority=0)
        pltpu.make_async_copy(row, xs_hbm.at[pl.ds(idx_ref[0, 0, tm + r], 1), :], sem.at[slot]).start(priority=1)
        return carry
    lax.fori_loop(0, tm, issue, 0, unroll=8)

    @pl.when(t == n_tiles - 1)
    def _():
        wait(slot)
        wait(1 - slot)


def _tile_index_blocks(dest, tm):
    n_tiles = dest.shape[0] // tm
    return dest.reshape(n_tiles, tm, 2).transpose(0, 2, 1).reshape(n_tiles, 1, 2 * tm)


def _moe_dispatch(h, dest, n_rows):
    tm = STREAM_ROW_TILE
    idx = _tile_index_blocks(dest, tm)
    n_tiles = idx.shape[0]
    assert n_tiles >= 2
    return pl.pallas_call(
        _moe_dispatch_body,
        grid=(n_tiles,),
        in_specs=[pl.BlockSpec((1, 1, 2 * tm), lambda t: (t, 0, 0), memory_space=pltpu.SMEM),
                  pl.BlockSpec((tm, D_MODEL), lambda t: (t, 0)), pl.BlockSpec(memory_space=pl.ANY)],
        out_specs=pl.BlockSpec(memory_space=pl.ANY),
        out_shape=jax.ShapeDtypeStruct((n_rows, D_MODEL), F32),
        scratch_shapes=[pltpu.VMEM((2, tm, D_MODEL), F32), pltpu.SemaphoreType.DMA((2,))],
        input_output_aliases={2: 0},
        compiler_params=_cparams("arbitrary"), name="moe_dispatch",
    )(idx, h, jnp.zeros((n_rows, D_MODEL), F32))


def _moe_group_body(te_ref, nu_ref, x_ref, wg_ref, wu_ref, wd_ref, y_ref):
    t = pl.program_id(0)

    @pl.when(t < nu_ref[0])
    def _():
        y_ref[...] = _swiglu(x_ref[...].astype(BF16), wg_ref, wu_ref, wd_ref)

    @pl.when(t >= nu_ref[0])
    def _():
        y_ref[...] = jnp.zeros_like(y_ref)


def _moe_group(xs, tile_expert, n_used, w_gate, w_up, w_down):
    tm = MOE_ROW_TILE
    n_rows = xs.shape[0]
    w_spec = lambda w: pl.BlockSpec((1,) + w.shape[1:], lambda t, te, nu: (te[t], 0, 0))
    grid_spec = pltpu.PrefetchScalarGridSpec(
        num_scalar_prefetch=2,
        grid=(n_rows // tm,),
        in_specs=[pl.BlockSpec((tm, D_MODEL), lambda t, te, nu: (jnp.maximum(jnp.minimum(t, nu[0] - 1), 0), 0)),
                  w_spec(w_gate), w_spec(w_up), w_spec(w_down)],
        out_specs=pl.BlockSpec((tm, D_MODEL), lambda t, te, nu: (t, 0)),
    )
    return pl.pallas_call(
        _moe_group_body, grid_spec=grid_spec,
        out_shape=jax.ShapeDtypeStruct((n_rows, D_MODEL), F32),
        compiler_params=_cparams("arbitrary"), name="moe_group",
    )(tile_expert, n_used, xs, w_gate, w_up, w_down)


def _moe_combine_body(tiles_per_group, idx_ref, idx_next_ref, x_ref, g2_ref, fg_ref, route_ref, y_hbm, o_ref,
                      yg_scr, sem):
    t = pl.program_id(0)
    n_tiles = pl.num_programs(0)
    group = t // tiles_per_group
    tm = x_ref.shape[0]
    slot = t % 2

    @pl.when(t == 0)
    def _():
        _gather_rows(idx_ref, 2 * tm, y_hbm, yg_scr.at[0], sem.at[0])

    _wait_rows(2 * tm, y_hbm, yg_scr.at[slot], sem.at[slot])

    @pl.when(t + 1 < n_tiles)
    def _():
        _gather_rows(idx_next_ref, 2 * tm, y_hbm, yg_scr.at[1 - slot], sem.at[1 - slot])

    route = route_ref[...]
    y = route[:, 2:3] * yg_scr[slot, :tm] + route[:, 3:4] * yg_scr[slot, tm:]
    o_ref[...] = _rms(x_ref[...] + _mod_row(g2_ref, group) * y, fg_ref[...])


def _moe_combine(x, mod, final_g, route, y_sorted, dest, tiles_per_group):
    rows = x.shape[0]
    tm = STREAM_ROW_TILE
    idx = _tile_index_blocks(dest, tm)
    n_tiles = idx.shape[0]
    return pl.pallas_call(
        functools.partial(_moe_combine_body, _scale_tiles(tiles_per_group, tm)),
        grid=(n_tiles,),
        in_specs=[pl.BlockSpec((1, 1, 2 * tm), lambda t: (t, 0, 0), memory_space=pltpu.SMEM),
                  pl.BlockSpec((1, 1, 2 * tm), lambda t: (jnp.minimum(t + 1, n_tiles - 1), 0, 0),
                               memory_space=pltpu.SMEM),
                  pl.BlockSpec((tm, D_MODEL), lambda t: (t, 0)), _mod_spec(5), _full((1, D_MODEL)),
                  pl.BlockSpec((tm, LANES), lambda t: (t, 0)), pl.BlockSpec(memory_space=pl.ANY)],
        out_specs=pl.BlockSpec((tm, D_MODEL), lambda t: (t, 0)),
        out_shape=jax.ShapeDtypeStruct((rows, D_MODEL), F32),
        scratch_shapes=[pltpu.VMEM((2, 2 * tm, D_MODEL), F32), pltpu.SemaphoreType.DMA((2,))],
        compiler_params=_cparams("arbitrary"), name="moe_combine",
    )(idx, idx, x, mod, final_g, route, y_sorted)


def _moe_plan(route, tm):
    t = route.shape[0]
    experts = jnp.concatenate([route[:, 0], route[:, 1]]).astype(jnp.int32)
    onehot = (experts[:, None] == jnp.arange(N_EXPERTS, dtype=jnp.int32)[None, :]).astype(jnp.int32)
    csum = jnp.cumsum(onehot, axis=0)
    rank = jnp.sum(csum * onehot, axis=1) - 1
    padded = (csum[-1] + tm - 1) // tm * tm
    ends = jnp.cumsum(padded)
    dest = jnp.sum((ends - padded)[None, :] * onehot, axis=1) + rank
    n_rows = 2 * t + N_EXPERTS * tm
    tile_start = jnp.arange(n_rows // tm, dtype=jnp.int32) * tm
    tile_expert = jnp.minimum(jnp.sum((tile_start[:, None] >= ends[None, :]).astype(jnp.int32), axis=1), N_EXPERTS - 1)
    n_used = (ends[-1:] // tm).astype(jnp.int32)
    return dest.reshape(2, t).T, tile_expert, n_used, n_rows


def _moe(x, mod, norm_g, router, w_gate, w_up, w_down, final_g, tiles_per_group):
    h, route = _route(x, mod, norm_g, router, tiles_per_group)
    dest, tile_expert, n_used, n_rows = _moe_plan(route, MOE_ROW_TILE)
    xs = _moe_dispatch(h, dest, n_rows)
    y_sorted = _moe_group(xs, tile_expert, n_used, w_gate, w_up, w_down)
    return _moe_combine(x, mod, final_g, route, y_sorted, dest, tiles_per_group)


def _pad_heads(w, width):
    k = w.shape[0]
    w = w.reshape(k, N_HEADS, width)
    return jnp.pad(w, ((0, 0), (0, 0), (0, HEAD_PAD - width))).reshape(k, N_HEADS * HEAD_PAD)


def _rot_cols(w):
    half = QK_ROPE // 2
    return jnp.concatenate([-w[..., half:], w[..., :half]], axis=-1)


def _layer_weights(p, layer):
    cols = lambda a, b: p["w_in"][layer, :, a:b]
    w_uq = p["w_uq"][layer].reshape(Q_LORA, N_HEADS, QK_NOPE + QK_ROPE)
    zeros_nope = jnp.zeros((Q_LORA, N_HEADS, QK_NOPE), F32)
    w_uq_b = jnp.concatenate([zeros_nope, _rot_cols(w_uq[..., QK_NOPE:])], axis=-1)
    w_kr = cols(KV_START + KV_LORA, HY_START)
    zk = jnp.zeros((D_MODEL, QK_NOPE), F32)
    zp = jnp.zeros((D_MODEL, HEAD_PAD - QK_NOPE - QK_ROPE), F32)
    w_kr2 = jnp.concatenate([zk, w_kr, zp, zk, _rot_cols(w_kr), zp], axis=-1)
    row = lambda v: v.reshape(1, -1)
    bf = lambda v: v.astype(BF16)
    return {
        "norm1_g": row(p["norm1_g"][layer]), "norm2_g": row(p["norm2_g"][layer]),
        "w_q": bf(cols(0, Q_LORA)), "q_norm_g": row(p["q_norm_g"][layer]),
        "w_uq_a": bf(_pad_heads(w_uq.reshape(Q_LORA, -1), QK_NOPE + QK_ROPE)),
        "w_uq_b": bf(_pad_heads(w_uq_b.reshape(Q_LORA, -1), QK_NOPE + QK_ROPE)),
        "w_kv": bf(cols(KV_START, KV_START + KV_LORA)), "kv_norm_g": row(p["kv_norm_g"][layer]),
        "w_uk": bf(_pad_heads(p["w_uk"][layer], QK_NOPE)), "w_uv": bf(_pad_heads(p["w_uv"][layer], V_HEAD).T),
        "w_kr": bf(w_kr2),
        "w_hy": bf(cols(HY_START, GATE_START)), "w_gate": bf(cols(GATE_START, GATE_START + 2 * D_MODEL)),
        "hy_short_w": p["hy_short_w"][layer], "hy_short_b": p["hy_short_b"][layer],
        "hy_w1": p["hy_w1"][layer], "hy_b1": p["hy_b1"][layer], "hy_w2": p["hy_w2"][layer],
        "hy_b2": p["hy_b2"][layer], "hy_w3": p["hy_w3"][layer], "hy_freq": p["hy_freq"][layer],
        "hy_decay": p["hy_decay"][layer], "hy_bias": row(p["hy_bias"][layer]),
        "w_br_attn": bf(p["w_br_attn"][layer]), "w_br_hy": bf(p["w_br_hy"][layer]), "w_out": bf(p["w_out"][layer]),
    }


def _rope_tables(n):
    rows = n // GRID_W
    n_freq = QK_ROPE // 4
    inv_freq = ROPE_BASE ** (-jnp.arange(n_freq, dtype=F32) / n_freq)
    r = jnp.repeat(jnp.arange(rows, dtype=F32), GRID_W)
    col = jnp.tile(jnp.arange(GRID_W, dtype=F32), rows)
    ang = jnp.concatenate([r[:, None] * inv_freq, col[:, None] * inv_freq], axis=-1)
    cos, sin = jnp.cos(ang), jnp.sin(ang)
    ones = jnp.ones((n, QK_NOPE), F32)
    zeros = jnp.zeros((n, QK_NOPE), F32)
    pad = jnp.zeros((n, HEAD_PAD - QK_NOPE - QK_ROPE), F32)
    return (jnp.concatenate([ones, cos, cos, pad], axis=-1), jnp.concatenate([zeros, sin, sin, pad], axis=-1))


def _token_mixer(xs, mod, lw, khat, conv_short, *, batch, n, tiles_per_group, fixed_group, rope_tabs,
                 k_ctx=None, v_ctx=None):
    pr = _inproj(xs, mod, lw, tiles_per_group=tiles_per_group, fixed_group=fixed_group, rope_tabs=rope_tabs,
                 want_q=True, want_hg=True)
    k3 = pr["k"].reshape(batch, n, -1)
    v3 = pr["v"].reshape(batch, n // VT_BLOCK, N_HEADS * HEAD_PAD, VT_BLOCK)
    if k_ctx is None:
        o_attn = _attention(pr["q"], k3, v3, None, None, batch)
    else:
        o_attn = _attention(pr["q"], k_ctx, v_ctx, k3, v3, batch)
    if conv_short:
        z, x0 = _hyena_pre(pr["p_hy"], lw["hy_short_w"], lw["hy_short_b"], batch, None)
        conv = _direct_conv(z, *khat, n)
    else:
        z, x0, zt = _hyena_pre(pr["p_hy"], lw["hy_short_w"], lw["hy_short_b"], batch, DFT_Q)
        conv = _long_conv(zt, khat)
    x_new = _merge(xs, mod, o_attn, conv, z, x0, pr["gates"], lw, tiles_per_group=tiles_per_group,
                   fixed_group=fixed_group)
    return x_new, k3, v3


def kernel(x, c, ctx, c_ctx, w_mod, b_mod, norm1_g, norm2_g, w_in, q_norm_g, kv_norm_g, w_uq, w_uk, w_uv, hy_short_w, hy_short_b, hy_w1, hy_b1, hy_w2, hy_b2, hy_w3, hy_freq, hy_decay, hy_bias, w_br_attn, w_br_hy, w_out, ffn_w_gate, ffn_w_up, ffn_w_down, moe_router, moe_w_gate, moe_w_up, moe_w_down, final_g):
    p = dict(w_in=w_in, norm1_g=norm1_g, norm2_g=norm2_g, q_norm_g=q_norm_g, kv_norm_g=kv_norm_g, w_uq=w_uq,
             w_uk=w_uk, w_uv=w_uv, hy_short_w=hy_short_w, hy_short_b=hy_short_b, hy_w1=hy_w1, hy_b1=hy_b1,
             hy_w2=hy_w2, hy_b2=hy_b2, hy_w3=hy_w3, hy_freq=hy_freq, hy_decay=hy_decay, hy_bias=hy_bias,
             w_br_attn=w_br_attn, w_br_hy=w_br_hy, w_out=w_out)
    batch, seq, d = x.shape
    ctx_len = ctx.shape[1]
    depth = w_mod.shape[0]
    ctx_group = batch
    cond8 = jnp.zeros((SUBLANES, d), F32).at[:batch].set(c).at[ctx_group].set(c_ctx)
    rope_tabs = _rope_tables(seq)
    lat_tiles = seq // ROW_TILE
    xs = x.reshape(batch * seq, d)
    cs = ctx.reshape(batch * ctx_len, d)
    bf = lambda v: v.astype(BF16)
    for layer in range(depth):
        last = layer == depth - 1
        lw = _layer_weights(p, layer)
        mod = _adaln(cond8, w_mod, b_mod, layer)
        khat_lat = _long_conv_spectrum(seq, lw)
        if last:
            pr = _inproj(cs, mod, lw, tiles_per_group=None, fixed_group=ctx_group, rope_tabs=None, want_q=False,
                         want_hg=False)
            k_ctx = pr["k"].reshape(batch, ctx_len, -1)
            v_ctx = pr["v"].reshape(batch, ctx_len // VT_BLOCK, N_HEADS * HEAD_PAD, VT_BLOCK)
        else:
            khat_ctx = _hyena_filter(ctx_len, ctx_len, lw)
            cs_mid, k_ctx, v_ctx = _token_mixer(cs, mod, lw, khat_ctx, True, batch=batch, n=ctx_len,
                                                tiles_per_group=None, fixed_group=ctx_group, rope_tabs=None)
        xs, _, _ = _token_mixer(xs, mod, lw, khat_lat, False, batch=batch, n=seq, tiles_per_group=lat_tiles,
                                fixed_group=None, rope_tabs=rope_tabs, k_ctx=k_ctx, v_ctx=v_ctx)
        i = layer // 2
        n2 = lw["norm2_g"]
        if layer % 2 == 0:
            assert not last
            wg, wu, wd = bf(ffn_w_gate[i])[None], bf(ffn_w_up[i])[None], bf(ffn_w_down[i])[None]
            xs = _ffn(xs, mod, n2, wg, wu, wd, tiles_per_group=lat_tiles, fixed_group=None)
            cs = _ffn(cs_mid, mod, n2, wg, wu, wd, tiles_per_group=None, fixed_group=ctx_group)
        else:
            assert last
            router = jnp.pad(moe_router[i], ((0, 0), (0, LANES - N_EXPERTS)))
            xs = _moe(xs, mod, n2, router, bf(moe_w_gate[i]), bf(moe_w_up[i]), bf(moe_w_down[i]),
                      final_g.reshape(1, d), lat_tiles)
    return xs.reshape(batch, seq, d)
```
